```python
import math
import jax
import jax.numpy as jnp
from jax import lax
import numpy as np

D_MODEL = 1024
BATCH = 4
SEQ = 4096
DEPTH = 2

GRID_W = 64
CTX_LEN = 256
EPS = 1e-6

DIFF_HEADS = 6
DIFF_DH = 64
DIFF_QK = DIFF_HEADS * 2 * DIFF_DH
DIFF_V = DIFF_HEADS * 2 * DIFF_DH
FOURIER_GROUPS = 4
FOURIER_GC = 64
FOURIER_W = FOURIER_GROUPS * FOURIER_GC
EVEN_IN = 2 * DIFF_QK + DIFF_V + FOURIER_W
EVEN_MIX = DIFF_V + FOURIER_W
Q_BLOCK = 128
ROPE_BASE = 10000.0
ROPE_PAIRS = DIFF_DH // 4

CHUNK = 128
SGU_GROUPS = 8
SGU_GC = 128
SGU_W = SGU_GROUPS * SGU_GC
ODD_IN = 2 * SGU_W

PEER_HEADS = 8
N_KEYS = 128
N_EXPERTS = N_KEYS * N_KEYS
PEER_TOPK = 16
PEER_DQ = 256
PEER_HALF = PEER_DQ // 2
PEER_BLOCK = 128

N_EVEN = (DEPTH + 1) // 2
N_ODD = DEPTH // 2

kernel_name = 'hybrid_diffattn_fourier_sgu_peer_dit'


def rmsnorm(t, g):
    tf = t.astype(jnp.float32)
    tf = tf * lax.rsqrt(jnp.mean(tf * tf, axis=-1, keepdims=True) + EPS)
    return (tf * g.astype(jnp.float32)).astype(t.dtype)


def axial_rope_tables(rows, dtype):
    r = jnp.repeat(jnp.arange(rows, dtype=jnp.float32), GRID_W)
    col = jnp.tile(jnp.arange(GRID_W, dtype=jnp.float32), rows)
    inv = ROPE_BASE ** (-jnp.arange(ROPE_PAIRS, dtype=jnp.float32) / ROPE_PAIRS)
    ar = r[:, None] * inv
    ac = col[:, None] * inv
    ang = jnp.concatenate([ar, ar, ac, ac], axis=-1)
    return (jnp.cos(ang).astype(dtype)[:, None, None, :],
            jnp.sin(ang).astype(dtype)[:, None, None, :])


def rope_axial(t, cos, sin):
    a, b, c, d = jnp.split(t, 4, axis=-1)
    rot = jnp.concatenate([-b, a, -d, c], axis=-1)
    return t * cos + rot * sin


def diff_attend(q, k, v, lam):
    s = jnp.einsum('bqhmd,bkhmd->bhmqk', q, k).astype(jnp.float32) * (DIFF_DH ** -0.5)
    p = jax.nn.softmax(s, axis=-1)
    a = p[:, :, 0] - lam * p[:, :, 1]
    return jnp.einsum('bhqk,bkhe->bqhe', a.astype(v.dtype), v)


def fourier_mix(f):
    B, L, _ = f.shape
    ff = f.astype(jnp.float32).reshape(B, L, FOURIER_GROUPS, FOURIER_GC)
    y = jnp.fft.fft2(ff, axes=(1, 3), norm='ortho').real
    return y.reshape(B, L, FOURIER_W).astype(f.dtype)


def even_mixer(h, hc, w_in, w_out, lam_p, head_g, layer_num, cos, sin, update_ctx):
    B, L, _ = h.shape
    C = hc.shape[1]
    lam_init = 0.8 - 0.6 * math.exp(-0.3 * (layer_num - 1))
    lp = lam_p.astype(jnp.float32)
    lam = jnp.exp(jnp.dot(lp[0], lp[1])) - jnp.exp(jnp.dot(lp[2], lp[3])) + lam_init
    q, k, v, f = jnp.split(h @ w_in, [DIFF_QK, 2 * DIFF_QK, 2 * DIFF_QK + DIFF_V], axis=-1)
    q = rope_axial(q.reshape(B, L, DIFF_HEADS, 2, DIFF_DH), cos, sin)
    k = rope_axial(k.reshape(B, L, DIFF_HEADS, 2, DIFF_DH), cos, sin)
    v = v.reshape(B, L, DIFF_HEADS, 2 * DIFF_DH)
    kc, vc = jnp.split(hc @ w_in[:, DIFF_QK:2 * DIFF_QK + DIFF_V], [DIFF_QK], axis=-1)
    kc = kc.reshape(B, C, DIFF_HEADS, 2, DIFF_DH)
    vc = vc.reshape(B, C, DIFF_HEADS, 2 * DIFF_DH)
    k_all = jnp.concatenate([kc, k], axis=1)
    v_all = jnp.concatenate([vc, v], axis=1)
    nb = L // Q_BLOCK
    qb = jnp.swapaxes(q.reshape(B, nb, Q_BLOCK, DIFF_HEADS, 2, DIFF_DH), 0, 1)
    o = lax.map(lambda qq: diff_attend(qq, k_all, v_all, lam), qb)
    o = jnp.swapaxes(o, 0, 1).reshape(B, L, DIFF_HEADS, 2 * DIFF_DH)
    o = (rmsnorm(o, head_g) * (1.0 - lam_init)).reshape(B, L, DIFF_V)
    y = jnp.concatenate([o, fourier_mix(f)], axis=-1) @ w_out
    if not update_ctx:
        return y, None
    qc = (hc @ w_in[:, :DIFF_QK]).reshape(B, C, DIFF_HEADS, 2, DIFF_DH)
    fc = hc @ w_in[:, 2 * DIFF_QK + DIFF_V:]
    oc = diff_attend(qc, kc, vc, lam)
    oc = (rmsnorm(oc, head_g) * (1.0 - lam_init)).reshape(B, C, DIFF_V)
    yc = jnp.concatenate([oc, fourier_mix(fc)], axis=-1) @ w_out
    return y, yc


def sgu_mixer(h, w_in, b_in, ng, ws, bs, w_out):
    B, L, _ = h.shape
    z = jax.nn.gelu(h @ w_in + b_in)
    u, v = jnp.split(z, 2, axis=-1)
    v = rmsnorm(v, ng).reshape(B, L // CHUNK, CHUNK, SGU_GROUPS, SGU_GC)
    s = jnp.einsum('gpq,bnqgc->bnpgc', ws, v) + bs.T[None, None, :, :, None]
    return (u * s.reshape(B, L, SGU_W)) @ w_out


def peer_mix(h, wq, keys, eu, ev):
    shape = h.shape
    tb = h.reshape(-1, PEER_BLOCK, shape[-1])

    def block(t):
        q = (t @ wq).reshape(PEER_BLOCK, PEER_HEADS, 2, PEER_HALF)
        s = jnp.einsum('thsd,hskd->thsk', q, keys).astype(jnp.float32)
        s1, i1 = lax.top_k(s[:, :, 0], PEER_TOPK)
        s2, i2 = lax.top_k(s[:, :, 1], PEER_TOPK)
        cand = (s1[..., :, None] + s2[..., None, :]).reshape(PEER_BLOCK, PEER_HEADS, PEER_TOPK * PEER_TOPK)
        sc, ci = lax.top_k(cand, PEER_TOPK)
        idx = (jnp.take_along_axis(i1, ci // PEER_TOPK, axis=-1) * N_KEYS
               + jnp.take_along_axis(i2, ci % PEER_TOPK, axis=-1))
        gate = jax.nn.softmax(sc, axis=-1)
        u = jnp.take(eu, idx, axis=0)
        act = jax.nn.gelu(jnp.einsum('td,thkd->thk', t, u).astype(jnp.float32))
        vv = jnp.take(ev, idx, axis=0)
        return jnp.einsum('thk,thkd->td', (gate * act).astype(t.dtype), vv)

    return lax.map(block, tb).reshape(shape)


def setup_inputs(seed: int = 0) -> dict:
    key = jax.random.key(seed)
    ks = jax.random.split(key, 24)
    D = D_MODEL

    def nrm(k, shape, s):
        return jax.random.normal(k, shape, jnp.float32) * s

    return {
        'x': nrm(ks[0], (BATCH, SEQ, D), 1.0),
        'c': nrm(ks[1], (BATCH, D), 1.0),
        'ctx': nrm(ks[2], (BATCH, CTX_LEN, D), 1.0),
        'c_ctx': nrm(ks[3], (D,), 1.0),
        'ada_w': nrm(ks[4], (DEPTH, D, 6 * D), 0.5 * D ** -0.5),
        'ada_b': nrm(ks[5], (DEPTH, 6 * D), 0.02),
        'norm1_g': 1.0 + nrm(ks[6], (DEPTH, D), 0.02),
        'norm2_g': 1.0 + nrm(ks[7], (DEPTH, D), 0.02),
        'final_g': 1.0 + nrm(ks[8], (D,), 0.02),
        'even_w_in': nrm(ks[9], (N_EVEN, D, EVEN_IN), D ** -0.5),
        'even_w_out': nrm(ks[10], (N_EVEN, EVEN_MIX, D), EVEN_MIX ** -0.5),
        'diff_lambda': nrm(ks[11], (N_EVEN, 4, DIFF_DH), 0.1),
        'diff_norm_g': 1.0 + nrm(ks[12], (N_EVEN, 2 * DIFF_DH), 0.02),
        'odd_w_in': nrm(ks[13], (N_ODD, D, ODD_IN), D ** -0.5),
        'odd_b_in': nrm(ks[14], (N_ODD, ODD_IN), 0.02),
        'sgu_norm_g': 1.0 + nrm(ks[15], (N_ODD, SGU_W), 0.02),
        'sgu_w': nrm(ks[16], (N_ODD, SGU_GROUPS, CHUNK, CHUNK), 0.5 * CHUNK ** -0.5),
        'sgu_b': 1.0 + nrm(ks[17], (N_ODD, SGU_GROUPS, CHUNK), 0.02),
        'odd_w_out': nrm(ks[18], (N_ODD, SGU_W, D), SGU_W ** -0.5),
        'peer_wq': nrm(ks[19], (DEPTH, D, PEER_HEADS * PEER_DQ), D ** -0.5),
        'peer_keys': nrm(ks[20], (DEPTH, PEER_HEADS, 2, N_KEYS, PEER_HALF), PEER_HALF ** -0.5),
        'peer_u': nrm(ks[21], (DEPTH, N_EXPERTS, D), D ** -0.5),
        'peer_v': nrm(ks[22], (DEPTH, N_EXPERTS, D), 0.5),
    }


def reference(x, c, ctx, c_ctx, ada_w, ada_b, norm1_g, norm2_g, final_g,
              even_w_in, even_w_out, diff_lambda, diff_norm_g,
              odd_w_in, odd_b_in, sgu_norm_g, sgu_w, sgu_b, odd_w_out,
              peer_wq, peer_keys, peer_u, peer_v):
    L = x.shape[1]
    rows = L // GRID_W
    cos, sin = axial_rope_tables(rows, x.dtype)
    c_s = jax.nn.silu(c)
    cc_s = jax.nn.silu(c_ctx)
    for i in range(DEPTH):
        even = i % 2 == 0
        update_ctx = any(j % 2 == 0 for j in range(i + 1, DEPTH))
        sh1, sc1, g1, sh2, sc2, g2 = jnp.split((c_s @ ada_w[i] + ada_b[i])[:, None, :], 6, axis=-1)
        h = rmsnorm(x, norm1_g[i]) * (1 + sc1) + sh1
        if even or update_ctx:
            csh1, csc1, cg1, csh2, csc2, cg2 = jnp.split(cc_s @ ada_w[i] + ada_b[i], 6)
            hc = rmsnorm(ctx, norm1_g[i]) * (1 + csc1) + csh1
        if even:
            e = i // 2
            y, yc = even_mixer(h, hc, even_w_in[e], even_w_out[e], diff_lambda[e],
                               diff_norm_g[e], i + 1, cos, sin, update_ctx)
        else:
            od = i // 2
            y = sgu_mixer(h, odd_w_in[od], odd_b_in[od], sgu_norm_g[od], sgu_w[od], sgu_b[od], odd_w_out[od])
            yc = (sgu_mixer(hc, odd_w_in[od], odd_b_in[od], sgu_norm_g[od], sgu_w[od], sgu_b[od], odd_w_out[od])
                  if update_ctx else None)
        x = x + g1 * y
        x = x + g2 * peer_mix(rmsnorm(x, norm2_g[i]) * (1 + sc2) + sh2,
                              peer_wq[i], peer_keys[i], peer_u[i], peer_v[i])
        if update_ctx:
            ctx = ctx + cg1 * yc
            ctx = ctx + cg2 * peer_mix(rmsnorm(ctx, norm2_g[i]) * (1 + csc2) + csh2,
                                       peer_wq[i], peer_keys[i], peer_u[i], peer_v[i])
    return rmsnorm(x, final_g)
```

```python
import functools
import math

import numpy as np
import jax
import jax.numpy as jnp
from jax import lax
from jax.experimental import pallas as pl
from jax.experimental.pallas import tpu as pltpu

F32 = jnp.float32
BF16 = jnp.bfloat16

D_MODEL = 1024
BATCH = 4
SEQ = 4096
GRID_W = 64
CTX_LEN = 256
SEQ_ALL = CTX_LEN + SEQ
EPS = 1e-6

DIFF_HEADS = 6
DIFF_DH = 64
HEAD_W = 2 * DIFF_DH
DIFF_QK = DIFF_HEADS * HEAD_W
FOURIER_GROUPS = 4
FOURIER_GC = 64
FOURIER_W = FOURIER_GROUPS * FOURIER_GC
EVEN_IN = 3 * DIFF_QK + FOURIER_W
ROPE_BASE = 10000.0
ROPE_PAIRS = DIFF_DH // 4
LAM_INIT_L1 = 0.8 - 0.6 * math.exp(-0.3 * 0.0)

CHUNK = 128
SGU_GROUPS = 8
SGU_W = SGU_GROUPS * CHUNK

PEER_HEADS = 8
N_KEYS = 128
N_EXPERTS = N_KEYS * N_KEYS
PEER_TOPK = 16
PEER_HALF = 128

VMEM_LIMIT = 56 * 1024 * 1024

NT_DIMS = (((1,), (1,)), ((), ()))


def _rms(t, g):
    return t * lax.rsqrt(jnp.mean(t * t, axis=-1, keepdims=True) + EPS) * g


def _gelu(x):
    cdf = 0.5 * (1.0 + jnp.tanh(math.sqrt(2.0 / math.pi) * (x + 0.044715 * (x * x * x))))
    return x * cdf


def _params(*sem):
    return pltpu.CompilerParams(dimension_semantics=sem, vmem_limit_bytes=VMEM_LIMIT)


def _ada_kernel(c_ref, w_ref, b_ref, o_ref):
    c = c_ref[...]
    s = c * jax.nn.sigmoid(c)
    o_ref[...] = jnp.dot(s, w_ref[...], preferred_element_type=F32) + b_ref[...]


def _ada(cvec, ada_w, ada_b):
    depth = ada_w.shape[0]
    tn = 1536
    return pl.pallas_call(
        _ada_kernel,
        grid=(depth, 6 * D_MODEL // tn),
        in_specs=[
            pl.BlockSpec((8, D_MODEL), lambda l, j: (0, 0)),
            pl.BlockSpec((None, D_MODEL, tn), lambda l, j: (l, 0, j)),
            pl.BlockSpec((None, 1, tn), lambda l, j: (l, 0, j)),
        ],
        out_specs=pl.BlockSpec((None, 8, tn), lambda l, j: (l, 0, j)),
        out_shape=jax.ShapeDtypeStruct((depth, 8, 6 * D_MODEL), F32),
        compiler_params=_params("arbitrary", "arbitrary"),
        name="ada",
    )(cvec, ada_w, ada_b.reshape(depth, 1, 6 * D_MODEL))


IN_TM = 256
TILES_PER_SEQ = SEQ_ALL // IN_TM


def _inproj_kernel(x_ref, g_ref, sc_ref, sh_ref, w_ref, cos_ref, sa_ref, sb_ref, o_ref):
    x = x_ref[...]
    h = _rms(x, g_ref[...]) * (1.0 + sc_ref[...]) + sh_ref[...]
    r = jnp.dot(h.astype(BF16), w_ref[...], preferred_element_type=F32)
    cos, sa, sb = cos_ref[...], sa_ref[...], sb_ref[...]
    for cb in range(2 * DIFF_HEADS):
        t = r[:, cb * HEAD_W:(cb + 1) * HEAD_W]
        rot = t * cos + pltpu.roll(t, HEAD_W - 16, 1) * sa + pltpu.roll(t, 16, 1) * sb
        if cb < DIFF_HEADS:
            rot = rot * (DIFF_DH ** -0.5)
        o_ref[:, cb * HEAD_W:(cb + 1) * HEAD_W] = rot.astype(BF16)
    o_ref[:, 2 * DIFF_QK:] = r[:, 2 * DIFF_QK:].astype(BF16)


def _inproj(xin, g, sc, sh, w, cos, sa, sb):
    rows = xin.shape[0]
    mod_spec = pl.BlockSpec((None, None, 1, D_MODEL),
                            lambda i: (i // TILES_PER_SEQ, jnp.where(i % TILES_PER_SEQ == 0, 1, 0), 0, 0))
    rope_spec = pl.BlockSpec((IN_TM, HEAD_W), lambda i: (i % TILES_PER_SEQ, 0))
    return pl.pallas_call(
        _inproj_kernel,
        grid=(rows // IN_TM,),
        in_specs=[
            pl.BlockSpec((IN_TM, D_MODEL), lambda i: (i, 0)),
            pl.BlockSpec((1, D_MODEL), lambda i: (0, 0)),
            mod_spec, mod_spec,
            pl.BlockSpec((D_MODEL, EVEN_IN), lambda i: (0, 0)),
            rope_spec, rope_spec, rope_spec,
        ],
        out_specs=pl.BlockSpec((IN_TM, EVEN_IN), lambda i: (i, 0)),
        out_shape=jax.ShapeDtypeStruct((rows, EVEN_IN), BF16),
        compiler_params=_params("arbitrary"),
        name="inproj",
    )(xin, g, sc, sh, w, cos, sa, sb)


def _rope_tables():
    rows = SEQ // GRID_W
    r = jnp.repeat(jnp.arange(rows, dtype=F32), GRID_W)
    col = jnp.tile(jnp.arange(GRID_W, dtype=F32), rows)
    inv = ROPE_BASE ** (-jnp.arange(ROPE_PAIRS, dtype=F32) / ROPE_PAIRS)
    ar = r[:, None] * inv
    ac = col[:, None] * inv
    ang = jnp.concatenate([ar, ar, ac, ac] * 2, axis=-1)
    cos, sin = jnp.cos(ang), jnp.sin(ang)
    even = ((np.arange(HEAD_W) // ROPE_PAIRS) % 2 == 0)[None, :]
    sa = jnp.where(even, -sin, 0.0)
    sb = jnp.where(even, 0.0, sin)
    ident = jnp.ones((CTX_LEN, HEAD_W), F32)
    zero = jnp.zeros((CTX_LEN, HEAD_W), F32)
    return (jnp.concatenate([ident, cos], 0), jnp.concatenate([zero, sa], 0),
            jnp.concatenate([zero, sb], 0))


ATT_TQ = 256


def _attn_kernel(lam_ref, g_ref, q_ref, k_ref, v_ref, o_ref):
    q = q_ref[...].astype(F32)
    lane = lax.broadcasted_iota(jnp.int32, (1, HEAD_W), 1)
    k = k_ref[...]
    v = v_ref[...]

    def soft(qm):
        s = lax.dot_general(qm.astype(BF16), k, NT_DIMS, preferred_element_type=F32)
        m = jnp.max(s, axis=-1, keepdims=True)
        e = jnp.exp(s - m)
        z = jnp.sum(e, axis=-1, keepdims=True)
        return jnp.dot(e.astype(BF16), v, preferred_element_type=F32) / z

    lp = lam_ref[...]
    lam = (jnp.exp(jnp.sum(lp[0:1] * lp[1:2], axis=-1, keepdims=True))
           - jnp.exp(jnp.sum(lp[2:3] * lp[3:4], axis=-1, keepdims=True)) + LAM_INIT_L1)
    o = soft(jnp.where(lane < DIFF_DH, q, 0.0)) - lam * soft(jnp.where(lane >= DIFF_DH, q, 0.0))
    o_ref[...] = (_rms(o, g_ref[...]) * (1.0 - LAM_INIT_L1)).astype(BF16)


def _attn(qkvf, lam_p, head_g):
    nq = SEQ // ATT_TQ
    q_off = CTX_LEN // ATT_TQ
    return pl.pallas_call(
        _attn_kernel,
        grid=(BATCH, DIFF_HEADS, nq),
        in_specs=[
            pl.BlockSpec((4, DIFF_DH), lambda b, h, i: (0, 0)),
            pl.BlockSpec((1, HEAD_W), lambda b, h, i: (0, 0)),
            pl.BlockSpec((None, ATT_TQ, HEAD_W), lambda b, h, i: (b, i + q_off, h)),
            pl.BlockSpec((None, SEQ_ALL, HEAD_W), lambda b, h, i: (b, 0, DIFF_HEADS + h)),
            pl.BlockSpec((None, SEQ_ALL, HEAD_W), lambda b, h, i: (b, 0, 2 * DIFF_HEADS + h)),
        ],
        out_specs=pl.BlockSpec((None, ATT_TQ, HEAD_W), lambda b, h, i: (b, i, h)),
        out_shape=jax.ShapeDtypeStruct((BATCH, SEQ, DIFF_QK), BF16),
        compiler_params=_params("arbitrary", "arbitrary", "arbitrary"),
        name="diff_attn",
    )(lam_p, head_g, qkvf, qkvf, qkvf)


def _channel_dft():
    n = np.arange(FOURIER_GC)
    ang = 2.0 * np.pi * np.outer(n, n) / FOURIER_GC
    eye = np.eye(FOURIER_GROUPS)
    scale = FOURIER_GC ** -0.5
    return np.concatenate([np.kron(eye, np.cos(ang)), np.kron(eye, np.sin(ang))], axis=1) * scale


def _seq_dft():
    k = jnp.arange(SEQ, dtype=jnp.int32)[:, None]
    n = jnp.arange(GRID_W, dtype=jnp.int32)[None, :]
    a = ((k * n) % GRID_W).astype(F32) * (2.0 * math.pi / GRID_W)
    b = ((k * n) % SEQ).astype(F32) * (2.0 * math.pi / SEQ)
    ca, sa, cb, sb = jnp.cos(a)[:, :, None], jnp.sin(a)[:, :, None], jnp.cos(b)[:, None, :], jnp.sin(b)[:, None, :]
    scale = SEQ ** -0.5
    cos = ((ca * cb - sa * sb) * scale).reshape(SEQ, SEQ)
    sin = ((sa * cb + ca * sb) * scale).reshape(SEQ, SEQ)
    return jnp.concatenate([cos, -sin], axis=1).astype(BF16)


def _fourier_chan_kernel(f_ref, bd_ref, o_ref):
    f = f_ref[CTX_LEN:, :]
    uv = jnp.dot(f, bd_ref[...], preferred_element_type=F32)
    o_ref[:SEQ, :] = uv[:, :FOURIER_W].astype(BF16)
    o_ref[SEQ:, :] = uv[:, FOURIER_W:].astype(BF16)


def _fourier_seq_kernel(t_ref, uv_ref, o_ref):
    o_ref[...] = jnp.dot(t_ref[...], uv_ref[...], preferred_element_type=F32).astype(BF16)


FOURIER_TM = 512


def _fourier(qkvf, seq_dft):
    bd = jnp.asarray(_channel_dft(), dtype=BF16)
    uv = pl.pallas_call(
        _fourier_chan_kernel,
        grid=(BATCH,),
        in_specs=[
            pl.BlockSpec((None, SEQ_ALL, FOURIER_W), lambda b: (b, 0, 3 * DIFF_QK // FOURIER_W)),
            pl.BlockSpec((FOURIER_W, 2 * FOURIER_W), lambda b: (0, 0)),
        ],
        out_specs=pl.BlockSpec((None, 2 * SEQ, FOURIER_W), lambda b: (b, 0, 0)),
        out_shape=jax.ShapeDtypeStruct((BATCH, 2 * SEQ, FOURIER_W), BF16),
        compiler_params=_params("arbitrary"),
        name="fourier_chan",
    )(qkvf, bd)
    return pl.pallas_call(
        _fourier_seq_kernel,
        grid=(SEQ // FOURIER_TM, BATCH),
        in_specs=[
            pl.BlockSpec((FOURIER_TM, 2 * SEQ), lambda i, b: (i, 0)),
            pl.BlockSpec((None, 2 * SEQ, FOURIER_W), lambda i, b: (b, 0, 0)),
        ],
        out_specs=pl.BlockSpec((None, FOURIER_TM, FOURIER_W), lambda i, b: (b, i, 0)),
        out_shape=jax.ShapeDtypeStruct((BATCH, SEQ, FOURIER_W), BF16),
        compiler_params=_params("arbitrary", "arbitrary"),
        name="fourier_seq",
    )(seq_dft, uv)


OUT_TM = 512


def _outproj_kernel(o_ref, f_ref, w_ref, x_ref, g_ref, out_ref):
    y = jnp.dot(o_ref[...], w_ref[:DIFF_QK, :], preferred_element_type=F32)
    y = y + jnp.dot(f_ref[...], w_ref[DIFF_QK:, :], preferred_element_type=F32)
    out_ref[...] = x_ref[...] + g_ref[...] * y


def _outproj(o, fm, w, x2d, g1):
    rows = x2d.shape[0]
    per_b = SEQ // OUT_TM
    return pl.pallas_call(
        _outproj_kernel,
        grid=(rows // OUT_TM,),
        in_specs=[
            pl.BlockSpec((OUT_TM, DIFF_QK), lambda i: (i, 0)),
            pl.BlockSpec((OUT_TM, FOURIER_W), lambda i: (i, 0)),
            pl.BlockSpec((D_MODEL, D_MODEL), lambda i: (0, 0)),
            pl.BlockSpec((OUT_TM, D_MODEL), lambda i: (i, 0)),
            pl.BlockSpec((None, 1, D_MODEL), lambda i: (i // per_b, 0, 0)),
        ],
        out_specs=pl.BlockSpec((OUT_TM, D_MODEL), lambda i: (i, 0)),
        out_shape=jax.ShapeDtypeStruct((rows, D_MODEL), F32),
        compiler_params=_params("arbitrary"),
        name="outproj",
    )(o, fm, w, x2d, g1)


SGU_TM = 256


def _sgu_kernel(x_ref, g_ref, sc_ref, sh_ref, win_ref, bin_ref, ng_ref, ws_ref, bs_ref, wout_ref, g1_ref,
                out_ref, us_ref):
    x = x_ref[...]
    h = _rms(x, g_ref[...]) * (1.0 + sc_ref[...]) + sh_ref[...]
    z = _gelu(jnp.dot(h.astype(BF16), win_ref[...], preferred_element_type=F32) + bin_ref[...])
    u = z[:, :SGU_W]
    v = _rms(z[:, SGU_W:], ng_ref[...]).astype(BF16)
    for ck in range(SGU_TM // CHUNK):
        rows = slice(ck * CHUNK, (ck + 1) * CHUNK)
        for g in range(SGU_GROUPS):
            cols = slice(g * CHUNK, (g + 1) * CHUNK)
            s = jnp.dot(ws_ref[g], v[rows, cols], preferred_element_type=F32) + bs_ref[g]
            us_ref[rows, cols] = (u[rows, cols] * s).astype(BF16)
    y = jnp.dot(us_ref[...], wout_ref[...], preferred_element_type=F32)
    out_ref[...] = x + g1_ref[...] * y


def _sgu(x2d, g, sc, sh, win, b_in, ng, ws, bs, wout, g1):
    rows = x2d.shape[0]
    per_b = SEQ // SGU_TM
    vec = lambda n: pl.BlockSpec((1, n), lambda i: (0, 0))
    per_batch = pl.BlockSpec((None, 1, D_MODEL), lambda i: (i // per_b, 0, 0))
    return pl.pallas_call(
        _sgu_kernel,
        grid=(rows // SGU_TM,),
        in_specs=[
            pl.BlockSpec((SGU_TM, D_MODEL), lambda i: (i, 0)),
            vec(D_MODEL), per_batch, per_batch,
            pl.BlockSpec((D_MODEL, 2 * SGU_W), lambda i: (0, 0)),
            vec(2 * SGU_W), vec(SGU_W),
            pl.BlockSpec((SGU_GROUPS, CHUNK, CHUNK), lambda i: (0, 0, 0)),
            pl.BlockSpec((SGU_GROUPS, CHUNK, CHUNK), lambda i: (0, 0, 0)),
            pl.BlockSpec((SGU_W, D_MODEL), lambda i: (0, 0)),
            per_batch,
        ],
        out_specs=pl.BlockSpec((SGU_TM, D_MODEL), lambda i: (i, 0)),
        out_shape=jax.ShapeDtypeStruct((rows, D_MODEL), F32),
        scratch_shapes=[pltpu.VMEM((SGU_TM, SGU_W), BF16)],
        compiler_params=_params("arbitrary"),
        name="sgu",
    )(x2d, g, sc, sh, win, b_in, ng, ws, bs, wout, g1)


ROUTE_TM = 256
PAIR_CELLS = [(a, b) for a in range(PEER_TOPK + 1) for b in range(PEER_TOPK + 1)
              if (a + 1) * (b + 1) <= PEER_TOPK + 1]


def _route_kernel(x_ref, g_ref, sc_ref, sh_ref, wq_ref, keys_ref,
                  t_ref, n1_ref, a_ref, s2_ref, b_ref, s_scr):
    x = x_ref[...]
    t = (_rms(x, g_ref[...]) * (1.0 + sc_ref[...]) + sh_ref[...]).astype(BF16)
    t_ref[...] = t
    q = jnp.dot(t, wq_ref[...], preferred_element_type=F32)
    neg = jnp.float32(-jnp.inf)
    tops = []
    for hs in range(2 * PEER_HEADS):
        qh = q[:, hs * PEER_HALF:(hs + 1) * PEER_HALF].astype(BF16)
        st = lax.dot_general(keys_ref[hs], qh, NT_DIMS, preferred_element_type=F32)
        s_scr[hs] = st
        vals = []
        for r in range(PEER_TOPK + 1):
            m = jnp.max(st, axis=0, keepdims=True)
            vals.append(m)
            if r < PEER_TOPK:
                st = jnp.where(st == m, neg, st)
        tops.append(vals)
    v1 = [jnp.concatenate([tops[2 * h][r] for h in range(PEER_HEADS)], axis=0) for r in range(PEER_TOPK + 1)]
    v2 = [jnp.concatenate([tops[2 * h + 1][r] for h in range(PEER_HEADS)], axis=0) for r in range(PEER_TOPK + 1)]
    cand = [v1[a] + v2[b] for a, b in PAIR_CELLS]
    cur = cand
    kth = None
    for r in range(PEER_TOPK + 1):
        m = functools.reduce(jnp.maximum, cur)
        if r == PEER_TOPK - 1:
            kth = m
        if r < PEER_TOPK:
            cur = [jnp.where(c == m, neg, c) for c in cur]
    thr = 0.5 * (kth + m)
    top = v1[0] + v2[0]
    z = functools.reduce(lambda p, c: p + jnp.where(c >= thr, jnp.exp(c - top), 0.0), cand, jnp.zeros_like(top))
    rz = 1.0 / z
    for h in range(PEER_HEADS):
        s1 = s_scr[2 * h]
        s2 = s_scr[2 * h + 1]
        n1_ref[h] = thr[h:h + 1] - s1
        a_ref[h] = jnp.exp(s1 - v1[0][h:h + 1]) * rz[h:h + 1]
        s2_ref[h] = s2
        b_ref[h] = jnp.exp(s2 - v2[0][h:h + 1])


def _route(x2d, g, sc, sh, wq, keys):
    rows = x2d.shape[0]
    per_b = SEQ // ROUTE_TM
    per_batch = pl.BlockSpec((None, 1, D_MODEL), lambda i: (i // per_b, 0, 0))
    tbl = pl.BlockSpec((PEER_HEADS, N_KEYS, ROUTE_TM), lambda i: (0, 0, i))
    tbl_shape = jax.ShapeDtypeStruct((PEER_HEADS, N_KEYS, rows), F32)
    return pl.pallas_call(
        _route_kernel,
        grid=(rows // ROUTE_TM,),
        in_specs=[
            pl.BlockSpec((ROUTE_TM, D_MODEL), lambda i: (i, 0)),
            pl.BlockSpec((1, D_MODEL), lambda i: (0, 0)),
            per_batch, per_batch,
            pl.BlockSpec((D_MODEL, 2 * PEER_HEADS * PEER_HALF), lambda i: (0, 0)),
            pl.BlockSpec((2 * PEER_HEADS, N_KEYS, PEER_HALF), lambda i: (0, 0, 0)),
        ],
        out_specs=[pl.BlockSpec((ROUTE_TM, D_MODEL), lambda i: (i, 0)), tbl, tbl, tbl, tbl],
        out_shape=[jax.ShapeDtypeStruct((rows, D_MODEL), BF16), tbl_shape, tbl_shape, tbl_shape, tbl_shape],
        scratch_shapes=[pltpu.VMEM((2 * PEER_HEADS, N_KEYS, ROUTE_TM), F32)],
        compiler_params=_params("arbitrary"),
        name="peer_route",
    )(x2d, g, sc, sh, wq, keys)


DENSE_TM = 512
DENSE_TE = 1024


def _dense_kernel(t_ref, n1_ref, a_ref, s2_ref, b_ref, eu_ref, evt_ref, x_ref, g2_ref, fg_ref,
                  out_ref, acc_ref, w_ref, *, final_norm):
    e = pl.program_id(1)

    @pl.when(e == 0)
    def _():
        acc_ref[...] = jnp.zeros_like(acc_ref)

    t = t_ref[...]
    rows_per_step = DENSE_TE // N_KEYS
    for ii in range(rows_per_step):
        i = e * rows_per_step + ii
        act = lax.dot_general(eu_ref[ii * N_KEYS:(ii + 1) * N_KEYS, :], t, NT_DIMS,
                              preferred_element_type=F32)
        w = jnp.zeros_like(act)
        for h in range(PEER_HEADS):
            n1 = n1_ref[h, pl.ds(i, 1), :]
            a = a_ref[h, pl.ds(i, 1), :]
            w = w + jnp.where(s2_ref[h] >= n1, a * b_ref[h], 0.0)
        w_ref[ii * N_KEYS:(ii + 1) * N_KEYS, :] = (_gelu(act) * w).astype(BF16)
    acc_ref[...] += jnp.dot(evt_ref[...], w_ref[...], preferred_element_type=F32)

    @pl.when(e == pl.num_programs(1) - 1)
    def _():
        y = x_ref[...] + g2_ref[...] * acc_ref[...].T
        if final_norm:
            y = _rms(y, fg_ref[...])
        out_ref[...] = y


def _dense(t, n1, a, s2, b, eu, evt, x2d, g2, fg, final_norm):
    rows = x2d.shape[0]
    per_b = SEQ // DENSE_TM
    tbl = pl.BlockSpec((PEER_HEADS, N_KEYS, DENSE_TM), lambda i, e: (0, 0, i))
    return pl.pallas_call(
        functools.partial(_dense_kernel, final_norm=final_norm),
        grid=(rows // DENSE_TM, N_EXPERTS // DENSE_TE),
        in_specs=[
            pl.BlockSpec((DENSE_TM, D_MODEL), lambda i, e: (i, 0)),
            tbl, tbl, tbl, tbl,
            pl.BlockSpec((DENSE_TE, D_MODEL), lambda i, e: (e, 0)),
            pl.BlockSpec((D_MODEL, DENSE_TE), lambda i, e: (0, e)),
            pl.BlockSpec((DENSE_TM, D_MODEL), lambda i, e: (i, 0)),
            pl.BlockSpec((None, 1, D_MODEL), lambda i, e: (i // per_b, 0, 0)),
            pl.BlockSpec((1, D_MODEL), lambda i, e: (0, 0)),
        ],
        out_specs=pl.BlockSpec((DENSE_TM, D_MODEL), lambda i, e: (i, 0)),
        out_shape=jax.ShapeDtypeStruct((rows, D_MODEL), F32),
        scratch_shapes=[pltpu.VMEM((D_MODEL, DENSE_TM), F32), pltpu.VMEM((DENSE_TE, DENSE_TM), BF16)],
        compiler_params=_params("arbitrary", "arbitrary"),
        name="peer_dense",
    )(t, n1, a, s2, b, eu, evt, x2d, g2, fg)


def _peer(x2d, g, sc, sh, g2, wq, keys, eu, ev, fg, final_norm):
    wq_b = wq.astype(BF16)
    keys_b = keys.reshape(2 * PEER_HEADS, N_KEYS, PEER_HALF).astype(BF16)
    t, n1, a, s2, b = _route(x2d, g, sc, sh, wq_b, keys_b)
    return _dense(t, n1, a, s2, b, eu.astype(BF16), ev.T.astype(BF16), x2d, g2, fg, final_norm)


def kernel(x, c, ctx, c_ctx, ada_w, ada_b, norm1_g, norm2_g, final_g, even_w_in, even_w_out, diff_lambda,
           diff_norm_g, odd_w_in, odd_b_in, sgu_norm_g, sgu_w, sgu_b, odd_w_out, peer_wq, peer_keys, peer_u, peer_v):
    cvec = jnp.concatenate([c, c_ctx[None, :], jnp.zeros((8 - BATCH - 1, D_MODEL), F32)], axis=0)
    mod = _ada(cvec, ada_w, ada_b).reshape(2, 8, 6, D_MODEL)
    row = lambda v: v.reshape(1, -1)
    per_batch = lambda l, j: mod[l, :BATCH, j].reshape(BATCH, 1, D_MODEL)
    fg = row(final_g)

    both = lambda j: jnp.stack([mod[0, :BATCH, j], jnp.broadcast_to(mod[0, BATCH, j], (BATCH, D_MODEL))],
                               axis=1).reshape(BATCH, 2, 1, D_MODEL)
    xin = jnp.concatenate([ctx, x], axis=1).reshape(BATCH * SEQ_ALL, D_MODEL)
    cos, sa, sb = _rope_tables()
    qkvf = _inproj(xin, row(norm1_g[0]), both(1), both(0), even_w_in[0].astype(BF16), cos, sa, sb)
    qkvf = qkvf.reshape(BATCH, SEQ_ALL, EVEN_IN)
    o = _attn(qkvf, diff_lambda[0], row(diff_norm_g[0]))
    fm = _fourier(qkvf, _seq_dft())
    x2d = x.reshape(BATCH * SEQ, D_MODEL)
    x2d = _outproj(o.reshape(BATCH * SEQ, DIFF_QK), fm.reshape(BATCH * SEQ, FOURIER_W),
                   even_w_out[0].astype(BF16), x2d, per_batch(0, 2))
    x2d = _peer(x2d, row(norm2_g[0]), per_batch(0, 4), per_batch(0, 3), per_batch(0, 5),
                peer_wq[0], peer_keys[0], peer_u[0], peer_v[0], fg, False)

    bs = jnp.broadcast_to(sgu_b[0][:, :, None], (SGU_GROUPS, CHUNK, CHUNK))
    x2d = _sgu(x2d, row(norm1_g[1]), per_batch(1, 1), per_batch(1, 0), odd_w_in[0].astype(BF16),
               row(odd_b_in[0]), row(sgu_norm_g[0]), sgu_w[0].astype(BF16), bs, odd_w_out[0].astype(BF16),
               per_batch(1, 2))
    x2d = _peer(x2d, row(norm2_g[1]), per_batch(1, 4), per_batch(1, 3), per_batch(1, 5),
                peer_wq[1], peer_keys[1], peer_u[1], peer_v[1], fg, True)
    return x2d.reshape(BATCH, SEQ, D_MODEL)
```

```python
import functools
import math

import numpy as np
import jax
import jax.numpy as jnp
from jax import lax
from jax.experimental import pallas as pl
from jax.experimental.pallas import tpu as pltpu

F32 = jnp.float32
BF16 = jnp.bfloat16

D_MODEL = 1024
BATCH = 4
SEQ = 4096
GRID_W = 64
CTX_LEN = 256
SEQ_ALL = CTX_LEN + SEQ
EPS = 1e-6

DIFF_HEADS = 6
DIFF_DH = 64
HEAD_W = 2 * DIFF_DH
DIFF_QK = DIFF_HEADS * HEAD_W
FOURIER_GROUPS = 4
FOURIER_GC = 64
FOURIER_W = FOURIER_GROUPS * FOURIER_GC
EVEN_IN = 3 * DIFF_QK + FOURIER_W
ROPE_BASE = 10000.0
ROPE_PAIRS = DIFF_DH // 4
LAM_INIT_L1 = 0.8 - 0.6 * math.exp(-0.3 * 0.0)

CHUNK = 128
SGU_GROUPS = 8
SGU_W = SGU_GROUPS * CHUNK

PEER_HEADS = 8
N_KEYS = 128
N_EXPERTS = N_KEYS * N_KEYS
PEER_TOPK = 16
PEER_HALF = 128

VMEM_LIMIT = 56 * 1024 * 1024

NT_DIMS = (((1,), (1,)), ((), ()))


def _rms(t, g):
    return t * lax.rsqrt(jnp.mean(t * t, axis=-1, keepdims=True) + EPS) * g


def _gelu(x):
    cdf = 0.5 * (1.0 + jnp.tanh(math.sqrt(2.0 / math.pi) * (x + 0.044715 * (x * x * x))))
    return x * cdf


def _params(*sem):
    return pltpu.CompilerParams(dimension_semantics=sem, vmem_limit_bytes=VMEM_LIMIT)


def _ada_kernel(c_ref, w_ref, b_ref, o_ref):
    c = c_ref[...]
    s = c * jax.nn.sigmoid(c)
    o_ref[...] = jnp.dot(s, w_ref[...], preferred_element_type=F32) + b_ref[...]


def _ada(cvec, ada_w, ada_b):
    depth = ada_w.shape[0]
    tn = 1536
    return pl.pallas_call(
        _ada_kernel,
        grid=(depth, 6 * D_MODEL // tn),
        in_specs=[
            pl.BlockSpec((8, D_MODEL), lambda l, j: (0, 0)),
            pl.BlockSpec((None, D_MODEL, tn), lambda l, j: (l, 0, j)),
            pl.BlockSpec((None, 1, tn), lambda l, j: (l, 0, j)),
        ],
        out_specs=pl.BlockSpec((None, 8, tn), lambda l, j: (l, 0, j)),
        out_shape=jax.ShapeDtypeStruct((depth, 8, 6 * D_MODEL), F32),
        compiler_params=_params("arbitrary", "arbitrary"),
        name="ada",
    )(cvec, ada_w, ada_b.reshape(depth, 1, 6 * D_MODEL))


IN_TM = 256
TILES_PER_SEQ = SEQ_ALL // IN_TM


def _inproj_kernel(x_ref, g_ref, sc_ref, sh_ref, w_ref, cos_ref, sa_ref, sb_ref, o_ref):
    x = x_ref[...]
    h = _rms(x, g_ref[...]) * (1.0 + sc_ref[...]) + sh_ref[...]
    r = jnp.dot(h.astype(BF16), w_ref[...], preferred_element_type=F32)
    cos, sa, sb = cos_ref[...], sa_ref[...], sb_ref[...]
    for cb in range(2 * DIFF_HEADS):
        t = r[:, cb * HEAD_W:(cb + 1) * HEAD_W]
        rot = t * cos + pltpu.roll(t, HEAD_W - 16, 1) * sa + pltpu.roll(t, 16, 1) * sb
        if cb < DIFF_HEADS:
            rot = rot * (DIFF_DH ** -0.5)
        o_ref[:, cb * HEAD_W:(cb + 1) * HEAD_W] = rot.astype(BF16)
    o_ref[:, 2 * DIFF_QK:] = r[:, 2 * DIFF_QK:].astype(BF16)


def _inproj(xin, g, sc, sh, w, cos, sa, sb):
    rows = xin.shape[0]
    mod_spec = pl.BlockSpec((None, None, 1, D_MODEL),
                            lambda i: (i // TILES_PER_SEQ, jnp.where(i % TILES_PER_SEQ == 0, 1, 0), 0, 0))
    rope_spec = pl.BlockSpec((IN_TM, HEAD_W), lambda i: (i % TILES_PER_SEQ, 0))
    return pl.pallas_call(
        _inproj_kernel,
        grid=(rows // IN_TM,),
        in_specs=[
            pl.BlockSpec((IN_TM, D_MODEL), lambda i: (i, 0)),
            pl.BlockSpec((1, D_MODEL), lambda i: (0, 0)),
            mod_spec, mod_spec,
            pl.BlockSpec((D_MODEL, EVEN_IN), lambda i: (0, 0)),
            rope_spec, rope_spec, rope_spec,
        ],
        out_specs=pl.BlockSpec((IN_TM, EVEN_IN), lambda i: (i, 0)),
        out_shape=jax.ShapeDtypeStruct((rows, EVEN_IN), BF16),
        compiler_params=_params("arbitrary"),
        name="inproj",
    )(xin, g, sc, sh, w, cos, sa, sb)


def _rope_tables():
    rows = SEQ // GRID_W
    r = jnp.repeat(jnp.arange(rows, dtype=F32), GRID_W)
    col = jnp.tile(jnp.arange(GRID_W, dtype=F32), rows)
    inv = ROPE_BASE ** (-jnp.arange(ROPE_PAIRS, dtype=F32) / ROPE_PAIRS)
    ar = r[:, None] * inv
    ac = col[:, None] * inv
    ang = jnp.concatenate([ar, ar, ac, ac] * 2, axis=-1)
    cos, sin = jnp.cos(ang), jnp.sin(ang)
    even = ((np.arange(HEAD_W) // ROPE_PAIRS) % 2 == 0)[None, :]
    sa = jnp.where(even, -sin, 0.0)
    sb = jnp.where(even, 0.0, sin)
    ident = jnp.ones((CTX_LEN, HEAD_W), F32)
    zero = jnp.zeros((CTX_LEN, HEAD_W), F32)
    return (jnp.concatenate([ident, cos], 0), jnp.concatenate([zero, sa], 0),
            jnp.concatenate([zero, sb], 0))


ATT_TQ = 256


def _attn_kernel(lam_ref, g_ref, q_ref, k_ref, v_ref, o_ref):
    q = q_ref[...].astype(F32)
    lane = lax.broadcasted_iota(jnp.int32, (1, HEAD_W), 1)
    k = k_ref[...]
    v = v_ref[...]

    def soft(qm):
        s = lax.dot_general(qm.astype(BF16), k, NT_DIMS, preferred_element_type=F32)
        m = jnp.max(s, axis=-1, keepdims=True)
        e = jnp.exp(s - m)
        z = jnp.sum(e, axis=-1, keepdims=True)
        return jnp.dot(e.astype(BF16), v, preferred_element_type=F32) / z

    lp = lam_ref[...]
    lam = (jnp.exp(jnp.sum(lp[0:1] * lp[1:2], axis=-1, keepdims=True))
           - jnp.exp(jnp.sum(lp[2:3] * lp[3:4], axis=-1, keepdims=True)) + LAM_INIT_L1)
    o = soft(jnp.where(lane < DIFF_DH, q, 0.0)) - lam * soft(jnp.where(lane >= DIFF_DH, q, 0.0))
    o_ref[...] = (_rms(o, g_ref[...]) * (1.0 - LAM_INIT_L1)).astype(BF16)


def _attn(qkvf, lam_p, head_g):
    nq = SEQ // ATT_TQ
    q_off = CTX_LEN // ATT_TQ
    return pl.pallas_call(
        _attn_kernel,
        grid=(BATCH, DIFF_HEADS, nq),
        in_specs=[
            pl.BlockSpec((4, DIFF_DH), lambda b, h, i: (0, 0)),
            pl.BlockSpec((1, HEAD_W), lambda b, h, i: (0, 0)),
            pl.BlockSpec((None, ATT_TQ, HEAD_W), lambda b, h, i: (b, i + q_off, h)),
            pl.BlockSpec((None, SEQ_ALL, HEAD_W), lambda b, h, i: (b, 0, DIFF_HEADS + h)),
            pl.BlockSpec((None, SEQ_ALL, HEAD_W), lambda b, h, i: (b, 0, 2 * DIFF_HEADS + h)),
        ],
        out_specs=pl.BlockSpec((None, ATT_TQ, HEAD_W), lambda b, h, i: (b, i, h)),
        out_shape=jax.ShapeDtypeStruct((BATCH, SEQ, DIFF_QK), BF16),
        compiler_params=_params("arbitrary", "arbitrary", "arbitrary"),
        name="diff_attn",
    )(lam_p, head_g, qkvf, qkvf, qkvf)


def _channel_dft():
    n = np.arange(FOURIER_GC)
    ang = 2.0 * np.pi * np.outer(n, n) / FOURIER_GC
    eye = np.eye(FOURIER_GROUPS)
    scale = FOURIER_GC ** -0.5
    return np.concatenate([np.kron(eye, np.cos(ang)), np.kron(eye, np.sin(ang))], axis=1) * scale


def _seq_dft():
    k = jnp.arange(SEQ, dtype=jnp.int32)[:, None]
    n = jnp.arange(GRID_W, dtype=jnp.int32)[None, :]
    a = ((k * n) % GRID_W).astype(F32) * (2.0 * math.pi / GRID_W)
    b = ((k * n) % SEQ).astype(F32) * (2.0 * math.pi / SEQ)
    ca, sa, cb, sb = jnp.cos(a)[:, :, None], jnp.sin(a)[:, :, None], jnp.cos(b)[:, None, :], jnp.sin(b)[:, None, :]
    scale = SEQ ** -0.5
    cos = ((ca * cb - sa * sb) * scale).reshape(SEQ, SEQ)
    sin = ((sa * cb + ca * sb) * scale).reshape(SEQ, SEQ)
    return jnp.concatenate([cos, -sin], axis=1).astype(BF16)


def _fourier_chan_kernel(f_ref, bd_ref, o_ref):
    f = f_ref[CTX_LEN:, :]
    uv = jnp.dot(f, bd_ref[...], preferred_element_type=F32)
    o_ref[:SEQ, :] = uv[:, :FOURIER_W].astype(BF16)
    o_ref[SEQ:, :] = uv[:, FOURIER_W:].astype(BF16)


def _fourier_seq_kernel(t_ref, uv_ref, o_ref):
    o_ref[...] = jnp.dot(t_ref[...], uv_ref[...], preferred_element_type=F32).astype(BF16)


FOURIER_TM = 512


def _fourier(qkvf, seq_dft):
    bd = jnp.asarray(_channel_dft(), dtype=BF16)
    uv = pl.pallas_call(
        _fourier_chan_kernel,
        grid=(BATCH,),
        in_specs=[
            pl.BlockSpec((None, SEQ_ALL, FOURIER_W), lambda b: (b, 0, 3 * DIFF_QK // FOURIER_W)),
            pl.BlockSpec((FOURIER_W, 2 * FOURIER_W), lambda b: (0, 0)),
        ],
        out_specs=pl.BlockSpec((None, 2 * SEQ, FOURIER_W), lambda b: (b, 0, 0)),
        out_shape=jax.ShapeDtypeStruct((BATCH, 2 * SEQ, FOURIER_W), BF16),
        compiler_params=_params("arbitrary"),
        name="fourier_chan",
    )(qkvf, bd)
    return pl.pallas_call(
        _fourier_seq_kernel,
        grid=(SEQ // FOURIER_TM, BATCH),
        in_specs=[
            pl.BlockSpec((FOURIER_TM, 2 * SEQ), lambda i, b: (i, 0)),
            pl.BlockSpec((None, 2 * SEQ, FOURIER_W), lambda i, b: (b, 0, 0)),
        ],
        out_specs=pl.BlockSpec((None, FOURIER_TM, FOURIER_W), lambda i, b: (b, i, 0)),
        out_shape=jax.ShapeDtypeStruct((BATCH, SEQ, FOURIER_W), BF16),
        compiler_params=_params("arbitrary", "arbitrary"),
        name="fourier_seq",
    )(seq_dft, uv)


OUT_TM = 512


def _outproj_kernel(o_ref, f_ref, w_ref, x_ref, g_ref, out_ref):
    y = jnp.dot(o_ref[...], w_ref[:DIFF_QK, :], preferred_element_type=F32)
    y = y + jnp.dot(f_ref[...], w_ref[DIFF_QK:, :], preferred_element_type=F32)
    out_ref[...] = x_ref[...] + g_ref[...] * y


def _outproj(o, fm, w, x2d, g1):
    rows = x2d.shape[0]
    per_b = SEQ // OUT_TM
    return pl.pallas_call(
        _outproj_kernel,
        grid=(rows // OUT_TM,),
        in_specs=[
            pl.BlockSpec((OUT_TM, DIFF_QK), lambda i: (i, 0)),
            pl.BlockSpec((OUT_TM, FOURIER_W), lambda i: (i, 0)),
            pl.BlockSpec((D_MODEL, D_MODEL), lambda i: (0, 0)),
            pl.BlockSpec((OUT_TM, D_MODEL), lambda i: (i, 0)),
            pl.BlockSpec((None, 1, D_MODEL), lambda i: (i // per_b, 0, 0)),
        ],
        out_specs=pl.BlockSpec((OUT_TM, D_MODEL), lambda i: (i, 0)),
        out_shape=jax.ShapeDtypeStruct((rows, D_MODEL), F32),
        compiler_params=_params("arbitrary"),
        name="outproj",
    )(o, fm, w, x2d, g1)


SGU_TM = 256


def _sgu_kernel(x_ref, g_ref, sc_ref, sh_ref, win_ref, bin_ref, ng_ref, ws_ref, bs_ref, wout_ref, g1_ref,
                out_ref, us_ref):
    x = x_ref[...]
    h = _rms(x, g_ref[...]) * (1.0 + sc_ref[...]) + sh_ref[...]
    z = _gelu(jnp.dot(h.astype(BF16), win_ref[...], preferred_element_type=F32) + bin_ref[...])
    u = z[:, :SGU_W]
    v = _rms(z[:, SGU_W:], ng_ref[...]).astype(BF16)
    for ck in range(SGU_TM // CHUNK):
        rows = slice(ck * CHUNK, (ck + 1) * CHUNK)
        for g in range(SGU_GROUPS):
            cols = slice(g * CHUNK, (g + 1) * CHUNK)
            s = jnp.dot(ws_ref[g], v[rows, cols], preferred_element_type=F32) + bs_ref[g]
            us_ref[rows, cols] = (u[rows, cols] * s).astype(BF16)
    y = jnp.dot(us_ref[...], wout_ref[...], preferred_element_type=F32)
    out_ref[...] = x + g1_ref[...] * y


def _sgu(x2d, g, sc, sh, win, b_in, ng, ws, bs, wout, g1):
    rows = x2d.shape[0]
    per_b = SEQ // SGU_TM
    vec = lambda n: pl.BlockSpec((1, n), lambda i: (0, 0))
    per_batch = pl.BlockSpec((None, 1, D_MODEL), lambda i: (i // per_b, 0, 0))
    return pl.pallas_call(
        _sgu_kernel,
        grid=(rows // SGU_TM,),
        in_specs=[
            pl.BlockSpec((SGU_TM, D_MODEL), lambda i: (i, 0)),
            vec(D_MODEL), per_batch, per_batch,
            pl.BlockSpec((D_MODEL, 2 * SGU_W), lambda i: (0, 0)),
            vec(2 * SGU_W), vec(SGU_W),
            pl.BlockSpec((SGU_GROUPS, CHUNK, CHUNK), lambda i: (0, 0, 0)),
            pl.BlockSpec((SGU_GROUPS, CHUNK, CHUNK), lambda i: (0, 0, 0)),
            pl.BlockSpec((SGU_W, D_MODEL), lambda i: (0, 0)),
            per_batch,
        ],
        out_specs=pl.BlockSpec((SGU_TM, D_MODEL), lambda i: (i, 0)),
        out_shape=jax.ShapeDtypeStruct((rows, D_MODEL), F32),
        scratch_shapes=[pltpu.VMEM((SGU_TM, SGU_W), BF16)],
        compiler_params=_params("arbitrary"),
        name="sgu",
    )(x2d, g, sc, sh, win, b_in, ng, ws, bs, wout, g1)


ROUTE_TM = 256
PAIR_CELLS = [(a, b) for a in range(PEER_TOPK + 1) for b in range(PEER_TOPK + 1)
              if (a + 1) * (b + 1) <= PEER_TOPK + 1]


def _route_kernel(x_ref, g_ref, sc_ref, sh_ref, wq_ref, keys_ref,
                  t_ref, cnt_ref, a_ref, r2_ref, b_ref, s_scr):
    x = x_ref[...]
    tf = _rms(x, g_ref[...]) * (1.0 + sc_ref[...]) + sh_ref[...]
    t_ref[...] = tf.T.astype(BF16)
    q = jnp.dot(tf.astype(BF16), wq_ref[...], preferred_element_type=F32)
    neg = jnp.float32(-jnp.inf)
    tops = []
    for hs in range(2 * PEER_HEADS):
        qh = q[:, hs * PEER_HALF:(hs + 1) * PEER_HALF].astype(BF16)
        st = lax.dot_general(keys_ref[hs], qh, NT_DIMS, preferred_element_type=F32)
        s_scr[hs] = st
        second = hs % 2 == 1
        rank = jnp.full_like(st, float(PEER_TOPK))
        vals = []
        for r in range(PEER_TOPK + 1):
            m = jnp.max(st, axis=0, keepdims=True)
            vals.append(m)
            if r < PEER_TOPK:
                hit = st == m
                st = jnp.where(hit, neg, st)
                if second:
                    rank = jnp.where(hit, float(r), rank)
        if second:
            r2_ref[hs // 2] = rank.astype(BF16)
        tops.append(vals)
    v1 = [jnp.concatenate([tops[2 * h][r] for h in range(PEER_HEADS)], axis=0) for r in range(PEER_TOPK + 1)]
    v2 = [jnp.concatenate([tops[2 * h + 1][r] for h in range(PEER_HEADS)], axis=0) for r in range(PEER_TOPK + 1)]
    cand = [v1[a] + v2[b] for a, b in PAIR_CELLS]
    cur = cand
    kth = None
    for r in range(PEER_TOPK + 1):
        m = functools.reduce(jnp.maximum, cur)
        if r == PEER_TOPK - 1:
            kth = m
        if r < PEER_TOPK:
            cur = [jnp.where(c == m, neg, c) for c in cur]
    thr = 0.5 * (kth + m)
    top = v1[0] + v2[0]
    z = functools.reduce(lambda p, c: p + jnp.where(c >= thr, jnp.exp(c - top), 0.0), cand, jnp.zeros_like(top))
    rz = 1.0 / z
    for h in range(PEER_HEADS):
        s1 = s_scr[2 * h]
        theta = thr[h:h + 1] - s1
        cnt = functools.reduce(lambda p, vb: p + jnp.where(vb[h:h + 1] > theta, 1.0, 0.0), v2[:PEER_TOPK],
                               jnp.zeros_like(theta))
        cnt_ref[h] = cnt
        a_ref[h] = jnp.exp(s1 - v1[0][h:h + 1]) * rz[h:h + 1]
        b_ref[h] = jnp.exp(s_scr[2 * h + 1] - v2[0][h:h + 1]).astype(BF16)


def _route(x2d, g, sc, sh, wq, keys):
    rows = x2d.shape[0]
    per_b = SEQ // ROUTE_TM
    per_batch = pl.BlockSpec((None, 1, D_MODEL), lambda i: (i // per_b, 0, 0))
    tbl = pl.BlockSpec((PEER_HEADS, N_KEYS, ROUTE_TM), lambda i: (0, 0, i))
    row_tbl = jax.ShapeDtypeStruct((PEER_HEADS, N_KEYS, rows), F32)
    col_tbl = jax.ShapeDtypeStruct((PEER_HEADS, N_KEYS, rows), BF16)
    return pl.pallas_call(
        _route_kernel,
        grid=(rows // ROUTE_TM,),
        in_specs=[
            pl.BlockSpec((ROUTE_TM, D_MODEL), lambda i: (i, 0)),
            pl.BlockSpec((1, D_MODEL), lambda i: (0, 0)),
            per_batch, per_batch,
            pl.BlockSpec((D_MODEL, 2 * PEER_HEADS * PEER_HALF), lambda i: (0, 0)),
            pl.BlockSpec((2 * PEER_HEADS, N_KEYS, PEER_HALF), lambda i: (0, 0, 0)),
        ],
        out_specs=[pl.BlockSpec((D_MODEL, ROUTE_TM), lambda i: (0, i)), tbl, tbl, tbl, tbl],
        out_shape=[jax.ShapeDtypeStruct((D_MODEL, rows), BF16), row_tbl, row_tbl, col_tbl, col_tbl],
        scratch_shapes=[pltpu.VMEM((2 * PEER_HEADS, N_KEYS, ROUTE_TM), F32)],
        compiler_params=_params("arbitrary"),
        name="peer_route",
    )(x2d, g, sc, sh, wq, keys)


DENSE_TM = 512
DENSE_TE = 1024
DENSE_SUB = 256


def _dense_kernel(t_ref, cnt_ref, a_ref, r2_ref, b_ref, eu_ref, evt_ref, x_ref, g2_ref, fg_ref,
                  out_ref, acc_ref, act_ref, g_ref, *, final_norm):
    e = pl.program_id(1)

    @pl.when(e == 0)
    def _():
        acc_ref[...] = jnp.zeros_like(acc_ref)

    tm = t_ref.shape[1]
    zero = jnp.zeros((), BF16)

    def row(ref, h, i):
        tile = jnp.broadcast_to(ref[h, pl.ds(i, 1), :], (16, tm)).astype(BF16)
        return jnp.concatenate([tile] * (N_KEYS // 16), axis=0)

    rows_per_sub = DENSE_SUB // N_KEYS
    n_sub = DENSE_TE // DENSE_SUB

    def activations(sb):
        act_ref[sb % 2] = jnp.dot(eu_ref[sb * DENSE_SUB:(sb + 1) * DENSE_SUB, :], t_ref[...],
                                  preferred_element_type=F32)

    activations(0)
    for sb in range(n_sub):
        if sb + 1 < n_sub:
            activations(sb + 1)
        slot = sb % 2
        for ii in range(rows_per_sub):
            i = e * (DENSE_TE // N_KEYS) + sb * rows_per_sub + ii
            w = None
            for h in range(PEER_HEADS):
                term = jnp.where(r2_ref[h] < row(cnt_ref, h, i), b_ref[h] * row(a_ref, h, i), zero)
                w = term if w is None else w + term
            rows = slice(ii * N_KEYS, (ii + 1) * N_KEYS)
            g_ref[slot, rows, :] = _gelu(act_ref[slot, rows, :]).astype(BF16) * w
        acc_ref[...] += jnp.dot(evt_ref[:, sb * DENSE_SUB:(sb + 1) * DENSE_SUB], g_ref[slot],
                                preferred_element_type=F32)

    @pl.when(e == pl.num_programs(1) - 1)
    def _():
        y = x_ref[...] + g2_ref[...] * acc_ref[...].T
        if final_norm:
            y = _rms(y, fg_ref[...])
        out_ref[...] = y


def _dense(t, cnt, a, r2, b, eu, evt, x2d, g2, fg, final_norm):
    rows = x2d.shape[0]
    per_b = SEQ // DENSE_TM
    tbl = pl.BlockSpec((PEER_HEADS, N_KEYS, DENSE_TM), lambda i, e: (0, 0, i))
    return pl.pallas_call(
        functools.partial(_dense_kernel, final_norm=final_norm),
        grid=(rows // DENSE_TM, N_EXPERTS // DENSE_TE),
        in_specs=[
            pl.BlockSpec((D_MODEL, DENSE_TM), lambda i, e: (0, i)),
            tbl, tbl, tbl, tbl,
            pl.BlockSpec((DENSE_TE, D_MODEL), lambda i, e: (e, 0)),
            pl.BlockSpec((D_MODEL, DENSE_TE), lambda i, e: (0, e)),
            pl.BlockSpec((DENSE_TM, D_MODEL), lambda i, e: (i, 0)),
            pl.BlockSpec((None, 1, D_MODEL), lambda i, e: (i // per_b, 0, 0)),
            pl.BlockSpec((1, D_MODEL), lambda i, e: (0, 0)),
        ],
        out_specs=pl.BlockSpec((DENSE_TM, D_MODEL), lambda i, e: (i, 0)),
        out_shape=jax.ShapeDtypeStruct((rows, D_MODEL), F32),
        scratch_shapes=[pltpu.VMEM((D_MODEL, DENSE_TM), F32), pltpu.VMEM((2, DENSE_SUB, DENSE_TM), F32),
                        pltpu.VMEM((2, DENSE_SUB, DENSE_TM), BF16)],
        compiler_params=_params("arbitrary", "arbitrary"),
        name="peer_dense",
    )(t, cnt, a, r2, b, eu, evt, x2d, g2, fg)


def _peer(x2d, g, sc, sh, g2, wq, keys, eu, ev, fg, final_norm):
    wq_b = wq.astype(BF16)
    keys_b = keys.reshape(2 * PEER_HEADS, N_KEYS, PEER_HALF).astype(BF16)
    t, cnt, a, r2, b = _route(x2d, g, sc, sh, wq_b, keys_b)
    return _dense(t, cnt, a, r2, b, eu.astype(BF16), ev.T.astype(BF16), x2d, g2, fg, final_norm)


def kernel(x, c, ctx, c_ctx, ada_w, ada_b, norm1_g, norm2_g, final_g, even_w_in, even_w_out, diff_lambda,
           diff_norm_g, odd_w_in, odd_b_in, sgu_norm_g, sgu_w, sgu_b, odd_w_out, peer_wq, peer_keys, peer_u, peer_v):
    cvec = jnp.concatenate([c, c_ctx[None, :], jnp.zeros((8 - BATCH - 1, D_MODEL), F32)], axis=0)
    mod = _ada(cvec, ada_w, ada_b).reshape(2, 8, 6, D_MODEL)
    row = lambda v: v.reshape(1, -1)
    per_batch = lambda l, j: mod[l, :BATCH, j].reshape(BATCH, 1, D_MODEL)
    fg = row(final_g)

    both = lambda j: jnp.stack([mod[0, :BATCH, j], jnp.broadcast_to(mod[0, BATCH, j], (BATCH, D_MODEL))],
                               axis=1).reshape(BATCH, 2, 1, D_MODEL)
    xin = jnp.concatenate([ctx, x], axis=1).reshape(BATCH * SEQ_ALL, D_MODEL)
    cos, sa, sb = _rope_tables()
    qkvf = _inproj(xin, row(norm1_g[0]), both(1), both(0), even_w_in[0].astype(BF16), cos, sa, sb)
    qkvf = qkvf.reshape(BATCH, SEQ_ALL, EVEN_IN)
    o = _attn(qkvf, diff_lambda[0], row(diff_norm_g[0]))
    fm = _fourier(qkvf, _seq_dft())
    x2d = x.reshape(BATCH * SEQ, D_MODEL)
    x2d = _outproj(o.reshape(BATCH * SEQ, DIFF_QK), fm.reshape(BATCH * SEQ, FOURIER_W),
                   even_w_out[0].astype(BF16), x2d, per_batch(0, 2))
    x2d = _peer(x2d, row(norm2_g[0]), per_batch(0, 4), per_batch(0, 3), per_batch(0, 5),
                peer_wq[0], peer_keys[0], peer_u[0], peer_v[0], fg, False)

    bs = jnp.broadcast_to(sgu_b[0][:, :, None], (SGU_GROUPS, CHUNK, CHUNK))
    x2d = _sgu(x2d, row(norm1_g[1]), per_batch(1, 1), per_batch(1, 0), odd_w_in[0].astype(BF16),
               row(odd_b_in[0]), row(sgu_norm_g[0]), sgu_w[0].astype(BF16), bs, odd_w_out[0].astype(BF16),
               per_batch(1, 2))
    x2d = _peer(x2d, row(norm2_g[1]), per_batch(1, 4), per_batch(1, 3), per_batch(1, 5),
                peer_wq[1], peer_keys[1], peer_u[1], peer_v[1], fg, True)
    return x2d.reshape(BATCH, SEQ, D_MODEL)
```

```python
import functools
import math

import numpy as np
import jax
import jax.numpy as jnp
from jax import lax
from jax.experimental import pallas as pl
from jax.experimental.pallas import tpu as pltpu

F32 = jnp.float32
BF16 = jnp.bfloat16

D_MODEL = 1024
BATCH = 4
SEQ = 4096
GRID_W = 64
CTX_LEN = 256
SEQ_ALL = CTX_LEN + SEQ
EPS = 1e-6

DIFF_HEADS = 6
DIFF_DH = 64
HEAD_W = 2 * DIFF_DH
DIFF_QK = DIFF_HEADS * HEAD_W
FOURIER_GROUPS = 4
FOURIER_GC = 64
FOURIER_W = FOURIER_GROUPS * FOURIER_GC
EVEN_IN = 3 * DIFF_QK + FOURIER_W
ROPE_BASE = 10000.0
ROPE_PAIRS = DIFF_DH // 4
LAM_INIT_L1 = 0.8 - 0.6 * math.exp(-0.3 * 0.0)

CHUNK = 128
SGU_GROUPS = 8
SGU_W = SGU_GROUPS * CHUNK

PEER_HEADS = 8
N_KEYS = 128
N_EXPERTS = N_KEYS * N_KEYS
PEER_TOPK = 16
PEER_HALF = 128
LANES = 128

VMEM_LIMIT = 56 * 1024 * 1024

NT_DIMS = (((1,), (1,)), ((), ()))


def _rms(t, g):
    return t * lax.rsqrt(jnp.mean(t * t, axis=-1, keepdims=True) + EPS) * g


def _gelu(x):
    cdf = 0.5 * (1.0 + jnp.tanh(math.sqrt(2.0 / math.pi) * (x + 0.044715 * (x * x * x))))
    return x * cdf


def _gelu_packed(x):
    c = math.sqrt(2.0 / math.pi)
    half = 0.5 * x
    return half + half * jnp.tanh(x * (c + (c * 0.044715) * (x * x)))


def _params(*sem):
    return pltpu.CompilerParams(dimension_semantics=sem, vmem_limit_bytes=VMEM_LIMIT)


def _ada_kernel(c_ref, w_ref, b_ref, o_ref):
    c = c_ref[...]
    s = c * jax.nn.sigmoid(c)
    o_ref[...] = jnp.dot(s, w_ref[...], preferred_element_type=F32) + b_ref[...]


def _ada(cvec, ada_w, ada_b):
    depth = ada_w.shape[0]
    tn = 1536
    return pl.pallas_call(
        _ada_kernel,
        grid=(depth, 6 * D_MODEL // tn),
        in_specs=[
            pl.BlockSpec((8, D_MODEL), lambda l, j: (0, 0)),
            pl.BlockSpec((None, D_MODEL, tn), lambda l, j: (l, 0, j)),
            pl.BlockSpec((None, 1, tn), lambda l, j: (l, 0, j)),
        ],
        out_specs=pl.BlockSpec((None, 8, tn), lambda l, j: (l, 0, j)),
        out_shape=jax.ShapeDtypeStruct((depth, 8, 6 * D_MODEL), F32),
        compiler_params=_params("arbitrary", "arbitrary"),
        name="ada",
    )(cvec, ada_w, ada_b.reshape(depth, 1, 6 * D_MODEL))


IN_TM = 256
TILES_PER_SEQ = SEQ_ALL // IN_TM


def _inproj_kernel(x_ref, g_ref, sc_ref, sh_ref, w_ref, cos_ref, sa_ref, sb_ref, o_ref):
    x = x_ref[...]
    h = _rms(x, g_ref[...]) * (1.0 + sc_ref[...]) + sh_ref[...]
    r = jnp.dot(h.astype(BF16), w_ref[...], preferred_element_type=F32)
    cos, sa, sb = cos_ref[...], sa_ref[...], sb_ref[...]
    for cb in range(2 * DIFF_HEADS):
        t = r[:, cb * HEAD_W:(cb + 1) * HEAD_W]
        rot = t * cos + pltpu.roll(t, HEAD_W - 16, 1) * sa + pltpu.roll(t, 16, 1) * sb
        if cb < DIFF_HEADS:
            rot = rot * (DIFF_DH ** -0.5)
        o_ref[:, cb * HEAD_W:(cb + 1) * HEAD_W] = rot.astype(BF16)
    o_ref[:, 2 * DIFF_QK:] = r[:, 2 * DIFF_QK:].astype(BF16)


def _inproj(xin, g, sc, sh, w, cos, sa, sb):
    rows = xin.shape[0]
    mod_spec = pl.BlockSpec((None, None, 1, D_MODEL),
                            lambda i: (i // TILES_PER_SEQ, jnp.where(i % TILES_PER_SEQ == 0, 1, 0), 0, 0))
    rope_spec = pl.BlockSpec((IN_TM, HEAD_W), lambda i: (i % TILES_PER_SEQ, 0))
    return pl.pallas_call(
        _inproj_kernel,
        grid=(rows // IN_TM,),
        in_specs=[
            pl.BlockSpec((IN_TM, D_MODEL), lambda i: (i, 0)),
            pl.BlockSpec((1, D_MODEL), lambda i: (0, 0)),
            mod_spec, mod_spec,
            pl.BlockSpec((D_MODEL, EVEN_IN), lambda i: (0, 0)),
            rope_spec, rope_spec, rope_spec,
        ],
        out_specs=pl.BlockSpec((IN_TM, EVEN_IN), lambda i: (i, 0)),
        out_shape=jax.ShapeDtypeStruct((rows, EVEN_IN), BF16),
        compiler_params=_params("arbitrary"),
        name="inproj",
    )(xin, g, sc, sh, w, cos, sa, sb)


def _rope_tables():
    rows = SEQ // GRID_W
    r = jnp.repeat(jnp.arange(rows, dtype=F32), GRID_W)
    col = jnp.tile(jnp.arange(GRID_W, dtype=F32), rows)
    inv = ROPE_BASE ** (-jnp.arange(ROPE_PAIRS, dtype=F32) / ROPE_PAIRS)
    ar = r[:, None] * inv
    ac = col[:, None] * inv
    ang = jnp.concatenate([ar, ar, ac, ac] * 2, axis=-1)
    cos, sin = jnp.cos(ang), jnp.sin(ang)
    even = ((np.arange(HEAD_W) // ROPE_PAIRS) % 2 == 0)[None, :]
    sa = jnp.where(even, -sin, 0.0)
    sb = jnp.where(even, 0.0, sin)
    ident = jnp.ones((CTX_LEN, HEAD_W), F32)
    zero = jnp.zeros((CTX_LEN, HEAD_W), F32)
    return (jnp.concatenate([ident, cos], 0), jnp.concatenate([zero, sa], 0),
            jnp.concatenate([zero, sb], 0))


ATT_TQ = 256


def _attn_kernel(lam_ref, g_ref, q_ref, k_ref, v_ref, o_ref):
    q = q_ref[...].astype(F32)
    lane = lax.broadcasted_iota(jnp.int32, (1, HEAD_W), 1)
    k = k_ref[...]
    v = v_ref[...]

    def soft(qm):
        s = lax.dot_general(qm.astype(BF16), k, NT_DIMS, preferred_element_type=F32)
        m = jnp.max(s, axis=-1, keepdims=True)
        e = jnp.exp(s - m)
        z = jnp.sum(e, axis=-1, keepdims=True)
        return jnp.dot(e.astype(BF16), v, preferred_element_type=F32) / z

    lp = lam_ref[...]
    lam = (jnp.exp(jnp.sum(lp[0:1] * lp[1:2], axis=-1, keepdims=True))
           - jnp.exp(jnp.sum(lp[2:3] * lp[3:4], axis=-1, keepdims=True)) + LAM_INIT_L1)
    o = soft(jnp.where(lane < DIFF_DH, q, 0.0)) - lam * soft(jnp.where(lane >= DIFF_DH, q, 0.0))
    o_ref[...] = (_rms(o, g_ref[...]) * (1.0 - LAM_INIT_L1)).astype(BF16)


def _attn(qkvf, lam_p, head_g):
    nq = SEQ // ATT_TQ
    q_off = CTX_LEN // ATT_TQ
    return pl.pallas_call(
        _attn_kernel,
        grid=(BATCH, DIFF_HEADS, nq),
        in_specs=[
            pl.BlockSpec((4, DIFF_DH), lambda b, h, i: (0, 0)),
            pl.BlockSpec((1, HEAD_W), lambda b, h, i: (0, 0)),
            pl.BlockSpec((None, ATT_TQ, HEAD_W), lambda b, h, i: (b, i + q_off, h)),
            pl.BlockSpec((None, SEQ_ALL, HEAD_W), lambda b, h, i: (b, 0, DIFF_HEADS + h)),
            pl.BlockSpec((None, SEQ_ALL, HEAD_W), lambda b, h, i: (b, 0, 2 * DIFF_HEADS + h)),
        ],
        out_specs=pl.BlockSpec((None, ATT_TQ, HEAD_W), lambda b, h, i: (b, i, h)),
        out_shape=jax.ShapeDtypeStruct((BATCH, SEQ, DIFF_QK), BF16),
        compiler_params=_params("arbitrary", "arbitrary", "arbitrary"),
        name="diff_attn",
    )(lam_p, head_g, qkvf, qkvf, qkvf)


def _channel_dft():
    n = np.arange(FOURIER_GC)
    ang = 2.0 * np.pi * np.outer(n, n) / FOURIER_GC
    eye = np.eye(FOURIER_GROUPS)
    scale = FOURIER_GC ** -0.5
    return np.concatenate([np.kron(eye, np.cos(ang)), np.kron(eye, np.sin(ang))], axis=1) * scale


def _seq_dft():
    k = jnp.arange(SEQ, dtype=jnp.int32)[:, None]
    n = jnp.arange(GRID_W, dtype=jnp.int32)[None, :]
    a = ((k * n) % GRID_W).astype(F32) * (2.0 * math.pi / GRID_W)
    b = ((k * n) % SEQ).astype(F32) * (2.0 * math.pi / SEQ)
    ca, sa, cb, sb = jnp.cos(a)[:, :, None], jnp.sin(a)[:, :, None], jnp.cos(b)[:, None, :], jnp.sin(b)[:, None, :]
    scale = SEQ ** -0.5
    cos = ((ca * cb - sa * sb) * scale).reshape(SEQ, SEQ)
    sin = ((sa * cb + ca * sb) * scale).reshape(SEQ, SEQ)
    return jnp.concatenate([cos, -sin], axis=1).astype(BF16)


def _fourier_chan_kernel(f_ref, bd_ref, o_ref):
    f = f_ref[CTX_LEN:, :]
    uv = jnp.dot(f, bd_ref[...], preferred_element_type=F32)
    o_ref[:SEQ, :] = uv[:, :FOURIER_W].astype(BF16)
    o_ref[SEQ:, :] = uv[:, FOURIER_W:].astype(BF16)


def _fourier_seq_kernel(t_ref, uv_ref, o_ref):
    o_ref[...] = jnp.dot(t_ref[...], uv_ref[...], preferred_element_type=F32).astype(BF16)


FOURIER_TM = 512


def _fourier(qkvf, seq_dft):
    bd = jnp.asarray(_channel_dft(), dtype=BF16)
    uv = pl.pallas_call(
        _fourier_chan_kernel,
        grid=(BATCH,),
        in_specs=[
            pl.BlockSpec((None, SEQ_ALL, FOURIER_W), lambda b: (b, 0, 3 * DIFF_QK // FOURIER_W)),
            pl.BlockSpec((FOURIER_W, 2 * FOURIER_W), lambda b: (0, 0)),
        ],
        out_specs=pl.BlockSpec((None, 2 * SEQ, FOURIER_W), lambda b: (b, 0, 0)),
        out_shape=jax.ShapeDtypeStruct((BATCH, 2 * SEQ, FOURIER_W), BF16),
        compiler_params=_params("arbitrary"),
        name="fourier_chan",
    )(qkvf, bd)
    return pl.pallas_call(
        _fourier_seq_kernel,
        grid=(SEQ // FOURIER_TM, BATCH),
        in_specs=[
            pl.BlockSpec((FOURIER_TM, 2 * SEQ), lambda i, b: (i, 0)),
            pl.BlockSpec((None, 2 * SEQ, FOURIER_W), lambda i, b: (b, 0, 0)),
        ],
        out_specs=pl.BlockSpec((None, FOURIER_TM, FOURIER_W), lambda i, b: (b, i, 0)),
        out_shape=jax.ShapeDtypeStruct((BATCH, SEQ, FOURIER_W), BF16),
        compiler_params=_params("arbitrary", "arbitrary"),
        name="fourier_seq",
    )(seq_dft, uv)


OUT_TM = 512


def _outproj_kernel(o_ref, f_ref, w_ref, x_ref, g_ref, out_ref):
    y = jnp.dot(o_ref[...], w_ref[:DIFF_QK, :], preferred_element_type=F32)
    y = y + jnp.dot(f_ref[...], w_ref[DIFF_QK:, :], preferred_element_type=F32)
    out_ref[...] = x_ref[...] + g_ref[...] * y


def _outproj(o, fm, w, x2d, g1):
    rows = x2d.shape[0]
    per_b = SEQ // OUT_TM
    return pl.pallas_call(
        _outproj_kernel,
        grid=(rows // OUT_TM,),
        in_specs=[
            pl.BlockSpec((OUT_TM, DIFF_QK), lambda i: (i, 0)),
            pl.BlockSpec((OUT_TM, FOURIER_W), lambda i: (i, 0)),
            pl.BlockSpec((D_MODEL, D_MODEL), lambda i: (0, 0)),
            pl.BlockSpec((OUT_TM, D_MODEL), lambda i: (i, 0)),
            pl.BlockSpec((None, 1, D_MODEL), lambda i: (i // per_b, 0, 0)),
        ],
        out_specs=pl.BlockSpec((OUT_TM, D_MODEL), lambda i: (i, 0)),
        out_shape=jax.ShapeDtypeStruct((rows, D_MODEL), F32),
        compiler_params=_params("arbitrary"),
        name="outproj",
    )(o, fm, w, x2d, g1)


SGU_TM = 256


def _sgu_kernel(x_ref, g_ref, sc_ref, sh_ref, win_ref, bin_ref, ng_ref, ws_ref, bs_ref, wout_ref, g1_ref,
                out_ref, us_ref):
    x = x_ref[...]
    h = _rms(x, g_ref[...]) * (1.0 + sc_ref[...]) + sh_ref[...]
    z = _gelu(jnp.dot(h.astype(BF16), win_ref[...], preferred_element_type=F32) + bin_ref[...])
    u = z[:, :SGU_W]
    v = _rms(z[:, SGU_W:], ng_ref[...]).astype(BF16)
    for ck in range(SGU_TM // CHUNK):
        rows = slice(ck * CHUNK, (ck + 1) * CHUNK)
        for g in range(SGU_GROUPS):
            cols = slice(g * CHUNK, (g + 1) * CHUNK)
            s = jnp.dot(ws_ref[g], v[rows, cols], preferred_element_type=F32) + bs_ref[g]
            us_ref[rows, cols] = (u[rows, cols] * s).astype(BF16)
    y = jnp.dot(us_ref[...], wout_ref[...], preferred_element_type=F32)
    out_ref[...] = x + g1_ref[...] * y


def _sgu(x2d, g, sc, sh, win, b_in, ng, ws, bs, wout, g1):
    rows = x2d.shape[0]
    per_b = SEQ // SGU_TM
    vec = lambda n: pl.BlockSpec((1, n), lambda i: (0, 0))
    per_batch = pl.BlockSpec((None, 1, D_MODEL), lambda i: (i // per_b, 0, 0))
    return pl.pallas_call(
        _sgu_kernel,
        grid=(rows // SGU_TM,),
        in_specs=[
            pl.BlockSpec((SGU_TM, D_MODEL), lambda i: (i, 0)),
            vec(D_MODEL), per_batch, per_batch,
            pl.BlockSpec((D_MODEL, 2 * SGU_W), lambda i: (0, 0)),
            vec(2 * SGU_W), vec(SGU_W),
            pl.BlockSpec((SGU_GROUPS, CHUNK, CHUNK), lambda i: (0, 0, 0)),
            pl.BlockSpec((SGU_GROUPS, CHUNK, CHUNK), lambda i: (0, 0, 0)),
            pl.BlockSpec((SGU_W, D_MODEL), lambda i: (0, 0)),
            per_batch,
        ],
        out_specs=pl.BlockSpec((SGU_TM, D_MODEL), lambda i: (i, 0)),
        out_shape=jax.ShapeDtypeStruct((rows, D_MODEL), F32),
        scratch_shapes=[pltpu.VMEM((SGU_TM, SGU_W), BF16)],
        compiler_params=_params("arbitrary"),
        name="sgu",
    )(x2d, g, sc, sh, win, b_in, ng, ws, bs, wout, g1)


ROUTE_TM = 256
PAIR_CELLS = [(a, b) for a in range(PEER_TOPK + 1) for b in range(PEER_TOPK + 1)
              if (a + 1) * (b + 1) <= PEER_TOPK + 1]


def _route_kernel(x_ref, g_ref, sc_ref, sh_ref, wq_ref, keys_ref,
                  t_ref, cnt_ref, a_ref, r2_ref, b_ref, s_scr):
    x = x_ref[...]
    tf = _rms(x, g_ref[...]) * (1.0 + sc_ref[...]) + sh_ref[...]
    t_ref[...] = tf.T.astype(BF16)
    q = jnp.dot(tf.astype(BF16), wq_ref[...], preferred_element_type=F32)
    neg = jnp.float32(-jnp.inf)
    tops = []
    for hs in range(2 * PEER_HEADS):
        qh = q[:, hs * PEER_HALF:(hs + 1) * PEER_HALF].astype(BF16)
        st = lax.dot_general(keys_ref[hs], qh, NT_DIMS, preferred_element_type=F32)
        s_scr[hs] = st
        second = hs % 2 == 1
        rank = jnp.full_like(st, float(PEER_TOPK))
        vals = []
        for r in range(PEER_TOPK + 1):
            m = jnp.max(st, axis=0, keepdims=True)
            vals.append(m)
            if r < PEER_TOPK:
                hit = st == m
                st = jnp.where(hit, neg, st)
                if second:
                    rank = jnp.where(hit, float(r), rank)
        if second:
            r2_ref[hs // 2] = rank.astype(BF16)
        tops.append(vals)
    v1 = [jnp.concatenate([tops[2 * h][r] for h in range(PEER_HEADS)], axis=0) for r in range(PEER_TOPK + 1)]
    v2 = [jnp.concatenate([tops[2 * h + 1][r] for h in range(PEER_HEADS)], axis=0) for r in range(PEER_TOPK + 1)]
    cand = [v1[a] + v2[b] for a, b in PAIR_CELLS]
    cur = cand
    kth = None
    for r in range(PEER_TOPK + 1):
        m = functools.reduce(jnp.maximum, cur)
        if r == PEER_TOPK - 1:
            kth = m
        if r < PEER_TOPK:
            cur = [jnp.where(c == m, neg, c) for c in cur]
    thr = 0.5 * (kth + m)
    top = v1[0] + v2[0]
    z = functools.reduce(lambda p, c: p + jnp.where(c >= thr, jnp.exp(c - top), 0.0), cand, jnp.zeros_like(top))
    rz = 1.0 / z
    for h in range(PEER_HEADS):
        s1 = s_scr[2 * h]
        theta = thr[h:h + 1] - s1
        cnt = functools.reduce(lambda p, vb: p + jnp.where(vb[h:h + 1] > theta, 1.0, 0.0), v2[:PEER_TOPK],
                               jnp.zeros_like(theta))
        cnt_ref[h] = cnt
        a_ref[h] = jnp.exp(s1 - v1[0][h:h + 1]) * rz[h:h + 1]
        b_ref[h] = jnp.exp(s_scr[2 * h + 1] - v2[0][h:h + 1]).astype(BF16)


def _route(x2d, g, sc, sh, wq, keys):
    rows = x2d.shape[0]
    per_b = SEQ // ROUTE_TM
    per_batch = pl.BlockSpec((None, 1, D_MODEL), lambda i: (i // per_b, 0, 0))
    tbl = pl.BlockSpec((PEER_HEADS, N_KEYS, ROUTE_TM), lambda i: (0, 0, i))
    row_tbl = jax.ShapeDtypeStruct((PEER_HEADS, N_KEYS, rows), F32)
    col_tbl = jax.ShapeDtypeStruct((PEER_HEADS, N_KEYS, rows), BF16)
    return pl.pallas_call(
        _route_kernel,
        grid=(rows // ROUTE_TM,),
        in_specs=[
            pl.BlockSpec((ROUTE_TM, D_MODEL), lambda i: (i, 0)),
            pl.BlockSpec((1, D_MODEL), lambda i: (0, 0)),
            per_batch, per_batch,
            pl.BlockSpec((D_MODEL, 2 * PEER_HEADS * PEER_HALF), lambda i: (0, 0)),
            pl.BlockSpec((2 * PEER_HEADS, N_KEYS, PEER_HALF), lambda i: (0, 0, 0)),
        ],
        out_specs=[pl.BlockSpec((D_MODEL, ROUTE_TM), lambda i: (0, i)), tbl, tbl, tbl, tbl],
        out_shape=[jax.ShapeDtypeStruct((D_MODEL, rows), BF16), row_tbl, row_tbl, col_tbl, col_tbl],
        scratch_shapes=[pltpu.VMEM((2 * PEER_HEADS, N_KEYS, ROUTE_TM), F32)],
        compiler_params=_params("arbitrary"),
        name="peer_route",
    )(x2d, g, sc, sh, wq, keys)


DENSE_TM = 512
DENSE_TE = 2048
DENSE_SUB = 256
DENSE_JC = 32


def _dense_kernel(t_ref, cnt_ref, a_ref, r2_ref, b_ref, eu_ref, evt_ref, x_ref, g2_ref, fg_ref,
                  out_ref, acc_ref, act_ref, g_ref, rt_ref, *, final_norm):
    e = pl.program_id(1)

    @pl.when(e == 0)
    def _():
        acc_ref[...] = jnp.zeros_like(acc_ref)

    tm = t_ref.shape[1]
    zero = jnp.zeros((), BF16)

    def row_tile(ref, h, i):
        return jnp.broadcast_to(ref[h, pl.ds(i, 1), :], (16, tm)).astype(BF16)

    def tiled(ref, k, h):
        return jnp.concatenate([ref[k, h]] * (DENSE_JC // 16), axis=0)

    rows_per_sub = DENSE_SUB // N_KEYS
    n_sub = DENSE_TE // DENSE_SUB

    def activations(sb):
        act_ref[sb % 2] = jnp.dot(eu_ref[sb * DENSE_SUB:(sb + 1) * DENSE_SUB, :], t_ref[...],
                                  preferred_element_type=F32)

    activations(0)
    for sb in range(n_sub):
        if sb + 1 < n_sub:
            activations(sb + 1)
        slot = sb % 2
        for ii in range(rows_per_sub):
            i = e * (DENSE_TE // N_KEYS) + sb * rows_per_sub + ii
            for h in range(PEER_HEADS):
                rt_ref[0, h] = row_tile(cnt_ref, h, i)
                rt_ref[1, h] = row_tile(a_ref, h, i)
            for jc in range(N_KEYS // DENSE_JC):
                js = slice(jc * DENSE_JC, (jc + 1) * DENSE_JC)
                w = None
                for h in range(PEER_HEADS):
                    term = jnp.where(r2_ref[h, js, :] < tiled(rt_ref, 0, h),
                                     b_ref[h, js, :] * tiled(rt_ref, 1, h), zero)
                    w = term if w is None else w + term
                rows = slice(ii * N_KEYS + jc * DENSE_JC, ii * N_KEYS + (jc + 1) * DENSE_JC)
                g_ref[slot, rows, :] = _gelu_packed(act_ref[slot, rows, :].astype(BF16)) * w
        acc_ref[...] += jnp.dot(evt_ref[:, sb * DENSE_SUB:(sb + 1) * DENSE_SUB], g_ref[slot],
                                preferred_element_type=F32)

    @pl.when(e == pl.num_programs(1) - 1)
    def _():
        y = x_ref[...] + g2_ref[...] * acc_ref[...].T
        if final_norm:
            y = _rms(y, fg_ref[...])
        out_ref[...] = y


def _dense(t, cnt, a, r2, b, eu, evt, x2d, g2, fg, final_norm):
    rows = x2d.shape[0]
    per_b = SEQ // DENSE_TM
    tbl = pl.BlockSpec((PEER_HEADS, N_KEYS, DENSE_TM), lambda i, e: (0, 0, i))
    return pl.pallas_call(
        functools.partial(_dense_kernel, final_norm=final_norm),
        grid=(rows // DENSE_TM, N_EXPERTS // DENSE_TE),
        in_specs=[
            pl.BlockSpec((D_MODEL, DENSE_TM), lambda i, e: (0, i)),
            tbl, tbl, tbl, tbl,
            pl.BlockSpec((DENSE_TE, D_MODEL), lambda i, e: (e, 0)),
            pl.BlockSpec((D_MODEL, DENSE_TE), lambda i, e: (0, e)),
            pl.BlockSpec((DENSE_TM, D_MODEL), lambda i, e: (i, 0)),
            pl.BlockSpec((None, 1, D_MODEL), lambda i, e: (i // per_b, 0, 0)),
            pl.BlockSpec((1, D_MODEL), lambda i, e: (0, 0)),
        ],
        out_specs=pl.BlockSpec((DENSE_TM, D_MODEL), lambda i, e: (i, 0)),
        out_shape=jax.ShapeDtypeStruct((rows, D_MODEL), F32),
        scratch_shapes=[pltpu.VMEM((D_MODEL, DENSE_TM), F32), pltpu.VMEM((2, DENSE_SUB, DENSE_TM), F32),
                        pltpu.VMEM((2, DENSE_SUB, DENSE_TM), BF16),
                        pltpu.VMEM((2, PEER_HEADS, 16, DENSE_TM), BF16)],
        compiler_params=_params("arbitrary", "arbitrary"),
        name="peer_dense",
    )(t, cnt, a, r2, b, eu, evt, x2d, g2, fg)


def _peer(x2d, g, sc, sh, g2, wq, keys, eu, ev, fg, final_norm):
    wq_b = wq.astype(BF16)
    keys_b = keys.reshape(2 * PEER_HEADS, N_KEYS, PEER_HALF).astype(BF16)
    t, cnt, a, r2, b = _route(x2d, g, sc, sh, wq_b, keys_b)
    return _dense(t, cnt, a, r2, b, eu.astype(BF16), ev.T.astype(BF16), x2d, g2, fg, final_norm)


def kernel(x, c, ctx, c_ctx, ada_w, ada_b, norm1_g, norm2_g, final_g, even_w_in, even_w_out, diff_lambda,
           diff_norm_g, odd_w_in, odd_b_in, sgu_norm_g, sgu_w, sgu_b, odd_w_out, peer_wq, peer_keys, peer_u, peer_v):
    cvec = jnp.concatenate([c, c_ctx[None, :], jnp.zeros((8 - BATCH - 1, D_MODEL), F32)], axis=0)
    mod = _ada(cvec, ada_w, ada_b).reshape(2, 8, 6, D_MODEL)
    row = lambda v: v.reshape(1, -1)
    per_batch = lambda l, j: mod[l, :BATCH, j].reshape(BATCH, 1, D_MODEL)
    fg = row(final_g)

    both = lambda j: jnp.stack([mod[0, :BATCH, j], jnp.broadcast_to(mod[0, BATCH, j], (BATCH, D_MODEL))],
                               axis=1).reshape(BATCH, 2, 1, D_MODEL)
    xin = jnp.concatenate([ctx, x], axis=1).reshape(BATCH * SEQ_ALL, D_MODEL)
    cos, sa, sb = _rope_tables()
    qkvf = _inproj(xin, row(norm1_g[0]), both(1), both(0), even_w_in[0].astype(BF16), cos, sa, sb)
    qkvf = qkvf.reshape(BATCH, SEQ_ALL, EVEN_IN)
    o = _attn(qkvf, diff_lambda[0], row(diff_norm_g[0]))
    fm = _fourier(qkvf, _seq_dft())
    x2d = x.reshape(BATCH * SEQ, D_MODEL)
    x2d = _outproj(o.reshape(BATCH * SEQ, DIFF_QK), fm.reshape(BATCH * SEQ, FOURIER_W),
                   even_w_out[0].astype(BF16), x2d, per_batch(0, 2))
    x2d = _peer(x2d, row(norm2_g[0]), per_batch(0, 4), per_batch(0, 3), per_batch(0, 5),
                peer_wq[0], peer_keys[0], peer_u[0], peer_v[0], fg, False)

    bs = jnp.broadcast_to(sgu_b[0][:, :, None], (SGU_GROUPS, CHUNK, CHUNK))
    x2d = _sgu(x2d, row(norm1_g[1]), per_batch(1, 1), per_batch(1, 0), odd_w_in[0].astype(BF16),
               row(odd_b_in[0]), row(sgu_norm_g[0]), sgu_w[0].astype(BF16), bs, odd_w_out[0].astype(BF16),
               per_batch(1, 2))
    x2d = _peer(x2d, row(norm2_g[1]), per_batch(1, 4), per_batch(1, 3), per_batch(1, 5),
                peer_wq[1], peer_keys[1], peer_u[1], peer_v[1], fg, True)
    return x2d.reshape(BATCH, SEQ, D_MODEL)
```

```python
import functools
import math

import numpy as np
import jax
import jax.numpy as jnp
from jax import lax
from jax.experimental import pallas as pl
from jax.experimental.pallas import tpu as pltpu

F32 = jnp.float32
BF16 = jnp.bfloat16

D_MODEL = 1024
BATCH = 4
SEQ = 4096
GRID_W = 64
CTX_LEN = 256
SEQ_ALL = CTX_LEN + SEQ
EPS = 1e-6

DIFF_HEADS = 6
DIFF_DH = 64
HEAD_W = 2 * DIFF_DH
DIFF_QK = DIFF_HEADS * HEAD_W
FOURIER_GROUPS = 4
FOURIER_GC = 64
FOURIER_W = FOURIER_GROUPS * FOURIER_GC
EVEN_IN = 3 * DIFF_QK + FOURIER_W
ROPE_BASE = 10000.0
ROPE_PAIRS = DIFF_DH // 4
LAM_INIT_L1 = 0.8 - 0.6 * math.exp(-0.3 * 0.0)
LOG2_E = math.log2(math.e)

CHUNK = 128
SGU_GROUPS = 8
SGU_W = SGU_GROUPS * CHUNK

PEER_HEADS = 8
N_KEYS = 128
N_EXPERTS = N_KEYS * N_KEYS
PEER_TOPK = 16
PEER_HALF = 128
LANES = 128

VMEM_LIMIT = 56 * 1024 * 1024

NT_DIMS = (((1,), (1,)), ((), ()))


def _rms(t, g):
    return t * lax.rsqrt(jnp.mean(t * t, axis=-1, keepdims=True) + EPS) * g


def _gelu(x):
    cdf = 0.5 * (1.0 + jnp.tanh(math.sqrt(2.0 / math.pi) * (x + 0.044715 * (x * x * x))))
    return x * cdf


def _gelu_packed(x):
    c = math.sqrt(2.0 / math.pi)
    half = 0.5 * x
    return half + half * jnp.tanh(x * (c + (c * 0.044715) * (x * x)))


def _params(*sem, flags=None):
    return pltpu.CompilerParams(dimension_semantics=sem, vmem_limit_bytes=VMEM_LIMIT, flags=flags)


def _ada_kernel(c_ref, w_ref, b_ref, o_ref):
    c = c_ref[...]
    s = c * jax.nn.sigmoid(c)
    o_ref[...] = jnp.dot(s, w_ref[...], preferred_element_type=F32) + b_ref[...]


def _ada(cvec, ada_w, ada_b):
    depth = ada_w.shape[0]
    tn = 1536
    return pl.pallas_call(
        _ada_kernel,
        grid=(depth, 6 * D_MODEL // tn),
        in_specs=[
            pl.BlockSpec((8, D_MODEL), lambda l, j: (0, 0)),
            pl.BlockSpec((None, D_MODEL, tn), lambda l, j: (l, 0, j)),
            pl.BlockSpec((None, 1, tn), lambda l, j: (l, 0, j)),
        ],
        out_specs=pl.BlockSpec((None, 8, tn), lambda l, j: (l, 0, j)),
        out_shape=jax.ShapeDtypeStruct((depth, 8, 6 * D_MODEL), F32),
        compiler_params=_params("arbitrary", "arbitrary"),
        name="ada",
    )(cvec, ada_w, ada_b.reshape(depth, 1, 6 * D_MODEL))


IN_TM = 256
TILES_PER_SEQ = SEQ_ALL // IN_TM


def _inproj_kernel(x_ref, g_ref, sc_ref, sh_ref, w_ref, cos_ref, sa_ref, sb_ref, o_ref):
    x = x_ref[...]
    h = _rms(x, g_ref[...]) * (1.0 + sc_ref[...]) + sh_ref[...]
    r = jnp.dot(h.astype(BF16), w_ref[...], preferred_element_type=F32)
    cos, sa, sb = cos_ref[...], sa_ref[...], sb_ref[...]
    for cb in range(2 * DIFF_HEADS):
        t = r[:, cb * HEAD_W:(cb + 1) * HEAD_W]
        rot = t * cos + pltpu.roll(t, HEAD_W - 16, 1) * sa + pltpu.roll(t, 16, 1) * sb
        if cb < DIFF_HEADS:
            rot = rot * (DIFF_DH ** -0.5 * LOG2_E)
        o_ref[:, cb * HEAD_W:(cb + 1) * HEAD_W] = rot.astype(BF16)
    o_ref[:, 2 * DIFF_QK:] = r[:, 2 * DIFF_QK:].astype(BF16)


def _inproj(xin, g, sc, sh, w, cos, sa, sb):
    rows = xin.shape[0]
    mod_spec = pl.BlockSpec((None, None, 1, D_MODEL),
                            lambda i: (i // TILES_PER_SEQ, jnp.where(i % TILES_PER_SEQ == TILES_PER_SEQ - 1, 1, 0), 0, 0))
    rope_spec = pl.BlockSpec((IN_TM, HEAD_W), lambda i: (i % TILES_PER_SEQ, 0))
    return pl.pallas_call(
        _inproj_kernel,
        grid=(rows // IN_TM,),
        in_specs=[
            pl.BlockSpec((IN_TM, D_MODEL), lambda i: (i, 0)),
            pl.BlockSpec((1, D_MODEL), lambda i: (0, 0)),
            mod_spec, mod_spec,
            pl.BlockSpec((D_MODEL, EVEN_IN), lambda i: (0, 0)),
            rope_spec, rope_spec, rope_spec,
        ],
        out_specs=pl.BlockSpec((IN_TM, EVEN_IN), lambda i: (i, 0)),
        out_shape=jax.ShapeDtypeStruct((rows, EVEN_IN), BF16),
        compiler_params=_params("arbitrary"),
        name="inproj",
    )(xin, g, sc, sh, w, cos, sa, sb)


def _rope_tables():
    rows = SEQ // GRID_W
    r = jnp.repeat(jnp.arange(rows, dtype=F32), GRID_W)
    col = jnp.tile(jnp.arange(GRID_W, dtype=F32), rows)
    inv = ROPE_BASE ** (-jnp.arange(ROPE_PAIRS, dtype=F32) / ROPE_PAIRS)
    ar = r[:, None] * inv
    ac = col[:, None] * inv
    ang = jnp.concatenate([ar, ar, ac, ac] * 2, axis=-1)
    cos, sin = jnp.cos(ang), jnp.sin(ang)
    even = ((np.arange(HEAD_W) // ROPE_PAIRS) % 2 == 0)[None, :]
    sa = jnp.where(even, -sin, 0.0)
    sb = jnp.where(even, 0.0, sin)
    ident = jnp.ones((CTX_LEN, HEAD_W), F32)
    zero = jnp.zeros((CTX_LEN, HEAD_W), F32)
    return (jnp.concatenate([cos, ident], 0), jnp.concatenate([sa, zero], 0),
            jnp.concatenate([sb, zero], 0))


ATT_TQ = 512
ATT_CHAINS = 2


def _attn_kernel(lam_ref, g_ref, q_ref, k_ref, v_ref, o_ref, vt_ref):
    @pl.when(pl.program_id(2) == 0)
    def _():
        vt_ref[...] = v_ref[...].astype(F32).T.astype(BF16)

    lane = lax.broadcasted_iota(jnp.int32, (1, HEAD_W), 1)
    lp = lam_ref[...]
    lam = (jnp.exp(jnp.sum(lp[0:1] * lp[1:2], axis=-1, keepdims=True))
           - jnp.exp(jnp.sum(lp[2:3] * lp[3:4], axis=-1, keepdims=True)) + LAM_INIT_L1)
    tq = ATT_TQ // ATT_CHAINS
    for c in range(ATT_CHAINS):
        rows = slice(c * tq, (c + 1) * tq)
        q = q_ref[rows, :].astype(F32)
        qq = jnp.concatenate([jnp.where(lane < DIFF_DH, q, 0.0), jnp.where(lane >= DIFF_DH, q, 0.0)], axis=0)
        st = lax.dot_general(k_ref[...], qq.astype(BF16), NT_DIMS, preferred_element_type=F32)
        m = jnp.max(st, axis=0, keepdims=True)
        e = jnp.exp2(st - m)
        z = jnp.sum(e, axis=0, keepdims=True)
        ot = jnp.dot(vt_ref[...], e.astype(BF16), preferred_element_type=F32) / z
        o = (ot[:, :tq] - lam * ot[:, tq:]).T
        o_ref[rows, :] = (_rms(o, g_ref[...]) * (1.0 - LAM_INIT_L1)).astype(BF16)


def _attn(qkvf, lam_p, head_g):
    nq = SEQ // ATT_TQ
    return pl.pallas_call(
        _attn_kernel,
        grid=(BATCH, DIFF_HEADS, nq),
        in_specs=[
            pl.BlockSpec((4, DIFF_DH), lambda b, h, i: (0, 0)),
            pl.BlockSpec((1, HEAD_W), lambda b, h, i: (0, 0)),
            pl.BlockSpec((None, ATT_TQ, HEAD_W), lambda b, h, i: (b, i, h)),
            pl.BlockSpec((None, SEQ_ALL, HEAD_W), lambda b, h, i: (b, 0, DIFF_HEADS + h)),
            pl.BlockSpec((None, SEQ_ALL, HEAD_W), lambda b, h, i: (b, 0, 2 * DIFF_HEADS + h)),
        ],
        out_specs=pl.BlockSpec((None, ATT_TQ, HEAD_W), lambda b, h, i: (b, i, h)),
        out_shape=jax.ShapeDtypeStruct((BATCH, SEQ, DIFF_QK), BF16),
        scratch_shapes=[pltpu.VMEM((HEAD_W, SEQ_ALL), BF16)],
        compiler_params=_params("arbitrary", "arbitrary", "arbitrary"),
        name="diff_attn",
    )(lam_p, head_g, qkvf, qkvf, qkvf)


def _channel_dft():
    n = np.arange(FOURIER_GC)
    ang = 2.0 * np.pi * np.outer(n, n) / FOURIER_GC
    eye = np.eye(FOURIER_GROUPS)
    scale = FOURIER_GC ** -0.5
    return np.concatenate([np.kron(eye, np.cos(ang)), np.kron(eye, np.sin(ang))], axis=1) * scale


def _seq_dft():
    k = jnp.arange(SEQ, dtype=jnp.int32)[:, None]
    n = jnp.arange(GRID_W, dtype=jnp.int32)[None, :]
    a = ((k * n) % GRID_W).astype(F32) * (2.0 * math.pi / GRID_W)
    b = ((k * n) % SEQ).astype(F32) * (2.0 * math.pi / SEQ)
    ca, sa, cb, sb = jnp.cos(a)[:, :, None], jnp.sin(a)[:, :, None], jnp.cos(b)[:, None, :], jnp.sin(b)[:, None, :]
    scale = SEQ ** -0.5
    cos = ((ca * cb - sa * sb) * scale).reshape(SEQ, SEQ)
    sin = ((sa * cb + ca * sb) * scale).reshape(SEQ, SEQ)
    return jnp.concatenate([cos, -sin], axis=1).astype(BF16)


def _fourier_chan_kernel(f_ref, bd_ref, o_ref):
    f = f_ref[:SEQ, :]
    uv = jnp.dot(f, bd_ref[...], preferred_element_type=F32)
    o_ref[:SEQ, :] = uv[:, :FOURIER_W].astype(BF16)
    o_ref[SEQ:, :] = uv[:, FOURIER_W:].astype(BF16)


def _fourier_seq_kernel(t_ref, uv_ref, o_ref):
    o_ref[...] = jnp.dot(t_ref[...], uv_ref[...], preferred_element_type=F32).astype(BF16)


FOURIER_TM = 512


def _fourier(qkvf, seq_dft):
    bd = jnp.asarray(_channel_dft(), dtype=BF16)
    uv = pl.pallas_call(
        _fourier_chan_kernel,
        grid=(BATCH,),
        in_specs=[
            pl.BlockSpec((None, SEQ_ALL, FOURIER_W), lambda b: (b, 0, 3 * DIFF_QK // FOURIER_W)),
            pl.BlockSpec((FOURIER_W, 2 * FOURIER_W), lambda b: (0, 0)),
        ],
        out_specs=pl.BlockSpec((None, 2 * SEQ, FOURIER_W), lambda b: (b, 0, 0)),
        out_shape=jax.ShapeDtypeStruct((BATCH, 2 * SEQ, FOURIER_W), BF16),
        compiler_params=_params("arbitrary"),
        name="fourier_chan",
    )(qkvf, bd)
    return pl.pallas_call(
        _fourier_seq_kernel,
        grid=(SEQ // FOURIER_TM, BATCH),
        in_specs=[
            pl.BlockSpec((FOURIER_TM, 2 * SEQ), lambda i, b: (i, 0)),
            pl.BlockSpec((None, 2 * SEQ, FOURIER_W), lambda i, b: (b, 0, 0)),
        ],
        out_specs=pl.BlockSpec((None, FOURIER_TM, FOURIER_W), lambda i, b: (b, i, 0)),
        out_shape=jax.ShapeDtypeStruct((BATCH, SEQ, FOURIER_W), BF16),
        compiler_params=_params("arbitrary", "arbitrary"),
        name="fourier_seq",
    )(seq_dft, uv)


OUT_TM = 512


def _outproj_kernel(o_ref, f_ref, w_ref, x_ref, g_ref, out_ref):
    y = jnp.dot(o_ref[...], w_ref[:DIFF_QK, :], preferred_element_type=F32)
    y = y + jnp.dot(f_ref[...], w_ref[DIFF_QK:, :], preferred_element_type=F32)
    out_ref[...] = x_ref[...] + g_ref[...] * y


def _outproj(o, fm, w, x2d, g1):
    rows = x2d.shape[0]
    per_b = SEQ // OUT_TM
    return pl.pallas_call(
        _outproj_kernel,
        grid=(rows // OUT_TM,),
        in_specs=[
            pl.BlockSpec((OUT_TM, DIFF_QK), lambda i: (i, 0)),
            pl.BlockSpec((OUT_TM, FOURIER_W), lambda i: (i, 0)),
            pl.BlockSpec((D_MODEL, D_MODEL), lambda i: (0, 0)),
            pl.BlockSpec((OUT_TM, D_MODEL), lambda i: (i, 0)),
            pl.BlockSpec((None, 1, D_MODEL), lambda i: (i // per_b, 0, 0)),
        ],
        out_specs=pl.BlockSpec((OUT_TM, D_MODEL), lambda i: (i, 0)),
        out_shape=jax.ShapeDtypeStruct((rows, D_MODEL), F32),
        compiler_params=_params("arbitrary"),
        name="outproj",
    )(o, fm, w, x2d, g1)


SGU_TM = 256


def _sgu_kernel(x_ref, g_ref, sc_ref, sh_ref, win_ref, bin_ref, ng_ref, ws_ref, bs_ref, wout_ref, g1_ref,
                out_ref, us_ref):
    x = x_ref[...]
    h = _rms(x, g_ref[...]) * (1.0 + sc_ref[...]) + sh_ref[...]
    z = _gelu(jnp.dot(h.astype(BF16), win_ref[...], preferred_element_type=F32) + bin_ref[...])
    u = z[:, :SGU_W]
    v = _rms(z[:, SGU_W:], ng_ref[...]).astype(BF16)
    for ck in range(SGU_TM // CHUNK):
        rows = slice(ck * CHUNK, (ck + 1) * CHUNK)
        for g in range(SGU_GROUPS):
            cols = slice(g * CHUNK, (g + 1) * CHUNK)
            s = jnp.dot(ws_ref[g], v[rows, cols], preferred_element_type=F32) + bs_ref[g]
            us_ref[rows, cols] = (u[rows, cols] * s).astype(BF16)
    y = jnp.dot(us_ref[...], wout_ref[...], preferred_element_type=F32)
    out_ref[...] = x + g1_ref[...] * y


def _sgu(x2d, g, sc, sh, win, b_in, ng, ws, bs, wout, g1):
    rows = x2d.shape[0]
    per_b = SEQ // SGU_TM
    vec = lambda n: pl.BlockSpec((1, n), lambda i: (0, 0))
    per_batch = pl.BlockSpec((None, 1, D_MODEL), lambda i: (i // per_b, 0, 0))
    return pl.pallas_call(
        _sgu_kernel,
        grid=(rows // SGU_TM,),
        in_specs=[
            pl.BlockSpec((SGU_TM, D_MODEL), lambda i: (i, 0)),
            vec(D_MODEL), per_batch, per_batch,
            pl.BlockSpec((D_MODEL, 2 * SGU_W), lambda i: (0, 0)),
            vec(2 * SGU_W), vec(SGU_W),
            pl.BlockSpec((SGU_GROUPS, CHUNK, CHUNK), lambda i: (0, 0, 0)),
            pl.BlockSpec((SGU_GROUPS, CHUNK, CHUNK), lambda i: (0, 0, 0)),
            pl.BlockSpec((SGU_W, D_MODEL), lambda i: (0, 0)),
            per_batch,
        ],
        out_specs=pl.BlockSpec((SGU_TM, D_MODEL), lambda i: (i, 0)),
        out_shape=jax.ShapeDtypeStruct((rows, D_MODEL), F32),
        scratch_shapes=[pltpu.VMEM((SGU_TM, SGU_W), BF16)],
        compiler_params=_params("arbitrary"),
        name="sgu",
    )(x2d, g, sc, sh, win, b_in, ng, ws, bs, wout, g1)


ROUTE_TM = 256
PAIR_CELLS = [(a, b) for a in range(PEER_TOPK + 1) for b in range(PEER_TOPK + 1)
              if (a + 1) * (b + 1) <= PEER_TOPK + 1]


def _route_kernel(x_ref, g_ref, sc_ref, sh_ref, wq_ref, keys_ref,
                  t_ref, cnt_ref, a_ref, r2_ref, b_ref, s_scr):
    x = x_ref[...]
    tf = _rms(x, g_ref[...]) * (1.0 + sc_ref[...]) + sh_ref[...]
    t_ref[...] = tf.T.astype(BF16)
    q = jnp.dot(tf.astype(BF16), wq_ref[...], preferred_element_type=F32)
    neg = jnp.float32(-jnp.inf)
    tops = []
    for hs in range(2 * PEER_HEADS):
        qh = q[:, hs * PEER_HALF:(hs + 1) * PEER_HALF].astype(BF16)
        st = lax.dot_general(keys_ref[hs], qh, NT_DIMS, preferred_element_type=F32)
        s_scr[hs] = st
        second = hs % 2 == 1
        rank = jnp.full_like(st, float(PEER_TOPK))
        vals = []
        for r in range(PEER_TOPK + 1):
            m = jnp.max(st, axis=0, keepdims=True)
            vals.append(m)
            if r < PEER_TOPK:
                hit = st == m
                st = jnp.where(hit, neg, st)
                if second:
                    rank = jnp.where(hit, float(r), rank)
        if second:
            r2_ref[hs // 2] = rank.astype(BF16)
        tops.append(vals)
    v1 = [jnp.concatenate([tops[2 * h][r] for h in range(PEER_HEADS)], axis=0) for r in range(PEER_TOPK + 1)]
    v2 = [jnp.concatenate([tops[2 * h + 1][r] for h in range(PEER_HEADS)], axis=0) for r in range(PEER_TOPK + 1)]
    cand = [v1[a] + v2[b] for a, b in PAIR_CELLS]
    cur = cand
    kth = None
    for r in range(PEER_TOPK + 1):
        m = functools.reduce(jnp.maximum, cur)
        if r == PEER_TOPK - 1:
            kth = m
        if r < PEER_TOPK:
            cur = [jnp.where(c == m, neg, c) for c in cur]
    thr = 0.5 * (kth + m)
    top = v1[0] + v2[0]
    z = functools.reduce(lambda p, c: p + jnp.where(c >= thr, jnp.exp(c - top), 0.0), cand, jnp.zeros_like(top))
    rz = 1.0 / z
    for h in range(PEER_HEADS):
        s1 = s_scr[2 * h]
        theta = thr[h:h + 1] - s1
        cnt = functools.reduce(lambda p, vb: p + jnp.where(vb[h:h + 1] > theta, 1.0, 0.0), v2[:PEER_TOPK],
                               jnp.zeros_like(theta))
        cnt_ref[h] = cnt
        a_ref[h] = jnp.exp(s1 - v1[0][h:h + 1]) * rz[h:h + 1]
        b_ref[h] = jnp.exp(s_scr[2 * h + 1] - v2[0][h:h + 1]).astype(BF16)


def _route(x2d, g, sc, sh, wq, keys):
    rows = x2d.shape[0]
    per_b = SEQ // ROUTE_TM
    per_batch = pl.BlockSpec((None, 1, D_MODEL), lambda i: (i // per_b, 0, 0))
    tbl = pl.BlockSpec((PEER_HEADS, N_KEYS, ROUTE_TM), lambda i: (0, 0, i))
    row_tbl = jax.ShapeDtypeStruct((PEER_HEADS, N_KEYS, rows), F32)
    col_tbl = jax.ShapeDtypeStruct((PEER_HEADS, N_KEYS, rows), BF16)
    return pl.pallas_call(
        _route_kernel,
        grid=(rows // ROUTE_TM,),
        in_specs=[
            pl.BlockSpec((ROUTE_TM, D_MODEL), lambda i: (i, 0)),
            pl.BlockSpec((1, D_MODEL), lambda i: (0, 0)),
            per_batch, per_batch,
            pl.BlockSpec((D_MODEL, 2 * PEER_HEADS * PEER_HALF), lambda i: (0, 0)),
            pl.BlockSpec((2 * PEER_HEADS, N_KEYS, PEER_HALF), lambda i: (0, 0, 0)),
        ],
        out_specs=[pl.BlockSpec((D_MODEL, ROUTE_TM), lambda i: (0, i)), tbl, tbl, tbl, tbl],
        out_shape=[jax.ShapeDtypeStruct((D_MODEL, rows), BF16), row_tbl, row_tbl, col_tbl, col_tbl],
        scratch_shapes=[pltpu.VMEM((2 * PEER_HEADS, N_KEYS, ROUTE_TM), F32)],
        compiler_params=_params("arbitrary"),
        name="peer_route",
    )(x2d, g, sc, sh, wq, keys)


DENSE_TM = 512
DENSE_TE = 2048
DENSE_SUB = 256
DENSE_JC = 32


def _dense_kernel(t_ref, cnt_ref, a_ref, r2_ref, b_ref, eu_ref, evt_ref, x_ref, g2_ref, fg_ref,
                  out_ref, acc_ref, act_ref, g_ref, rt_ref, *, final_norm):
    e = pl.program_id(1)

    @pl.when(e == 0)
    def _():
        acc_ref[...] = jnp.zeros_like(acc_ref)

    tm = t_ref.shape[1]
    zero = jnp.zeros((), BF16)

    def row_tile(ref, h, i):
        return jnp.broadcast_to(ref[h, pl.ds(i, 1), :], (16, tm)).astype(BF16)

    def tiled(ref, slot, ii, k, h):
        return jnp.concatenate([ref[slot, ii, k, h]] * (DENSE_JC // 16), axis=0)

    rows_per_sub = DENSE_SUB // N_KEYS
    n_sub = DENSE_TE // DENSE_SUB

    def activations(sb, slot):
        start = sb * DENSE_SUB
        if not isinstance(sb, int):
            start = pl.multiple_of(start, DENSE_SUB)
        act_ref[slot] = jnp.dot(eu_ref[pl.ds(start, DENSE_SUB), :], t_ref[...],
                                preferred_element_type=F32)

    def gating(sb, slot):
        i0 = e * (DENSE_TE // N_KEYS) + sb * rows_per_sub
        for ii in range(rows_per_sub):
            for h in range(PEER_HEADS):
                rt_ref[slot, ii, 0, h] = row_tile(cnt_ref, h, i0 + ii)
                rt_ref[slot, ii, 1, h] = row_tile(a_ref, h, i0 + ii)
        for jc in range(N_KEYS // DENSE_JC):
            js = slice(jc * DENSE_JC, (jc + 1) * DENSE_JC)
            w = [None] * rows_per_sub
            for h in range(PEER_HEADS):
                r2 = r2_ref[h, js, :]
                bb = b_ref[h, js, :]
                for ii in range(rows_per_sub):
                    term = jnp.where(r2 < tiled(rt_ref, slot, ii, 0, h), bb * tiled(rt_ref, slot, ii, 1, h), zero)
                    w[ii] = term if w[ii] is None else w[ii] + term
            for ii in range(rows_per_sub):
                rows = slice(ii * N_KEYS + jc * DENSE_JC, ii * N_KEYS + (jc + 1) * DENSE_JC)
                g_ref[slot, rows, :] = _gelu_packed(act_ref[slot, rows, :].astype(BF16)) * w[ii]

    def accumulate(sb, slot):
        acc_ref[...] += jnp.dot(evt_ref[sb], g_ref[slot], preferred_element_type=F32)

    activations(0, 0)
    activations(1, 1)
    gating(0, 0)

    def steady(k, carry):
        sb = 2 * k + 1
        activations(sb + 1, 0)
        gating(sb, 1)
        accumulate(sb - 1, 0)
        activations(sb + 2, 1)
        gating(sb + 1, 0)
        accumulate(sb, 1)
        return carry

    for k in range(n_sub // 2 - 1):
        steady(k, 0)
    gating(n_sub - 1, 1)
    accumulate(n_sub - 2, 0)
    accumulate(n_sub - 1, 1)

    @pl.when(e == pl.num_programs(1) - 1)
    def _():
        y = x_ref[...] + g2_ref[...] * acc_ref[...].T
        if final_norm:
            y = _rms(y, fg_ref[...])
        out_ref[...] = y


def _dense(t, cnt, a, r2, b, eu, evt, x2d, g2, fg, final_norm):
    rows = x2d.shape[0]
    per_b = SEQ // DENSE_TM
    tbl = pl.BlockSpec((PEER_HEADS, N_KEYS, DENSE_TM), lambda i, e: (0, 0, i))
    return pl.pallas_call(
        functools.partial(_dense_kernel, final_norm=final_norm),
        grid=(rows // DENSE_TM, N_EXPERTS // DENSE_TE),
        in_specs=[
            pl.BlockSpec((D_MODEL, DENSE_TM), lambda i, e: (0, i)),
            tbl, tbl, tbl, tbl,
            pl.BlockSpec((DENSE_TE, D_MODEL), lambda i, e: (e, 0)),
            pl.BlockSpec((DENSE_TE // DENSE_SUB, D_MODEL, DENSE_SUB), lambda i, e: (e, 0, 0)),
            pl.BlockSpec((DENSE_TM, D_MODEL), lambda i, e: (i, 0)),
            pl.BlockSpec((None, 1, D_MODEL), lambda i, e: (i // per_b, 0, 0)),
            pl.BlockSpec((1, D_MODEL), lambda i, e: (0, 0)),
        ],
        out_specs=pl.BlockSpec((DENSE_TM, D_MODEL), lambda i, e: (i, 0)),
        out_shape=jax.ShapeDtypeStruct((rows, D_MODEL), F32),
        scratch_shapes=[pltpu.VMEM((D_MODEL, DENSE_TM), F32), pltpu.VMEM((2, DENSE_SUB, DENSE_TM), F32),
                        pltpu.VMEM((2, DENSE_SUB, DENSE_TM), BF16),
                        pltpu.VMEM((2, DENSE_SUB // N_KEYS, 2, PEER_HEADS, 16, DENSE_TM), BF16)],
        compiler_params=_params("arbitrary", "arbitrary"),
        name="peer_dense",
    )(t, cnt, a, r2, b, eu, evt, x2d, g2, fg)


def _peer(x2d, g, sc, sh, g2, wq, keys, eu, ev, fg, final_norm):
    wq_b = wq.astype(BF16)
    keys_b = keys.reshape(2 * PEER_HEADS, N_KEYS, PEER_HALF).astype(BF16)
    t, cnt, a, r2, b = _route(x2d, g, sc, sh, wq_b, keys_b)
    evt = ev.astype(BF16).reshape(N_EXPERTS // DENSE_SUB, DENSE_SUB, D_MODEL).transpose(0, 2, 1)
    return _dense(t, cnt, a, r2, b, eu.astype(BF16), evt, x2d, g2, fg, final_norm)


def kernel(x, c, ctx, c_ctx, ada_w, ada_b, norm1_g, norm2_g, final_g, even_w_in, even_w_out, diff_lambda,
           diff_norm_g, odd_w_in, odd_b_in, sgu_norm_g, sgu_w, sgu_b, odd_w_out, peer_wq, peer_keys, peer_u, peer_v):
    cvec = jnp.concatenate([c, c_ctx[None, :], jnp.zeros((8 - BATCH - 1, D_MODEL), F32)], axis=0)
    mod = _ada(cvec, ada_w, ada_b).reshape(2, 8, 6, D_MODEL)
    row = lambda v: v.reshape(1, -1)
    per_batch = lambda l, j: mod[l, :BATCH, j].reshape(BATCH, 1, D_MODEL)
    fg = row(final_g)

    both = lambda j: jnp.stack([mod[0, :BATCH, j], jnp.broadcast_to(mod[0, BATCH, j], (BATCH, D_MODEL))],
                               axis=1).reshape(BATCH, 2, 1, D_MODEL)
    xin = jnp.concatenate([x, ctx], axis=1).reshape(BATCH * SEQ_ALL, D_MODEL)
    cos, sa, sb = _rope_tables()
    qkvf = _inproj(xin, row(norm1_g[0]), both(1), both(0), even_w_in[0].astype(BF16), cos, sa, sb)
    qkvf = qkvf.reshape(BATCH, SEQ_ALL, EVEN_IN)
    o = _attn(qkvf, diff_lambda[0], row(diff_norm_g[0]))
    fm = _fourier(qkvf, _seq_dft())
    x2d = x.reshape(BATCH * SEQ, D_MODEL)
    x2d = _outproj(o.reshape(BATCH * SEQ, DIFF_QK), fm.reshape(BATCH * SEQ, FOURIER_W),
                   even_w_out[0].astype(BF16), x2d, per_batch(0, 2))
    x2d = _peer(x2d, row(norm2_g[0]), per_batch(0, 4), per_batch(0, 3), per_batch(0, 5),
                peer_wq[0], peer_keys[0], peer_u[0], peer_v[0], fg, False)

    bs = jnp.broadcast_to(sgu_b[0][:, :, None], (SGU_GROUPS, CHUNK, CHUNK))
    x2d = _sgu(x2d, row(norm1_g[1]), per_batch(1, 1), per_batch(1, 0), odd_w_in[0].astype(BF16),
               row(odd_b_in[0]), row(sgu_norm_g[0]), sgu_w[0].astype(BF16), bs, odd_w_out[0].astype(BF16),
               per_batch(1, 2))
    x2d = _peer(x2d, row(norm2_g[1]), per_batch(1, 4), per_batch(1, 3), per_batch(1, 5),
                peer_wq[1], peer_keys[1], peer_u[1], peer_v[1], fg, True)
    return x2d.reshape(BATCH, SEQ, D_MODEL)
```

```python
import functools
import math

import numpy as np
import jax
import jax.numpy as jnp
from jax import lax
from jax.experimental import pallas as pl
from jax.experimental.pallas import tpu as pltpu

F32 = jnp.float32
BF16 = jnp.bfloat16

D_MODEL = 1024
BATCH = 4
SEQ = 4096
GRID_W = 64
CTX_LEN = 256
SEQ_ALL = CTX_LEN + SEQ
EPS = 1e-6

DIFF_HEADS = 6
DIFF_DH = 64
HEAD_W = 2 * DIFF_DH
DIFF_QK = DIFF_HEADS * HEAD_W
FOURIER_GROUPS = 4
FOURIER_GC = 64
FOURIER_W = FOURIER_GROUPS * FOURIER_GC
EVEN_IN = 3 * DIFF_QK + FOURIER_W
ROPE_BASE = 10000.0
ROPE_PAIRS = DIFF_DH // 4
LAM_INIT_L1 = 0.8 - 0.6 * math.exp(-0.3 * 0.0)
LOG2_E = math.log2(math.e)

CHUNK = 128
SGU_GROUPS = 8
SGU_W = SGU_GROUPS * CHUNK

PEER_HEADS = 8
N_KEYS = 128
N_EXPERTS = N_KEYS * N_KEYS
PEER_TOPK = 16
PEER_HALF = 128
LANES = 128

VMEM_LIMIT = 56 * 1024 * 1024

NT_DIMS = (((1,), (1,)), ((), ()))


def _rms(t, g):
    return t * lax.rsqrt(jnp.mean(t * t, axis=-1, keepdims=True) + EPS) * g


def _gelu(x):
    cdf = 0.5 * (1.0 + jnp.tanh(math.sqrt(2.0 / math.pi) * (x + 0.044715 * (x * x * x))))
    return x * cdf


def _gelu_packed(x):
    c = math.sqrt(2.0 / math.pi)
    half = 0.5 * x
    return half + half * jnp.tanh(x * (c + (c * 0.044715) * (x * x)))


def _params(*sem, flags=None):
    return pltpu.CompilerParams(dimension_semantics=sem, vmem_limit_bytes=VMEM_LIMIT, flags=flags)


def _ada_kernel(c_ref, w_ref, b_ref, o_ref):
    c = c_ref[...]
    s = c * jax.nn.sigmoid(c)
    o_ref[...] = jnp.dot(s, w_ref[...], preferred_element_type=F32) + b_ref[...]


def _ada(cvec, ada_w, ada_b):
    depth = ada_w.shape[0]
    tn = 1536
    return pl.pallas_call(
        _ada_kernel,
        grid=(depth, 6 * D_MODEL // tn),
        in_specs=[
            pl.BlockSpec((8, D_MODEL), lambda l, j: (0, 0)),
            pl.BlockSpec((None, D_MODEL, tn), lambda l, j: (l, 0, j)),
            pl.BlockSpec((None, 1, tn), lambda l, j: (l, 0, j)),
        ],
        out_specs=pl.BlockSpec((None, 8, tn), lambda l, j: (l, 0, j)),
        out_shape=jax.ShapeDtypeStruct((depth, 8, 6 * D_MODEL), F32),
        compiler_params=_params("arbitrary", "arbitrary"),
        name="ada",
    )(cvec, ada_w, ada_b.reshape(depth, 1, 6 * D_MODEL))


IN_TM = 256
TILES_PER_SEQ = SEQ_ALL // IN_TM


def _inproj_kernel(x_ref, g_ref, sc_ref, sh_ref, w_ref, cos_ref, sa_ref, sb_ref, o_ref):
    x = x_ref[...]
    h = _rms(x, g_ref[...]) * (1.0 + sc_ref[...]) + sh_ref[...]
    r = jnp.dot(h.astype(BF16), w_ref[...], preferred_element_type=F32)
    cos, sa, sb = cos_ref[...], sa_ref[...], sb_ref[...]
    for cb in range(2 * DIFF_HEADS):
        t = r[:, cb * HEAD_W:(cb + 1) * HEAD_W]
        rot = t * cos + pltpu.roll(t, HEAD_W - 16, 1) * sa + pltpu.roll(t, 16, 1) * sb
        if cb < DIFF_HEADS:
            rot = rot * (DIFF_DH ** -0.5 * LOG2_E)
        o_ref[:, cb * HEAD_W:(cb + 1) * HEAD_W] = rot.astype(BF16)
    o_ref[:, 2 * DIFF_QK:] = r[:, 2 * DIFF_QK:].astype(BF16)


def _inproj(xin, g, sc, sh, w, cos, sa, sb):
    rows = xin.shape[0]
    mod_spec = pl.BlockSpec((None, None, 1, D_MODEL),
                            lambda i: (i // TILES_PER_SEQ, jnp.where(i % TILES_PER_SEQ == TILES_PER_SEQ - 1, 1, 0), 0, 0))
    rope_spec = pl.BlockSpec((IN_TM, HEAD_W), lambda i: (i % TILES_PER_SEQ, 0))
    return pl.pallas_call(
        _inproj_kernel,
        grid=(rows // IN_TM,),
        in_specs=[
            pl.BlockSpec((IN_TM, D_MODEL), lambda i: (i, 0)),
            pl.BlockSpec((1, D_MODEL), lambda i: (0, 0)),
            mod_spec, mod_spec,
            pl.BlockSpec((D_MODEL, EVEN_IN), lambda i: (0, 0)),
            rope_spec, rope_spec, rope_spec,
        ],
        out_specs=pl.BlockSpec((IN_TM, EVEN_IN), lambda i: (i, 0)),
        out_shape=jax.ShapeDtypeStruct((rows, EVEN_IN), BF16),
        compiler_params=_params("arbitrary"),
        name="inproj",
    )(xin, g, sc, sh, w, cos, sa, sb)


def _rope_tables():
    rows = SEQ // GRID_W
    r = jnp.repeat(jnp.arange(rows, dtype=F32), GRID_W)
    col = jnp.tile(jnp.arange(GRID_W, dtype=F32), rows)
    inv = ROPE_BASE ** (-jnp.arange(ROPE_PAIRS, dtype=F32) / ROPE_PAIRS)
    ar = r[:, None] * inv
    ac = col[:, None] * inv
    ang = jnp.concatenate([ar, ar, ac, ac] * 2, axis=-1)
    cos, sin = jnp.cos(ang), jnp.sin(ang)
    even = ((np.arange(HEAD_W) // ROPE_PAIRS) % 2 == 0)[None, :]
    sa = jnp.where(even, -sin, 0.0)
    sb = jnp.where(even, 0.0, sin)
    ident = jnp.ones((CTX_LEN, HEAD_W), F32)
    zero = jnp.zeros((CTX_LEN, HEAD_W), F32)
    return (jnp.concatenate([cos, ident], 0), jnp.concatenate([sa, zero], 0),
            jnp.concatenate([sb, zero], 0))


ATT_TQ = 512
ATT_CHAINS = 2


def _attn_kernel(lam_ref, g_ref, q_ref, k_ref, v_ref, o_ref, vt_ref):
    @pl.when(pl.program_id(2) == 0)
    def _():
        vt_ref[...] = v_ref[...].astype(F32).T.astype(BF16)

    lane = lax.broadcasted_iota(jnp.int32, (1, HEAD_W), 1)
    lp = lam_ref[...]
    lam = (jnp.exp(jnp.sum(lp[0:1] * lp[1:2], axis=-1, keepdims=True))
           - jnp.exp(jnp.sum(lp[2:3] * lp[3:4], axis=-1, keepdims=True)) + LAM_INIT_L1)
    tq = ATT_TQ // ATT_CHAINS
    for c in range(ATT_CHAINS):
        rows = slice(c * tq, (c + 1) * tq)
        q = q_ref[rows, :].astype(F32)
        qq = jnp.concatenate([jnp.where(lane < DIFF_DH, q, 0.0), jnp.where(lane >= DIFF_DH, q, 0.0)], axis=0)
        st = lax.dot_general(k_ref[...], qq.astype(BF16), NT_DIMS, preferred_element_type=F32)
        m = jnp.max(st, axis=0, keepdims=True)
        e = jnp.exp2(st - m)
        z = jnp.sum(e, axis=0, keepdims=True)
        ot = jnp.dot(vt_ref[...], e.astype(BF16), preferred_element_type=F32) / z
        o = (ot[:, :tq] - lam * ot[:, tq:]).T
        o_ref[rows, :] = (_rms(o, g_ref[...]) * (1.0 - LAM_INIT_L1)).astype(BF16)


def _attn(qkvf, lam_p, head_g):
    nq = SEQ // ATT_TQ
    return pl.pallas_call(
        _attn_kernel,
        grid=(BATCH, DIFF_HEADS, nq),
        in_specs=[
            pl.BlockSpec((4, DIFF_DH), lambda b, h, i: (0, 0)),
            pl.BlockSpec((1, HEAD_W), lambda b, h, i: (0, 0)),
            pl.BlockSpec((None, ATT_TQ, HEAD_W), lambda b, h, i: (b, i, h)),
            pl.BlockSpec((None, SEQ_ALL, HEAD_W), lambda b, h, i: (b, 0, DIFF_HEADS + h)),
            pl.BlockSpec((None, SEQ_ALL, HEAD_W), lambda b, h, i: (b, 0, 2 * DIFF_HEADS + h)),
        ],
        out_specs=pl.BlockSpec((None, ATT_TQ, HEAD_W), lambda b, h, i: (b, i, h)),
        out_shape=jax.ShapeDtypeStruct((BATCH, SEQ, DIFF_QK), BF16),
        scratch_shapes=[pltpu.VMEM((HEAD_W, SEQ_ALL), BF16)],
        compiler_params=_params("arbitrary", "arbitrary", "arbitrary"),
        name="diff_attn",
    )(lam_p, head_g, qkvf, qkvf, qkvf)


def _channel_dft():
    n = np.arange(FOURIER_GC)
    ang = 2.0 * np.pi * np.outer(n, n) / FOURIER_GC
    eye = np.eye(FOURIER_GROUPS)
    scale = FOURIER_GC ** -0.5
    return np.concatenate([np.kron(eye, np.cos(ang)), np.kron(eye, np.sin(ang))], axis=1) * scale


def _seq_dft():
    k = jnp.arange(SEQ, dtype=jnp.int32)[:, None]
    n = jnp.arange(GRID_W, dtype=jnp.int32)[None, :]
    a = ((k * n) % GRID_W).astype(F32) * (2.0 * math.pi / GRID_W)
    b = ((k * n) % SEQ).astype(F32) * (2.0 * math.pi / SEQ)
    ca, sa, cb, sb = jnp.cos(a)[:, :, None], jnp.sin(a)[:, :, None], jnp.cos(b)[:, None, :], jnp.sin(b)[:, None, :]
    scale = SEQ ** -0.5
    cos = ((ca * cb - sa * sb) * scale).reshape(SEQ, SEQ)
    sin = ((sa * cb + ca * sb) * scale).reshape(SEQ, SEQ)
    return jnp.concatenate([cos, -sin], axis=1).astype(BF16)


def _fourier_chan_kernel(f_ref, bd_ref, o_ref):
    f = f_ref[:SEQ, :]
    uv = jnp.dot(f, bd_ref[...], preferred_element_type=F32)
    o_ref[:SEQ, :] = uv[:, :FOURIER_W].astype(BF16)
    o_ref[SEQ:, :] = uv[:, FOURIER_W:].astype(BF16)


def _fourier_seq_kernel(t_ref, uv_ref, o_ref):
    o_ref[...] = jnp.dot(t_ref[...], uv_ref[...], preferred_element_type=F32).astype(BF16)


FOURIER_TM = 512


def _fourier(qkvf, seq_dft):
    bd = jnp.asarray(_channel_dft(), dtype=BF16)
    uv = pl.pallas_call(
        _fourier_chan_kernel,
        grid=(BATCH,),
        in_specs=[
            pl.BlockSpec((None, SEQ_ALL, FOURIER_W), lambda b: (b, 0, 3 * DIFF_QK // FOURIER_W)),
            pl.BlockSpec((FOURIER_W, 2 * FOURIER_W), lambda b: (0, 0)),
        ],
        out_specs=pl.BlockSpec((None, 2 * SEQ, FOURIER_W), lambda b: (b, 0, 0)),
        out_shape=jax.ShapeDtypeStruct((BATCH, 2 * SEQ, FOURIER_W), BF16),
        compiler_params=_params("arbitrary"),
        name="fourier_chan",
    )(qkvf, bd)
    return pl.pallas_call(
        _fourier_seq_kernel,
        grid=(SEQ // FOURIER_TM, BATCH),
        in_specs=[
            pl.BlockSpec((FOURIER_TM, 2 * SEQ), lambda i, b: (i, 0)),
            pl.BlockSpec((None, 2 * SEQ, FOURIER_W), lambda i, b: (b, 0, 0)),
        ],
        out_specs=pl.BlockSpec((None, FOURIER_TM, FOURIER_W), lambda i, b: (b, i, 0)),
        out_shape=jax.ShapeDtypeStruct((BATCH, SEQ, FOURIER_W), BF16),
        compiler_params=_params("arbitrary", "arbitrary"),
        name="fourier_seq",
    )(seq_dft, uv)


OUT_TM = 512


def _outproj_kernel(o_ref, f_ref, w_ref, x_ref, g_ref, out_ref):
    y = jnp.dot(o_ref[...], w_ref[:DIFF_QK, :], preferred_element_type=F32)
    y = y + jnp.dot(f_ref[...], w_ref[DIFF_QK:, :], preferred_element_type=F32)
    out_ref[...] = x_ref[...] + g_ref[...] * y


def _outproj(o, fm, w, x2d, g1):
    rows = x2d.shape[0]
    per_b = SEQ // OUT_TM
    return pl.pallas_call(
        _outproj_kernel,
        grid=(rows // OUT_TM,),
        in_specs=[
            pl.BlockSpec((OUT_TM, DIFF_QK), lambda i: (i, 0)),
            pl.BlockSpec((OUT_TM, FOURIER_W), lambda i: (i, 0)),
            pl.BlockSpec((D_MODEL, D_MODEL), lambda i: (0, 0)),
            pl.BlockSpec((OUT_TM, D_MODEL), lambda i: (i, 0)),
            pl.BlockSpec((None, 1, D_MODEL), lambda i: (i // per_b, 0, 0)),
        ],
        out_specs=pl.BlockSpec((OUT_TM, D_MODEL), lambda i: (i, 0)),
        out_shape=jax.ShapeDtypeStruct((rows, D_MODEL), F32),
        compiler_params=_params("arbitrary"),
        name="outproj",
    )(o, fm, w, x2d, g1)


SGU_TM = 256


def _sgu_kernel(x_ref, g_ref, sc_ref, sh_ref, win_ref, bin_ref, ng_ref, ws_ref, bs_ref, wout_ref, g1_ref,
                out_ref, us_ref):
    x = x_ref[...]
    h = _rms(x, g_ref[...]) * (1.0 + sc_ref[...]) + sh_ref[...]
    z = _gelu(jnp.dot(h.astype(BF16), win_ref[...], preferred_element_type=F32) + bin_ref[...])
    u = z[:, :SGU_W]
    v = _rms(z[:, SGU_W:], ng_ref[...]).astype(BF16)
    for ck in range(SGU_TM // CHUNK):
        rows = slice(ck * CHUNK, (ck + 1) * CHUNK)
        for g in range(SGU_GROUPS):
            cols = slice(g * CHUNK, (g + 1) * CHUNK)
            s = jnp.dot(ws_ref[g], v[rows, cols], preferred_element_type=F32) + bs_ref[g]
            us_ref[rows, cols] = (u[rows, cols] * s).astype(BF16)
    y = jnp.dot(us_ref[...], wout_ref[...], preferred_element_type=F32)
    out_ref[...] = x + g1_ref[...] * y


def _sgu(x2d, g, sc, sh, win, b_in, ng, ws, bs, wout, g1):
    rows = x2d.shape[0]
    per_b = SEQ // SGU_TM
    vec = lambda n: pl.BlockSpec((1, n), lambda i: (0, 0))
    per_batch = pl.BlockSpec((None, 1, D_MODEL), lambda i: (i // per_b, 0, 0))
    return pl.pallas_call(
        _sgu_kernel,
        grid=(rows // SGU_TM,),
        in_specs=[
            pl.BlockSpec((SGU_TM, D_MODEL), lambda i: (i, 0)),
            vec(D_MODEL), per_batch, per_batch,
            pl.BlockSpec((D_MODEL, 2 * SGU_W), lambda i: (0, 0)),
            vec(2 * SGU_W), vec(SGU_W),
            pl.BlockSpec((SGU_GROUPS, CHUNK, CHUNK), lambda i: (0, 0, 0)),
            pl.BlockSpec((SGU_GROUPS, CHUNK, CHUNK), lambda i: (0, 0, 0)),
            pl.BlockSpec((SGU_W, D_MODEL), lambda i: (0, 0)),
            per_batch,
        ],
        out_specs=pl.BlockSpec((SGU_TM, D_MODEL), lambda i: (i, 0)),
        out_shape=jax.ShapeDtypeStruct((rows, D_MODEL), F32),
        scratch_shapes=[pltpu.VMEM((SGU_TM, SGU_W), BF16)],
        compiler_params=_params("arbitrary"),
        name="sgu",
    )(x2d, g, sc, sh, win, b_in, ng, ws, bs, wout, g1)


ROUTE_TM = 256
PAIR_CELLS = [(a, b) for a in range(PEER_TOPK + 1) for b in range(PEER_TOPK + 1)
              if (a + 1) * (b + 1) <= PEER_TOPK + 1]


def _route_kernel(x_ref, g_ref, sc_ref, sh_ref, wq_ref, keys_ref,
                  t_ref, cnt_ref, a_ref, r2_ref, b_ref, s_scr):
    x = x_ref[...]
    tf = _rms(x, g_ref[...]) * (1.0 + sc_ref[...]) + sh_ref[...]
    t_ref[...] = tf.T.astype(BF16)
    q = jnp.dot(tf.astype(BF16), wq_ref[...], preferred_element_type=F32)
    neg = jnp.float32(-jnp.inf)
    tops = []
    for hs in range(2 * PEER_HEADS):
        qh = q[:, hs * PEER_HALF:(hs + 1) * PEER_HALF].astype(BF16)
        st = lax.dot_general(keys_ref[hs], qh, NT_DIMS, preferred_element_type=F32)
        s_scr[hs] = st
        second = hs % 2 == 1
        rank = jnp.full_like(st, float(PEER_TOPK))
        vals = []
        for r in range(PEER_TOPK + 1):
            m = jnp.max(st, axis=0, keepdims=True)
            vals.append(m)
            if r < PEER_TOPK:
                hit = st == m
                st = jnp.where(hit, neg, st)
                if second:
                    rank = jnp.where(hit, float(r), rank)
        if second:
            r2_ref[hs // 2] = rank.astype(BF16)
        tops.append(vals)
    v1 = [jnp.concatenate([tops[2 * h][r] for h in range(PEER_HEADS)], axis=0) for r in range(PEER_TOPK + 1)]
    v2 = [jnp.concatenate([tops[2 * h + 1][r] for h in range(PEER_HEADS)], axis=0) for r in range(PEER_TOPK + 1)]
    cand = [v1[a] + v2[b] for a, b in PAIR_CELLS]
    cur = cand
    kth = None
    for r in range(PEER_TOPK + 1):
        m = functools.reduce(jnp.maximum, cur)
        if r == PEER_TOPK - 1:
            kth = m
        if r < PEER_TOPK:
            cur = [jnp.where(c == m, neg, c) for c in cur]
    thr = 0.5 * (kth + m)
    top = v1[0] + v2[0]
    z = functools.reduce(lambda p, c: p + jnp.where(c >= thr, jnp.exp(c - top), 0.0), cand, jnp.zeros_like(top))
    rz = 1.0 / z
    for h in range(PEER_HEADS):
        s1 = s_scr[2 * h]
        theta = thr[h:h + 1] - s1
        cnt = functools.reduce(lambda p, vb: p + jnp.where(vb[h:h + 1] > theta, 1.0, 0.0), v2[:PEER_TOPK],
                               jnp.zeros_like(theta))
        cnt_ref[h] = cnt
        a_ref[h] = jnp.exp(s1 - v1[0][h:h + 1]) * rz[h:h + 1]
        b_ref[h] = jnp.exp(s_scr[2 * h + 1] - v2[0][h:h + 1]).astype(BF16)


def _route(x2d, g, sc, sh, wq, keys):
    rows = x2d.shape[0]
    per_b = SEQ // ROUTE_TM
    per_batch = pl.BlockSpec((None, 1, D_MODEL), lambda i: (i // per_b, 0, 0))
    tbl = pl.BlockSpec((PEER_HEADS, N_KEYS, ROUTE_TM), lambda i: (0, 0, i))
    row_tbl = jax.ShapeDtypeStruct((PEER_HEADS, N_KEYS, rows), F32)
    col_tbl = jax.ShapeDtypeStruct((PEER_HEADS, N_KEYS, rows), BF16)
    return pl.pallas_call(
        _route_kernel,
        grid=(rows // ROUTE_TM,),
        in_specs=[
            pl.BlockSpec((ROUTE_TM, D_MODEL), lambda i: (i, 0)),
            pl.BlockSpec((1, D_MODEL), lambda i: (0, 0)),
            per_batch, per_batch,
            pl.BlockSpec((D_MODEL, 2 * PEER_HEADS * PEER_HALF), lambda i: (0, 0)),
            pl.BlockSpec((2 * PEER_HEADS, N_KEYS, PEER_HALF), lambda i: (0, 0, 0)),
        ],
        out_specs=[pl.BlockSpec((D_MODEL, ROUTE_TM), lambda i: (0, i)), tbl, tbl, tbl, tbl],
        out_shape=[jax.ShapeDtypeStruct((D_MODEL, rows), BF16), row_tbl, row_tbl, col_tbl, col_tbl],
        scratch_shapes=[pltpu.VMEM((2 * PEER_HEADS, N_KEYS, ROUTE_TM), F32)],
        compiler_params=_params("arbitrary"),
        name="peer_route",
    )(x2d, g, sc, sh, wq, keys)


DENSE_TM = 512
DENSE_TE = 2048
DENSE_SUB = 512
DENSE_SHARE = 2
DENSE_JC = 32


def _dense_kernel(t_ref, cnt_ref, a_ref, r2_ref, b_ref, eu_ref, evt_ref, x_ref, g2_ref, fg_ref,
                  out_ref, acc_ref, act_ref, g_ref, rt_ref, *, final_norm):
    e = pl.program_id(1)

    @pl.when(e == 0)
    def _():
        acc_ref[...] = jnp.zeros_like(acc_ref)

    tm = t_ref.shape[1]
    zero = jnp.zeros((), BF16)

    def row_tile(ref, h, i):
        return jnp.broadcast_to(ref[h, pl.ds(i, 1), :], (16, tm)).astype(BF16)

    def tiled(ref, slot, ii, k, h):
        return jnp.concatenate([ref[slot, ii, k, h]] * (DENSE_JC // 16), axis=0)

    rows_per_sub = DENSE_SUB // N_KEYS
    n_sub = DENSE_TE // DENSE_SUB

    def activations(sb, slot):
        start = sb * DENSE_SUB
        if not isinstance(sb, int):
            start = pl.multiple_of(start, DENSE_SUB)
        act_ref[slot] = jnp.dot(eu_ref[pl.ds(start, DENSE_SUB), :], t_ref[...],
                                preferred_element_type=F32)

    def gating(sb, slot):
        i0 = e * (DENSE_TE // N_KEYS) + sb * rows_per_sub
        for ii in range(rows_per_sub):
            for h in range(PEER_HEADS):
                rt_ref[slot, ii, 0, h] = row_tile(cnt_ref, h, i0 + ii)
                rt_ref[slot, ii, 1, h] = row_tile(a_ref, h, i0 + ii)
        for pair in range(0, rows_per_sub, DENSE_SHARE):
            for jc in range(N_KEYS // DENSE_JC):
                js = slice(jc * DENSE_JC, (jc + 1) * DENSE_JC)
                w = [None] * DENSE_SHARE
                for h in range(PEER_HEADS):
                    r2 = r2_ref[h, js, :]
                    bb = b_ref[h, js, :]
                    for s in range(DENSE_SHARE):
                        ii = pair + s
                        term = jnp.where(r2 < tiled(rt_ref, slot, ii, 0, h), bb * tiled(rt_ref, slot, ii, 1, h),
                                         zero)
                        w[s] = term if w[s] is None else w[s] + term
                for s in range(DENSE_SHARE):
                    ii = pair + s
                    rows = slice(ii * N_KEYS + jc * DENSE_JC, ii * N_KEYS + (jc + 1) * DENSE_JC)
                    g_ref[slot, rows, :] = _gelu_packed(act_ref[slot, rows, :].astype(BF16)) * w[s]

    def accumulate(sb, slot):
        acc_ref[...] += jnp.dot(evt_ref[sb], g_ref[slot], preferred_element_type=F32)

    activations(0, 0)
    activations(1, 1)
    gating(0, 0)
    for sb in range(1, n_sub - 1, 2):
        activations(sb + 1, 0)
        gating(sb, 1)
        accumulate(sb - 1, 0)
        activations(sb + 2, 1)
        gating(sb + 1, 0)
        accumulate(sb, 1)
    gating(n_sub - 1, 1)
    accumulate(n_sub - 2, 0)
    accumulate(n_sub - 1, 1)

    @pl.when(e == pl.num_programs(1) - 1)
    def _():
        y = x_ref[...] + g2_ref[...] * acc_ref[...].T
        if final_norm:
            y = _rms(y, fg_ref[...])
        out_ref[...] = y


def _dense(t, cnt, a, r2, b, eu, evt, x2d, g2, fg, final_norm):
    rows = x2d.shape[0]
    per_b = SEQ // DENSE_TM
    tbl = pl.BlockSpec((PEER_HEADS, N_KEYS, DENSE_TM), lambda i, e: (0, 0, i))
    return pl.pallas_call(
        functools.partial(_dense_kernel, final_norm=final_norm),
        grid=(rows // DENSE_TM, N_EXPERTS // DENSE_TE),
        in_specs=[
            pl.BlockSpec((D_MODEL, DENSE_TM), lambda i, e: (0, i)),
            tbl, tbl, tbl, tbl,
            pl.BlockSpec((DENSE_TE, D_MODEL), lambda i, e: (e, 0)),
            pl.BlockSpec((DENSE_TE // DENSE_SUB, D_MODEL, DENSE_SUB), lambda i, e: (e, 0, 0)),
            pl.BlockSpec((DENSE_TM, D_MODEL), lambda i, e: (i, 0)),
            pl.BlockSpec((None, 1, D_MODEL), lambda i, e: (i // per_b, 0, 0)),
            pl.BlockSpec((1, D_MODEL), lambda i, e: (0, 0)),
        ],
        out_specs=pl.BlockSpec((DENSE_TM, D_MODEL), lambda i, e: (i, 0)),
        out_shape=jax.ShapeDtypeStruct((rows, D_MODEL), F32),
        scratch_shapes=[pltpu.VMEM((D_MODEL, DENSE_TM), F32), pltpu.VMEM((2, DENSE_SUB, DENSE_TM), F32),
                        pltpu.VMEM((2, DENSE_SUB, DENSE_TM), BF16),
                        pltpu.VMEM((2, DENSE_SUB // N_KEYS, 2, PEER_HEADS, 16, DENSE_TM), BF16)],
        compiler_params=_params("arbitrary", "arbitrary"),
        name="peer_dense",
    )(t, cnt, a, r2, b, eu, evt, x2d, g2, fg)


def _peer(x2d, g, sc, sh, g2, wq, keys, eu, ev, fg, final_norm):
    wq_b = wq.astype(BF16)
    keys_b = keys.reshape(2 * PEER_HEADS, N_KEYS, PEER_HALF).astype(BF16)
    t, cnt, a, r2, b = _route(x2d, g, sc, sh, wq_b, keys_b)
    evt = ev.astype(BF16).reshape(N_EXPERTS // DENSE_SUB, DENSE_SUB, D_MODEL).transpose(0, 2, 1)
    return _dense(t, cnt, a, r2, b, eu.astype(BF16), evt, x2d, g2, fg, final_norm)


def kernel(x, c, ctx, c_ctx, ada_w, ada_b, norm1_g, norm2_g, final_g, even_w_in, even_w_out, diff_lambda,
           diff_norm_g, odd_w_in, odd_b_in, sgu_norm_g, sgu_w, sgu_b, odd_w_out, peer_wq, peer_keys, peer_u, peer_v):
    cvec = jnp.concatenate([c, c_ctx[None, :], jnp.zeros((8 - BATCH - 1, D_MODEL), F32)], axis=0)
    mod = _ada(cvec, ada_w, ada_b).reshape(2, 8, 6, D_MODEL)
    row = lambda v: v.reshape(1, -1)
    per_batch = lambda l, j: mod[l, :BATCH, j].reshape(BATCH, 1, D_MODEL)
    fg = row(final_g)

    both = lambda j: jnp.stack([mod[0, :BATCH, j], jnp.broadcast_to(mod[0, BATCH, j], (BATCH, D_MODEL))],
                               axis=1).reshape(BATCH, 2, 1, D_MODEL)
    xin = jnp.concatenate([x, ctx], axis=1).reshape(BATCH * SEQ_ALL, D_MODEL)
    cos, sa, sb = _rope_tables()
    qkvf = _inproj(xin, row(norm1_g[0]), both(1), both(0), even_w_in[0].astype(BF16), cos, sa, sb)
    qkvf = qkvf.reshape(BATCH, SEQ_ALL, EVEN_IN)
    o = _attn(qkvf, diff_lambda[0], row(diff_norm_g[0]))
    fm = _fourier(qkvf, _seq_dft())
    x2d = x.reshape(BATCH * SEQ, D_MODEL)
    x2d = _outproj(o.reshape(BATCH * SEQ, DIFF_QK), fm.reshape(BATCH * SEQ, FOURIER_W),
                   even_w_out[0].astype(BF16), x2d, per_batch(0, 2))
    x2d = _peer(x2d, row(norm2_g[0]), per_batch(0, 4), per_batch(0, 3), per_batch(0, 5),
                peer_wq[0], peer_keys[0], peer_u[0], peer_v[0], fg, False)

    bs = jnp.broadcast_to(sgu_b[0][:, :, None], (SGU_GROUPS, CHUNK, CHUNK))
    x2d = _sgu(x2d, row(norm1_g[1]), per_batch(1, 1), per_batch(1, 0), odd_w_in[0].astype(BF16),
               row(odd_b_in[0]), row(sgu_norm_g[0]), sgu_w[0].astype(BF16), bs, odd_w_out[0].astype(BF16),
               per_batch(1, 2))
    x2d = _peer(x2d, row(norm2_g[1]), per_batch(1, 4), per_batch(1, 3), per_batch(1, 5),
                peer_wq[1], peer_keys[1], peer_u[1], peer_v[1], fg, True)
    return x2d.reshape(BATCH, SEQ, D_MODEL)
```

```python
import functools
import math

import numpy as np
import jax
import jax.numpy as jnp
from jax import lax
from jax.experimental import pallas as pl
from jax.experimental.pallas import tpu as pltpu

F32 = jnp.float32
BF16 = jnp.bfloat16

D_MODEL = 1024
BATCH = 4
SEQ = 4096
GRID_W = 64
CTX_LEN = 256
SEQ_ALL = CTX_LEN + SEQ
EPS = 1e-6

DIFF_HEADS = 6
DIFF_DH = 64
HEAD_W = 2 * DIFF_DH
DIFF_QK = DIFF_HEADS * HEAD_W
FOURIER_GROUPS = 4
FOURIER_GC = 64
FOURIER_W = FOURIER_GROUPS * FOURIER_GC
EVEN_IN = 3 * DIFF_QK + FOURIER_W
ROPE_BASE = 10000.0
ROPE_PAIRS = DIFF_DH // 4
LAM_INIT_L1 = 0.8 - 0.6 * math.exp(-0.3 * 0.0)
LOG2_E = math.log2(math.e)

CHUNK = 128
SGU_GROUPS = 8
SGU_W = SGU_GROUPS * CHUNK

PEER_HEADS = 8
N_KEYS = 128
N_EXPERTS = N_KEYS * N_KEYS
PEER_TOPK = 16
PEER_HALF = 128
LANES = 128

VMEM_LIMIT = 56 * 1024 * 1024

NT_DIMS = (((1,), (1,)), ((), ()))


def _rms(t, g):
    return t * lax.rsqrt(jnp.mean(t * t, axis=-1, keepdims=True) + EPS) * g


def _gelu(x):
    cdf = 0.5 * (1.0 + jnp.tanh(math.sqrt(2.0 / math.pi) * (x + 0.044715 * (x * x * x))))
    return x * cdf


def _gelu_packed(x):
    c = math.sqrt(2.0 / math.pi)
    half = 0.5 * x
    return half + half * jnp.tanh(x * (c + (c * 0.044715) * (x * x)))


def _params(*sem, flags=None):
    return pltpu.CompilerParams(dimension_semantics=sem, vmem_limit_bytes=VMEM_LIMIT, flags=flags)


def _ada_kernel(c_ref, w_ref, b_ref, o_ref):
    c = c_ref[...]
    s = c * jax.nn.sigmoid(c)
    o_ref[...] = jnp.dot(s, w_ref[...], preferred_element_type=F32) + b_ref[...]


def _ada(cvec, ada_w, ada_b):
    depth = ada_w.shape[0]
    tn = 1536
    return pl.pallas_call(
        _ada_kernel,
        grid=(depth, 6 * D_MODEL // tn),
        in_specs=[
            pl.BlockSpec((8, D_MODEL), lambda l, j: (0, 0)),
            pl.BlockSpec((None, D_MODEL, tn), lambda l, j: (l, 0, j)),
            pl.BlockSpec((None, 1, tn), lambda l, j: (l, 0, j)),
        ],
        out_specs=pl.BlockSpec((None, 8, tn), lambda l, j: (l, 0, j)),
        out_shape=jax.ShapeDtypeStruct((depth, 8, 6 * D_MODEL), F32),
        compiler_params=_params("arbitrary", "arbitrary"),
        name="ada",
    )(cvec, ada_w, ada_b.reshape(depth, 1, 6 * D_MODEL))


IN_TM = 256
TILES_PER_SEQ = SEQ_ALL // IN_TM


def _inproj_kernel(x_ref, g_ref, sc_ref, sh_ref, w_ref, cos_ref, sa_ref, sb_ref, o_ref):
    x = x_ref[...]
    h = _rms(x, g_ref[...]) * (1.0 + sc_ref[...]) + sh_ref[...]
    r = jnp.dot(h.astype(BF16), w_ref[...], preferred_element_type=F32)
    cos, sa, sb = cos_ref[...], sa_ref[...], sb_ref[...]
    for cb in range(2 * DIFF_HEADS):
        t = r[:, cb * HEAD_W:(cb + 1) * HEAD_W]
        rot = t * cos + pltpu.roll(t, HEAD_W - 16, 1) * sa + pltpu.roll(t, 16, 1) * sb
        if cb < DIFF_HEADS:
            rot = rot * (DIFF_DH ** -0.5 * LOG2_E)
        o_ref[:, cb * HEAD_W:(cb + 1) * HEAD_W] = rot.astype(BF16)
    o_ref[:, 2 * DIFF_QK:] = r[:, 2 * DIFF_QK:].astype(BF16)


def _inproj(xin, g, sc, sh, w, cos, sa, sb):
    rows = xin.shape[0]
    mod_spec = pl.BlockSpec((None, None, 1, D_MODEL),
                            lambda i: (i // TILES_PER_SEQ, jnp.where(i % TILES_PER_SEQ == TILES_PER_SEQ - 1, 1, 0), 0, 0))
    rope_spec = pl.BlockSpec((IN_TM, HEAD_W), lambda i: (i % TILES_PER_SEQ, 0))
    return pl.pallas_call(
        _inproj_kernel,
        grid=(rows // IN_TM,),
        in_specs=[
            pl.BlockSpec((IN_TM, D_MODEL), lambda i: (i, 0)),
            pl.BlockSpec((1, D_MODEL), lambda i: (0, 0)),
            mod_spec, mod_spec,
            pl.BlockSpec((D_MODEL, EVEN_IN), lambda i: (0, 0)),
            rope_spec, rope_spec, rope_spec,
        ],
        out_specs=pl.BlockSpec((IN_TM, EVEN_IN), lambda i: (i, 0)),
        out_shape=jax.ShapeDtypeStruct((rows, EVEN_IN), BF16),
        compiler_params=_params("arbitrary"),
        name="inproj",
    )(xin, g, sc, sh, w, cos, sa, sb)


def _rope_tables():
    rows = SEQ // GRID_W
    r = jnp.repeat(jnp.arange(rows, dtype=F32), GRID_W)
    col = jnp.tile(jnp.arange(GRID_W, dtype=F32), rows)
    inv = ROPE_BASE ** (-jnp.arange(ROPE_PAIRS, dtype=F32) / ROPE_PAIRS)
    ar = r[:, None] * inv
    ac = col[:, None] * inv
    ang = jnp.concatenate([ar, ar, ac, ac] * 2, axis=-1)
    cos, sin = jnp.cos(ang), jnp.sin(ang)
    even = ((np.arange(HEAD_W) // ROPE_PAIRS) % 2 == 0)[None, :]
    sa = jnp.where(even, -sin, 0.0)
    sb = jnp.where(even, 0.0, sin)
    ident = jnp.ones((CTX_LEN, HEAD_W), F32)
    zero = jnp.zeros((CTX_LEN, HEAD_W), F32)
    return (jnp.concatenate([cos, ident], 0), jnp.concatenate([sa, zero], 0),
            jnp.concatenate([sb, zero], 0))


ATT_TQ = 1024
ATT_CHAINS = 4
ATT_KC = 256


def _attn_kernel(lam_ref, g_ref, q_ref, k_ref, v_ref, o_ref, vt_ref, st_ref, e_ref):
    @pl.when(pl.program_id(2) == 0)
    def _():
        vt_ref[...] = v_ref[...].astype(F32).T.astype(BF16)

    lane = lax.broadcasted_iota(jnp.int32, (1, HEAD_W), 1)
    lp = lam_ref[...]
    lam = (jnp.exp(jnp.sum(lp[0:1] * lp[1:2], axis=-1, keepdims=True))
           - jnp.exp(jnp.sum(lp[2:3] * lp[3:4], axis=-1, keepdims=True)) + LAM_INIT_L1)
    tq = ATT_TQ // ATT_CHAINS
    n_kc = SEQ_ALL // ATT_KC
    norms = [None] * ATT_CHAINS

    def scores(c):
        q = q_ref[c * tq:(c + 1) * tq, :].astype(F32)
        qq = jnp.concatenate([jnp.where(lane < DIFF_DH, q, 0.0), jnp.where(lane >= DIFF_DH, q, 0.0)], axis=0)
        st_ref[c % 2] = lax.dot_general(k_ref[...], qq.astype(BF16), NT_DIMS, preferred_element_type=F32)

    def softmax(c):
        slot = c % 2
        m = functools.reduce(jnp.maximum, [jnp.max(st_ref[slot, kc * ATT_KC:(kc + 1) * ATT_KC, :], axis=0,
                                                    keepdims=True) for kc in range(n_kc)])
        z = None
        for kc in range(n_kc):
            ks = slice(kc * ATT_KC, (kc + 1) * ATT_KC)
            e = jnp.exp2(st_ref[slot, ks, :] - m)
            zc = jnp.sum(e, axis=0, keepdims=True)
            z = zc if z is None else z + zc
            e_ref[slot, ks, :] = e.astype(BF16)
        norms[c] = z

    def values(c):
        ot = jnp.dot(vt_ref[...], e_ref[c % 2], preferred_element_type=F32) / norms[c]
        o = (ot[:, :tq] - lam * ot[:, tq:]).T
        o_ref[c * tq:(c + 1) * tq, :] = (_rms(o, g_ref[...]) * (1.0 - LAM_INIT_L1)).astype(BF16)

    scores(0)
    for c in range(ATT_CHAINS):
        if c + 1 < ATT_CHAINS:
            scores(c + 1)
        softmax(c)
        if c > 0:
            values(c - 1)
    values(ATT_CHAINS - 1)


def _attn(qkvf, lam_p, head_g):
    nq = SEQ // ATT_TQ
    return pl.pallas_call(
        _attn_kernel,
        grid=(BATCH, DIFF_HEADS, nq),
        in_specs=[
            pl.BlockSpec((4, DIFF_DH), lambda b, h, i: (0, 0)),
            pl.BlockSpec((1, HEAD_W), lambda b, h, i: (0, 0)),
            pl.BlockSpec((None, ATT_TQ, HEAD_W), lambda b, h, i: (b, i, h)),
            pl.BlockSpec((None, SEQ_ALL, HEAD_W), lambda b, h, i: (b, 0, DIFF_HEADS + h)),
            pl.BlockSpec((None, SEQ_ALL, HEAD_W), lambda b, h, i: (b, 0, 2 * DIFF_HEADS + h)),
        ],
        out_specs=pl.BlockSpec((None, ATT_TQ, HEAD_W), lambda b, h, i: (b, i, h)),
        out_shape=jax.ShapeDtypeStruct((BATCH, SEQ, DIFF_QK), BF16),
        scratch_shapes=[pltpu.VMEM((HEAD_W, SEQ_ALL), BF16),
                        pltpu.VMEM((2, SEQ_ALL, 2 * ATT_TQ // ATT_CHAINS), F32),
                        pltpu.VMEM((2, SEQ_ALL, 2 * ATT_TQ // ATT_CHAINS), BF16)],
        compiler_params=_params("arbitrary", "arbitrary", "arbitrary"),
        name="diff_attn",
    )(lam_p, head_g, qkvf, qkvf, qkvf)


def _channel_dft():
    n = np.arange(FOURIER_GC)
    ang = 2.0 * np.pi * np.outer(n, n) / FOURIER_GC
    eye = np.eye(FOURIER_GROUPS)
    scale = FOURIER_GC ** -0.5
    return np.concatenate([np.kron(eye, np.cos(ang)), np.kron(eye, np.sin(ang))], axis=1) * scale


def _seq_dft():
    k = jnp.arange(SEQ, dtype=jnp.int32)[:, None]
    n = jnp.arange(GRID_W, dtype=jnp.int32)[None, :]
    a = ((k * n) % GRID_W).astype(F32) * (2.0 * math.pi / GRID_W)
    b = ((k * n) % SEQ).astype(F32) * (2.0 * math.pi / SEQ)
    ca, sa, cb, sb = jnp.cos(a)[:, :, None], jnp.sin(a)[:, :, None], jnp.cos(b)[:, None, :], jnp.sin(b)[:, None, :]
    scale = SEQ ** -0.5
    cos = ((ca * cb - sa * sb) * scale).reshape(SEQ, SEQ)
    sin = ((sa * cb + ca * sb) * scale).reshape(SEQ, SEQ)
    return jnp.concatenate([cos, -sin], axis=1).astype(BF16)


def _fourier_chan_kernel(f_ref, bd_ref, o_ref):
    f = f_ref[:SEQ, :]
    uv = jnp.dot(f, bd_ref[...], preferred_element_type=F32)
    o_ref[:SEQ, :] = uv[:, :FOURIER_W].astype(BF16)
    o_ref[SEQ:, :] = uv[:, FOURIER_W:].astype(BF16)


def _fourier_seq_kernel(t_ref, uv_ref, o_ref):
    o_ref[...] = jnp.dot(t_ref[...], uv_ref[...], preferred_element_type=F32).astype(BF16)


FOURIER_TM = 512


def _fourier(qkvf, seq_dft):
    bd = jnp.asarray(_channel_dft(), dtype=BF16)
    uv = pl.pallas_call(
        _fourier_chan_kernel,
        grid=(BATCH,),
        in_specs=[
            pl.BlockSpec((None, SEQ_ALL, FOURIER_W), lambda b: (b, 0, 3 * DIFF_QK // FOURIER_W)),
            pl.BlockSpec((FOURIER_W, 2 * FOURIER_W), lambda b: (0, 0)),
        ],
        out_specs=pl.BlockSpec((None, 2 * SEQ, FOURIER_W), lambda b: (b, 0, 0)),
        out_shape=jax.ShapeDtypeStruct((BATCH, 2 * SEQ, FOURIER_W), BF16),
        compiler_params=_params("arbitrary"),
        name="fourier_chan",
    )(qkvf, bd)
    return pl.pallas_call(
        _fourier_seq_kernel,
        grid=(SEQ // FOURIER_TM, BATCH),
        in_specs=[
            pl.BlockSpec((FOURIER_TM, 2 * SEQ), lambda i, b: (i, 0)),
            pl.BlockSpec((None, 2 * SEQ, FOURIER_W), lambda i, b: (b, 0, 0)),
        ],
        out_specs=pl.BlockSpec((None, FOURIER_TM, FOURIER_W), lambda i, b: (b, i, 0)),
        out_shape=jax.ShapeDtypeStruct((BATCH, SEQ, FOURIER_W), BF16),
        compiler_params=_params("arbitrary", "arbitrary"),
        name="fourier_seq",
    )(seq_dft, uv)


OUT_TM = 512


def _outproj_kernel(o_ref, f_ref, w_ref, x_ref, g_ref, out_ref):
    y = jnp.dot(o_ref[...], w_ref[:DIFF_QK, :], preferred_element_type=F32)
    y = y + jnp.dot(f_ref[...], w_ref[DIFF_QK:, :], preferred_element_type=F32)
    out_ref[...] = x_ref[...] + g_ref[...] * y


def _outproj(o, fm, w, x2d, g1):
    rows = x2d.shape[0]
    per_b = SEQ // OUT_TM
    return pl.pallas_call(
        _outproj_kernel,
        grid=(rows // OUT_TM,),
        in_specs=[
            pl.BlockSpec((OUT_TM, DIFF_QK), lambda i: (i, 0)),
            pl.BlockSpec((OUT_TM, FOURIER_W), lambda i: (i, 0)),
            pl.BlockSpec((D_MODEL, D_MODEL), lambda i: (0, 0)),
            pl.BlockSpec((OUT_TM, D_MODEL), lambda i: (i, 0)),
            pl.BlockSpec((None, 1, D_MODEL), lambda i: (i // per_b, 0, 0)),
        ],
        out_specs=pl.BlockSpec((OUT_TM, D_MODEL), lambda i: (i, 0)),
        out_shape=jax.ShapeDtypeStruct((rows, D_MODEL), F32),
        compiler_params=_params("arbitrary"),
        name="outproj",
    )(o, fm, w, x2d, g1)


SGU_TM = 256


def _sgu_kernel(x_ref, g_ref, sc_ref, sh_ref, win_ref, bin_ref, ng_ref, ws_ref, bs_ref, wout_ref, g1_ref,
                out_ref, us_ref):
    x = x_ref[...]
    h = _rms(x, g_ref[...]) * (1.0 + sc_ref[...]) + sh_ref[...]
    z = _gelu(jnp.dot(h.astype(BF16), win_ref[...], preferred_element_type=F32) + bin_ref[...])
    u = z[:, :SGU_W]
    v = _rms(z[:, SGU_W:], ng_ref[...]).astype(BF16)
    for ck in range(SGU_TM // CHUNK):
        rows = slice(ck * CHUNK, (ck + 1) * CHUNK)
        for g in range(SGU_GROUPS):
            cols = slice(g * CHUNK, (g + 1) * CHUNK)
            s = jnp.dot(ws_ref[g], v[rows, cols], preferred_element_type=F32) + bs_ref[g]
            us_ref[rows, cols] = (u[rows, cols] * s).astype(BF16)
    y = jnp.dot(us_ref[...], wout_ref[...], preferred_element_type=F32)
    out_ref[...] = x + g1_ref[...] * y


def _sgu(x2d, g, sc, sh, win, b_in, ng, ws, bs, wout, g1):
    rows = x2d.shape[0]
    per_b = SEQ // SGU_TM
    vec = lambda n: pl.BlockSpec((1, n), lambda i: (0, 0))
    per_batch = pl.BlockSpec((None, 1, D_MODEL), lambda i: (i // per_b, 0, 0))
    return pl.pallas_call(
        _sgu_kernel,
        grid=(rows // SGU_TM,),
        in_specs=[
            pl.BlockSpec((SGU_TM, D_MODEL), lambda i: (i, 0)),
            vec(D_MODEL), per_batch, per_batch,
            pl.BlockSpec((D_MODEL, 2 * SGU_W), lambda i: (0, 0)),
            vec(2 * SGU_W), vec(SGU_W),
            pl.BlockSpec((SGU_GROUPS, CHUNK, CHUNK), lambda i: (0, 0, 0)),
            pl.BlockSpec((SGU_GROUPS, CHUNK, CHUNK), lambda i: (0, 0, 0)),
            pl.BlockSpec((SGU_W, D_MODEL), lambda i: (0, 0)),
            per_batch,
        ],
        out_specs=pl.BlockSpec((SGU_TM, D_MODEL), lambda i: (i, 0)),
        out_shape=jax.ShapeDtypeStruct((rows, D_MODEL), F32),
        scratch_shapes=[pltpu.VMEM((SGU_TM, SGU_W), BF16)],
        compiler_params=_params("arbitrary"),
        name="sgu",
    )(x2d, g, sc, sh, win, b_in, ng, ws, bs, wout, g1)


ROUTE_TM = 256
PAIR_CELLS = [(a, b) for a in range(PEER_TOPK + 1) for b in range(PEER_TOPK + 1)
              if (a + 1) * (b + 1) <= PEER_TOPK + 1]


def _count_above(vals, theta):
    def pivot(conds, lo, step):
        if not conds:
            return vals[lo + step - 1]
        return jnp.where(conds[0], pivot(conds[1:], lo + 2 * step * 2 ** (len(conds) - 1), step),
                         pivot(conds[1:], lo, step))

    conds, count = [], None
    for step in (8, 4, 2, 1):
        hit = pivot(conds, 0, step) > theta
        count = jnp.where(hit, float(step), 0.0) if count is None else jnp.where(hit, count + float(step), count)
        conds.append(hit)
    return jnp.where(vals[-1] > theta, count + 1.0, count)


def _route_kernel(x_ref, g_ref, sc_ref, sh_ref, wq_ref, keys_ref,
                  t_ref, cnt_ref, a_ref, r2_ref, b_ref, s_scr):
    x = x_ref[...]
    tf = _rms(x, g_ref[...]) * (1.0 + sc_ref[...]) + sh_ref[...]
    t_ref[...] = tf.T.astype(BF16)
    q = jnp.dot(tf.astype(BF16), wq_ref[...], preferred_element_type=F32)
    neg = jnp.float32(-jnp.inf)
    tops = []
    for hs in range(2 * PEER_HEADS):
        qh = q[:, hs * PEER_HALF:(hs + 1) * PEER_HALF].astype(BF16)
        st = lax.dot_general(keys_ref[hs], qh, NT_DIMS, preferred_element_type=F32)
        s_scr[hs] = st
        second = hs % 2 == 1
        rank = jnp.full_like(st, float(PEER_TOPK))
        vals = []
        for r in range(PEER_TOPK + 1):
            m = jnp.max(st, axis=0, keepdims=True)
            vals.append(m)
            if r < PEER_TOPK:
                hit = st == m
                st = jnp.where(hit, neg, st)
                if second:
                    rank = jnp.where(hit, float(r), rank)
        if second:
            r2_ref[hs // 2] = rank.astype(BF16)
        tops.append(vals)
    v1 = [jnp.concatenate([tops[2 * h][r] for h in range(PEER_HEADS)], axis=0) for r in range(PEER_TOPK + 1)]
    v2 = [jnp.concatenate([tops[2 * h + 1][r] for h in range(PEER_HEADS)], axis=0) for r in range(PEER_TOPK + 1)]
    cand = [v1[a] + v2[b] for a, b in PAIR_CELLS]
    cur = cand
    kth = None
    for r in range(PEER_TOPK + 1):
        m = functools.reduce(jnp.maximum, cur)
        if r == PEER_TOPK - 1:
            kth = m
        if r < PEER_TOPK:
            cur = [jnp.where(c == m, neg, c) for c in cur]
    thr = 0.5 * (kth + m)
    top = v1[0] + v2[0]
    z = functools.reduce(lambda p, c: p + jnp.where(c >= thr, jnp.exp(c - top), 0.0), cand, jnp.zeros_like(top))
    rz = 1.0 / z
    for h in range(PEER_HEADS):
        s1 = s_scr[2 * h]
        theta = thr[h:h + 1] - s1
        cnt_ref[h] = _count_above([vb[h:h + 1] for vb in v2[:PEER_TOPK]], theta)
        a_ref[h] = jnp.exp(s1 - v1[0][h:h + 1]) * rz[h:h + 1]
        b_ref[h] = jnp.exp(s_scr[2 * h + 1] - v2[0][h:h + 1]).astype(BF16)


def _route(x2d, g, sc, sh, wq, keys):
    rows = x2d.shape[0]
    per_b = SEQ // ROUTE_TM
    per_batch = pl.BlockSpec((None, 1, D_MODEL), lambda i: (i // per_b, 0, 0))
    tbl = pl.BlockSpec((PEER_HEADS, N_KEYS, ROUTE_TM), lambda i: (0, 0, i))
    row_tbl = jax.ShapeDtypeStruct((PEER_HEADS, N_KEYS, rows), F32)
    col_tbl = jax.ShapeDtypeStruct((PEER_HEADS, N_KEYS, rows), BF16)
    return pl.pallas_call(
        _route_kernel,
        grid=(rows // ROUTE_TM,),
        in_specs=[
            pl.BlockSpec((ROUTE_TM, D_MODEL), lambda i: (i, 0)),
            pl.BlockSpec((1, D_MODEL), lambda i: (0, 0)),
            per_batch, per_batch,
            pl.BlockSpec((D_MODEL, 2 * PEER_HEADS * PEER_HALF), lambda i: (0, 0)),
            pl.BlockSpec((2 * PEER_HEADS, N_KEYS, PEER_HALF), lambda i: (0, 0, 0)),
        ],
        out_specs=[pl.BlockSpec((D_MODEL, ROUTE_TM), lambda i: (0, i)), tbl, tbl, tbl, tbl],
        out_shape=[jax.ShapeDtypeStruct((D_MODEL, rows), BF16), row_tbl, row_tbl, col_tbl, col_tbl],
        scratch_shapes=[pltpu.VMEM((2 * PEER_HEADS, N_KEYS, ROUTE_TM), F32)],
        compiler_params=_params("arbitrary"),
        name="peer_route",
    )(x2d, g, sc, sh, wq, keys)


DENSE_TM = 512
DENSE_TE = 2048
DENSE_SUB = 512
DENSE_SHARE = 2
DENSE_JC = 32


def _dense_kernel(t_ref, cnt_ref, a_ref, r2_ref, b_ref, eu_ref, evt_ref, x_ref, g2_ref, fg_ref,
                  out_ref, acc_ref, act_ref, g_ref, rt_ref, *, final_norm):
    e = pl.program_id(1)

    @pl.when(e == 0)
    def _():
        acc_ref[...] = jnp.zeros_like(acc_ref)

    tm = t_ref.shape[1]
    zero = jnp.zeros((), BF16)

    def row_tile(ref, h, i):
        return jnp.broadcast_to(ref[h, pl.ds(i, 1), :], (16, tm)).astype(BF16)

    def tiled(ref, slot, ii, k, h):
        return jnp.concatenate([ref[slot, ii, k, h]] * (DENSE_JC // 16), axis=0)

    rows_per_sub = DENSE_SUB // N_KEYS
    n_sub = DENSE_TE // DENSE_SUB

    def activations(sb, slot):
        start = sb * DENSE_SUB
        if not isinstance(sb, int):
            start = pl.multiple_of(start, DENSE_SUB)
        act_ref[slot] = jnp.dot(eu_ref[pl.ds(start, DENSE_SUB), :], t_ref[...],
                                preferred_element_type=F32)

    def gating(sb, slot):
        i0 = e * (DENSE_TE // N_KEYS) + sb * rows_per_sub
        for ii in range(rows_per_sub):
            for h in range(PEER_HEADS):
                rt_ref[slot, ii, 0, h] = row_tile(cnt_ref, h, i0 + ii)
                rt_ref[slot, ii, 1, h] = row_tile(a_ref, h, i0 + ii)
        for pair in range(0, rows_per_sub, DENSE_SHARE):
            for jc in range(N_KEYS // DENSE_JC):
                js = slice(jc * DENSE_JC, (jc + 1) * DENSE_JC)
                w = [None] * DENSE_SHARE
                for h in range(PEER_HEADS):
                    r2 = r2_ref[h, js, :]
                    bb = b_ref[h, js, :]
                    for s in range(DENSE_SHARE):
                        ii = pair + s
                        term = jnp.where(r2 < tiled(rt_ref, slot, ii, 0, h), bb * tiled(rt_ref, slot, ii, 1, h),
                                         zero)
                        w[s] = term if w[s] is None else w[s] + term
                for s in range(DENSE_SHARE):
                    ii = pair + s
                    rows = slice(ii * N_KEYS + jc * DENSE_JC, ii * N_KEYS + (jc + 1) * DENSE_JC)
                    g_ref[slot, rows, :] = _gelu_packed(act_ref[slot, rows, :].astype(BF16)) * w[s]

    def accumulate(sb, slot):
        acc_ref[...] += jnp.dot(evt_ref[sb], g_ref[slot], preferred_element_type=F32)

    activations(0, 0)
    activations(1, 1)
    gating(0, 0)
    for sb in range(1, n_sub - 1, 2):
        activations(sb + 1, 0)
        gating(sb, 1)
        accumulate(sb - 1, 0)
        activations(sb + 2, 1)
        gating(sb + 1, 0)
        accumulate(sb, 1)
    gating(n_sub - 1, 1)
    accumulate(n_sub - 2, 0)
    accumulate(n_sub - 1, 1)

    @pl.when(e == pl.num_programs(1) - 1)
    def _():
        y = x_ref[...] + g2_ref[...] * acc_ref[...].T
        if final_norm:
            y = _rms(y, fg_ref[...])
        out_ref[...] = y


def _dense(t, cnt, a, r2, b, eu, evt, x2d, g2, fg, layer, final_norm):
    rows = x2d.shape[0]
    per_b = SEQ // DENSE_TM
    tbl = pl.BlockSpec((PEER_HEADS, N_KEYS, DENSE_TM), lambda i, e: (0, 0, i))
    return pl.pallas_call(
        functools.partial(_dense_kernel, final_norm=final_norm),
        grid=(rows // DENSE_TM, N_EXPERTS // DENSE_TE),
        in_specs=[
            pl.BlockSpec((D_MODEL, DENSE_TM), lambda i, e: (0, i)),
            tbl, tbl, tbl, tbl,
            pl.BlockSpec((None, DENSE_TE, D_MODEL), lambda i, e: (layer, e, 0)),
            pl.BlockSpec((None, DENSE_TE // DENSE_SUB, D_MODEL, DENSE_SUB), lambda i, e: (layer, e, 0, 0)),
            pl.BlockSpec((DENSE_TM, D_MODEL), lambda i, e: (i, 0)),
            pl.BlockSpec((None, 1, D_MODEL), lambda i, e: (i // per_b, 0, 0)),
            pl.BlockSpec((1, D_MODEL), lambda i, e: (0, 0)),
        ],
        out_specs=pl.BlockSpec((DENSE_TM, D_MODEL), lambda i, e: (i, 0)),
        out_shape=jax.ShapeDtypeStruct((rows, D_MODEL), F32),
        scratch_shapes=[pltpu.VMEM((D_MODEL, DENSE_TM), F32), pltpu.VMEM((2, DENSE_SUB, DENSE_TM), F32),
                        pltpu.VMEM((2, DENSE_SUB, DENSE_TM), BF16),
                        pltpu.VMEM((2, DENSE_SUB // N_KEYS, 2, PEER_HEADS, 16, DENSE_TM), BF16)],
        compiler_params=_params("arbitrary", "arbitrary"),
        name="peer_dense",
    )(t, cnt, a, r2, b, eu, evt, x2d, g2, fg)


def _peer(x2d, g, sc, sh, g2, wq, keys, eu_all, evt_all, fg, layer, final_norm):
    wq_b = wq.astype(BF16)
    keys_b = keys.reshape(2 * PEER_HEADS, N_KEYS, PEER_HALF).astype(BF16)
    t, cnt, a, r2, b = _route(x2d, g, sc, sh, wq_b, keys_b)
    return _dense(t, cnt, a, r2, b, eu_all, evt_all, x2d, g2, fg, layer, final_norm)


def kernel(x, c, ctx, c_ctx, ada_w, ada_b, norm1_g, norm2_g, final_g, even_w_in, even_w_out, diff_lambda,
           diff_norm_g, odd_w_in, odd_b_in, sgu_norm_g, sgu_w, sgu_b, odd_w_out, peer_wq, peer_keys, peer_u, peer_v):
    cvec = jnp.concatenate([c, c_ctx[None, :], jnp.zeros((8 - BATCH - 1, D_MODEL), F32)], axis=0)
    mod = _ada(cvec, ada_w, ada_b).reshape(2, 8, 6, D_MODEL)
    row = lambda v: v.reshape(1, -1)
    per_batch = lambda l, j: mod[l, :BATCH, j].reshape(BATCH, 1, D_MODEL)
    fg = row(final_g)

    both = lambda j: jnp.stack([mod[0, :BATCH, j], jnp.broadcast_to(mod[0, BATCH, j], (BATCH, D_MODEL))],
                               axis=1).reshape(BATCH, 2, 1, D_MODEL)
    xin = jnp.concatenate([x, ctx], axis=1).reshape(BATCH * SEQ_ALL, D_MODEL)
    cos, sa, sb = _rope_tables()
    qkvf = _inproj(xin, row(norm1_g[0]), both(1), both(0), even_w_in[0].astype(BF16), cos, sa, sb)
    qkvf = qkvf.reshape(BATCH, SEQ_ALL, EVEN_IN)
    o = _attn(qkvf, diff_lambda[0], row(diff_norm_g[0]))
    fm = _fourier(qkvf, _seq_dft())
    eu_all = peer_u.astype(BF16)
    evt_all = peer_v.astype(BF16).reshape(2, N_EXPERTS // DENSE_SUB, DENSE_SUB, D_MODEL).transpose(0, 1, 3, 2)
    x2d = x.reshape(BATCH * SEQ, D_MODEL)
    x2d = _outproj(o.reshape(BATCH * SEQ, DIFF_QK), fm.reshape(BATCH * SEQ, FOURIER_W),
                   even_w_out[0].astype(BF16), x2d, per_batch(0, 2))
    x2d = _peer(x2d, row(norm2_g[0]), per_batch(0, 4), per_batch(0, 3), per_batch(0, 5),
                peer_wq[0], peer_keys[0], eu_all, evt_all, fg, 0, False)

    bs = jnp.broadcast_to(sgu_b[0][:, :, None], (SGU_GROUPS, CHUNK, CHUNK))
    x2d = _sgu(x2d, row(norm1_g[1]), per_batch(1, 1), per_batch(1, 0), odd_w_in[0].astype(BF16),
               row(odd_b_in[0]), row(sgu_norm_g[0]), sgu_w[0].astype(BF16), bs, odd_w_out[0].astype(BF16),
               per_batch(1, 2))
    x2d = _peer(x2d, row(norm2_g[1]), per_batch(1, 4), per_batch(1, 3), per_batch(1, 5),
                peer_wq[1], peer_keys[1], eu_all, evt_all, fg, 1, True)
    return x2d.reshape(BATCH, SEQ, D_MODEL)
```

```python
import functools
import math

import numpy as np
import jax
import jax.numpy as jnp
from jax import lax
from jax.experimental import pallas as pl
from jax.experimental.pallas import tpu as pltpu

F32 = jnp.float32
BF16 = jnp.bfloat16

D_MODEL = 1024
BATCH = 4
SEQ = 4096
GRID_W = 64
CTX_LEN = 256
SEQ_ALL = CTX_LEN + SEQ
EPS = 1e-6

DIFF_HEADS = 6
DIFF_DH = 64
HEAD_W = 2 * DIFF_DH
DIFF_QK = DIFF_HEADS * HEAD_W
FOURIER_GROUPS = 4
FOURIER_GC = 64
FOURIER_W = FOURIER_GROUPS * FOURIER_GC
EVEN_IN = 3 * DIFF_QK + FOURIER_W
ROPE_BASE = 10000.0
ROPE_PAIRS = DIFF_DH // 4
LAM_INIT_L1 = 0.8 - 0.6 * math.exp(-0.3 * 0.0)
LOG2_E = math.log2(math.e)

CHUNK = 128
SGU_GROUPS = 8
SGU_W = SGU_GROUPS * CHUNK

PEER_HEADS = 8
N_KEYS = 128
N_EXPERTS = N_KEYS * N_KEYS
PEER_TOPK = 16
PEER_HALF = 128
LANES = 128

VMEM_LIMIT = 56 * 1024 * 1024

NT_DIMS = (((1,), (1,)), ((), ()))


def _rms(t, g):
    return t * lax.rsqrt(jnp.mean(t * t, axis=-1, keepdims=True) + EPS) * g


def _gelu(x):
    cdf = 0.5 * (1.0 + jnp.tanh(math.sqrt(2.0 / math.pi) * (x + 0.044715 * (x * x * x))))
    return x * cdf


def _gelu_packed(x):
    c = math.sqrt(2.0 / math.pi)
    half = 0.5 * x
    return half + half * jnp.tanh(x * (c + (c * 0.044715) * (x * x)))


def _params(*sem, flags=None):
    return pltpu.CompilerParams(dimension_semantics=sem, vmem_limit_bytes=VMEM_LIMIT, flags=flags)


def _ada_kernel(c_ref, w_ref, b_ref, o_ref):
    c = c_ref[...]
    s = c * jax.nn.sigmoid(c)
    o_ref[...] = jnp.dot(s, w_ref[...], preferred_element_type=F32) + b_ref[...]


def _ada(cvec, ada_w, ada_b):
    depth = ada_w.shape[0]
    tn = 1536
    return pl.pallas_call(
        _ada_kernel,
        grid=(depth, 6 * D_MODEL // tn),
        in_specs=[
            pl.BlockSpec((8, D_MODEL), lambda l, j: (0, 0)),
            pl.BlockSpec((None, D_MODEL, tn), lambda l, j: (l, 0, j)),
            pl.BlockSpec((None, 1, tn), lambda l, j: (l, 0, j)),
        ],
        out_specs=pl.BlockSpec((None, 8, tn), lambda l, j: (l, 0, j)),
        out_shape=jax.ShapeDtypeStruct((depth, 8, 6 * D_MODEL), F32),
        compiler_params=_params("arbitrary", "arbitrary"),
        name="ada",
    )(cvec, ada_w, ada_b.reshape(depth, 1, 6 * D_MODEL))


IN_TM = 256
TILES_PER_SEQ = SEQ_ALL // IN_TM


def _inproj_kernel(x_ref, g_ref, sc_ref, sh_ref, w_ref, cos_ref, sa_ref, sb_ref, o_ref):
    x = x_ref[...]
    h = _rms(x, g_ref[...]) * (1.0 + sc_ref[...]) + sh_ref[...]
    r = jnp.dot(h.astype(BF16), w_ref[...], preferred_element_type=F32)
    cos, sa, sb = cos_ref[...], sa_ref[...], sb_ref[...]
    for cb in range(2 * DIFF_HEADS):
        t = r[:, cb * HEAD_W:(cb + 1) * HEAD_W]
        rot = t * cos + pltpu.roll(t, HEAD_W - 16, 1) * sa + pltpu.roll(t, 16, 1) * sb
        if cb < DIFF_HEADS:
            rot = rot * (DIFF_DH ** -0.5 * LOG2_E)
        o_ref[:, cb * HEAD_W:(cb + 1) * HEAD_W] = rot.astype(BF16)
    o_ref[:, 2 * DIFF_QK:] = r[:, 2 * DIFF_QK:].astype(BF16)


def _inproj(xin, g, sc, sh, w, cos, sa, sb):
    rows = xin.shape[0]
    mod_spec = pl.BlockSpec((None, None, 1, D_MODEL),
                            lambda i: (i // TILES_PER_SEQ, jnp.where(i % TILES_PER_SEQ == TILES_PER_SEQ - 1, 1, 0), 0, 0))
    rope_spec = pl.BlockSpec((IN_TM, HEAD_W), lambda i: (i % TILES_PER_SEQ, 0))
    return pl.pallas_call(
        _inproj_kernel,
        grid=(rows // IN_TM,),
        in_specs=[
            pl.BlockSpec((IN_TM, D_MODEL), lambda i: (i, 0)),
            pl.BlockSpec((1, D_MODEL), lambda i: (0, 0)),
            mod_spec, mod_spec,
            pl.BlockSpec((D_MODEL, EVEN_IN), lambda i: (0, 0)),
            rope_spec, rope_spec, rope_spec,
        ],
        out_specs=pl.BlockSpec((IN_TM, EVEN_IN), lambda i: (i, 0)),
        out_shape=jax.ShapeDtypeStruct((rows, EVEN_IN), BF16),
        compiler_params=_params("arbitrary"),
        name="inproj",
    )(xin, g, sc, sh, w, cos, sa, sb)


def _rope_tables():
    rows = SEQ // GRID_W
    r = jnp.repeat(jnp.arange(rows, dtype=F32), GRID_W)
    col = jnp.tile(jnp.arange(GRID_W, dtype=F32), rows)
    inv = ROPE_BASE ** (-jnp.arange(ROPE_PAIRS, dtype=F32) / ROPE_PAIRS)
    ar = r[:, None] * inv
    ac = col[:, None] * inv
    ang = jnp.concatenate([ar, ar, ac, ac] * 2, axis=-1)
    cos, sin = jnp.cos(ang), jnp.sin(ang)
    even = ((np.arange(HEAD_W) // ROPE_PAIRS) % 2 == 0)[None, :]
    sa = jnp.where(even, -sin, 0.0)
    sb = jnp.where(even, 0.0, sin)
    ident = jnp.ones((CTX_LEN, HEAD_W), F32)
    zero = jnp.zeros((CTX_LEN, HEAD_W), F32)
    return (jnp.concatenate([cos, ident], 0), jnp.concatenate([sa, zero], 0),
            jnp.concatenate([sb, zero], 0))


ATT_TQ = 1024
ATT_CHAINS = 4
ATT_KC = 256


def _attn_kernel(lam_ref, g_ref, q_ref, k_ref, v_ref, o_ref, vt_ref, st_ref, e_ref):
    @pl.when(pl.program_id(2) == 0)
    def _():
        vt_ref[:HEAD_W, :] = v_ref[...].astype(F32).T.astype(BF16)
        ones_row = lax.broadcasted_iota(jnp.int32, (16, SEQ_ALL), 0) == 0
        vt_ref[HEAD_W:, :] = jnp.where(ones_row, 1.0, 0.0).astype(BF16)

    lane = lax.broadcasted_iota(jnp.int32, (1, HEAD_W), 1)
    lp = lam_ref[...]
    lam = (jnp.exp(jnp.sum(lp[0:1] * lp[1:2], axis=-1, keepdims=True))
           - jnp.exp(jnp.sum(lp[2:3] * lp[3:4], axis=-1, keepdims=True)) + LAM_INIT_L1)
    tq = ATT_TQ // ATT_CHAINS
    n_kc = SEQ_ALL // ATT_KC

    def scores(c):
        q = q_ref[c * tq:(c + 1) * tq, :].astype(F32)
        qq = jnp.concatenate([jnp.where(lane < DIFF_DH, q, 0.0), jnp.where(lane >= DIFF_DH, q, 0.0)], axis=0)
        st_ref[c % 2] = lax.dot_general(k_ref[...], qq.astype(BF16), NT_DIMS, preferred_element_type=F32)

    def softmax(c):
        slot = c % 2
        m = functools.reduce(jnp.maximum, [jnp.max(st_ref[slot, kc * ATT_KC:(kc + 1) * ATT_KC, :], axis=0,
                                                    keepdims=True) for kc in range(n_kc)])
        for kc in range(n_kc):
            ks = slice(kc * ATT_KC, (kc + 1) * ATT_KC)
            e_ref[slot, ks, :] = jnp.exp2(st_ref[slot, ks, :] - m).astype(BF16)

    def values(c):
        ext = jnp.dot(vt_ref[...], e_ref[c % 2], preferred_element_type=F32)
        ot = ext[:HEAD_W] / ext[HEAD_W:HEAD_W + 1]
        o = (ot[:, :tq] - lam * ot[:, tq:]).T
        o_ref[c * tq:(c + 1) * tq, :] = (_rms(o, g_ref[...]) * (1.0 - LAM_INIT_L1)).astype(BF16)

    scores(0)
    for c in range(ATT_CHAINS):
        if c + 1 < ATT_CHAINS:
            scores(c + 1)
        softmax(c)
        if c > 0:
            values(c - 1)
    values(ATT_CHAINS - 1)


def _attn(qkvf, lam_p, head_g):
    nq = SEQ // ATT_TQ
    return pl.pallas_call(
        _attn_kernel,
        grid=(BATCH, DIFF_HEADS, nq),
        in_specs=[
            pl.BlockSpec((4, DIFF_DH), lambda b, h, i: (0, 0)),
            pl.BlockSpec((1, HEAD_W), lambda b, h, i: (0, 0)),
            pl.BlockSpec((None, ATT_TQ, HEAD_W), lambda b, h, i: (b, i, h)),
            pl.BlockSpec((None, SEQ_ALL, HEAD_W), lambda b, h, i: (b, 0, DIFF_HEADS + h)),
            pl.BlockSpec((None, SEQ_ALL, HEAD_W), lambda b, h, i: (b, 0, 2 * DIFF_HEADS + h)),
        ],
        out_specs=pl.BlockSpec((None, ATT_TQ, HEAD_W), lambda b, h, i: (b, i, h)),
        out_shape=jax.ShapeDtypeStruct((BATCH, SEQ, DIFF_QK), BF16),
        scratch_shapes=[pltpu.VMEM((HEAD_W + 16, SEQ_ALL), BF16),
                        pltpu.VMEM((2, SEQ_ALL, 2 * ATT_TQ // ATT_CHAINS), F32),
                        pltpu.VMEM((2, SEQ_ALL, 2 * ATT_TQ // ATT_CHAINS), BF16)],
        compiler_params=_params("arbitrary", "arbitrary", "arbitrary"),
        name="diff_attn",
    )(lam_p, head_g, qkvf, qkvf, qkvf)


def _channel_dft():
    n = np.arange(FOURIER_GC)
    ang = 2.0 * np.pi * np.outer(n, n) / FOURIER_GC
    eye = np.eye(FOURIER_GROUPS)
    scale = FOURIER_GC ** -0.5
    return np.concatenate([np.kron(eye, np.cos(ang)), np.kron(eye, np.sin(ang))], axis=1) * scale


def _seq_dft():
    k = jnp.arange(SEQ, dtype=jnp.int32)[:, None]
    n = jnp.arange(GRID_W, dtype=jnp.int32)[None, :]
    a = ((k * n) % GRID_W).astype(F32) * (2.0 * math.pi / GRID_W)
    b = ((k * n) % SEQ).astype(F32) * (2.0 * math.pi / SEQ)
    ca, sa, cb, sb = jnp.cos(a)[:, :, None], jnp.sin(a)[:, :, None], jnp.cos(b)[:, None, :], jnp.sin(b)[:, None, :]
    scale = SEQ ** -0.5
    cos = ((ca * cb - sa * sb) * scale).reshape(SEQ, SEQ)
    sin = ((sa * cb + ca * sb) * scale).reshape(SEQ, SEQ)
    return jnp.concatenate([cos, -sin], axis=1).astype(BF16)


def _fourier_chan_kernel(f_ref, bd_ref, o_ref):
    f = f_ref[:SEQ, :]
    uv = jnp.dot(f, bd_ref[...], preferred_element_type=F32)
    o_ref[:SEQ, :] = uv[:, :FOURIER_W].astype(BF16)
    o_ref[SEQ:, :] = uv[:, FOURIER_W:].astype(BF16)


def _fourier_seq_kernel(t_ref, uv_ref, o_ref):
    o_ref[...] = jnp.dot(t_ref[...], uv_ref[...], preferred_element_type=F32).astype(BF16)


FOURIER_TM = 512


def _fourier(qkvf, seq_dft):
    bd = jnp.asarray(_channel_dft(), dtype=BF16)
    uv = pl.pallas_call(
        _fourier_chan_kernel,
        grid=(BATCH,),
        in_specs=[
            pl.BlockSpec((None, SEQ_ALL, FOURIER_W), lambda b: (b, 0, 3 * DIFF_QK // FOURIER_W)),
            pl.BlockSpec((FOURIER_W, 2 * FOURIER_W), lambda b: (0, 0)),
        ],
        out_specs=pl.BlockSpec((None, 2 * SEQ, FOURIER_W), lambda b: (b, 0, 0)),
        out_shape=jax.ShapeDtypeStruct((BATCH, 2 * SEQ, FOURIER_W), BF16),
        compiler_params=_params("arbitrary"),
        name="fourier_chan",
    )(qkvf, bd)
    return pl.pallas_call(
        _fourier_seq_kernel,
        grid=(SEQ // FOURIER_TM, BATCH),
        in_specs=[
            pl.BlockSpec((FOURIER_TM, 2 * SEQ), lambda i, b: (i, 0)),
            pl.BlockSpec((None, 2 * SEQ, FOURIER_W), lambda i, b: (b, 0, 0)),
        ],
        out_specs=pl.BlockSpec((None, FOURIER_TM, FOURIER_W), lambda i, b: (b, i, 0)),
        out_shape=jax.ShapeDtypeStruct((BATCH, SEQ, FOURIER_W), BF16),
        compiler_params=_params("arbitrary", "arbitrary"),
        name="fourier_seq",
    )(seq_dft, uv)


OUT_TM = 512


def _outproj_kernel(o_ref, f_ref, w_ref, x_ref, g_ref, out_ref):
    y = jnp.dot(o_ref[...], w_ref[:DIFF_QK, :], preferred_element_type=F32)
    y = y + jnp.dot(f_ref[...], w_ref[DIFF_QK:, :], preferred_element_type=F32)
    out_ref[...] = x_ref[...] + g_ref[...] * y


def _outproj(o, fm, w, x2d, g1):
    rows = x2d.shape[0]
    per_b = SEQ // OUT_TM
    return pl.pallas_call(
        _outproj_kernel,
        grid=(rows // OUT_TM,),
        in_specs=[
            pl.BlockSpec((OUT_TM, DIFF_QK), lambda i: (i, 0)),
            pl.BlockSpec((OUT_TM, FOURIER_W), lambda i: (i, 0)),
            pl.BlockSpec((D_MODEL, D_MODEL), lambda i: (0, 0)),
            pl.BlockSpec((OUT_TM, D_MODEL), lambda i: (i, 0)),
            pl.BlockSpec((None, 1, D_MODEL), lambda i: (i // per_b, 0, 0)),
        ],
        out_specs=pl.BlockSpec((OUT_TM, D_MODEL), lambda i: (i, 0)),
        out_shape=jax.ShapeDtypeStruct((rows, D_MODEL), F32),
        compiler_params=_params("arbitrary"),
        name="outproj",
    )(o, fm, w, x2d, g1)


SGU_TM = 256


def _sgu_kernel(x_ref, g_ref, sc_ref, sh_ref, win_ref, bin_ref, ng_ref, ws_ref, bs_ref, wout_ref, g1_ref,
                out_ref, us_ref):
    x = x_ref[...]
    h = _rms(x, g_ref[...]) * (1.0 + sc_ref[...]) + sh_ref[...]
    z = _gelu(jnp.dot(h.astype(BF16), win_ref[...], preferred_element_type=F32) + bin_ref[...])
    u = z[:, :SGU_W]
    v = _rms(z[:, SGU_W:], ng_ref[...]).astype(BF16)
    for ck in range(SGU_TM // CHUNK):
        rows = slice(ck * CHUNK, (ck + 1) * CHUNK)
        for g in range(SGU_GROUPS):
            cols = slice(g * CHUNK, (g + 1) * CHUNK)
            s = jnp.dot(ws_ref[g], v[rows, cols], preferred_element_type=F32) + bs_ref[g]
            us_ref[rows, cols] = (u[rows, cols] * s).astype(BF16)
    y = jnp.dot(us_ref[...], wout_ref[...], preferred_element_type=F32)
    out_ref[...] = x + g1_ref[...] * y


def _sgu(x2d, g, sc, sh, win, b_in, ng, ws, bs, wout, g1):
    rows = x2d.shape[0]
    per_b = SEQ // SGU_TM
    vec = lambda n: pl.BlockSpec((1, n), lambda i: (0, 0))
    per_batch = pl.BlockSpec((None, 1, D_MODEL), lambda i: (i // per_b, 0, 0))
    return pl.pallas_call(
        _sgu_kernel,
        grid=(rows // SGU_TM,),
        in_specs=[
            pl.BlockSpec((SGU_TM, D_MODEL), lambda i: (i, 0)),
            vec(D_MODEL), per_batch, per_batch,
            pl.BlockSpec((D_MODEL, 2 * SGU_W), lambda i: (0, 0)),
            vec(2 * SGU_W), vec(SGU_W),
            pl.BlockSpec((SGU_GROUPS, CHUNK, CHUNK), lambda i: (0, 0, 0)),
            pl.BlockSpec((SGU_GROUPS, CHUNK, CHUNK), lambda i: (0, 0, 0)),
            pl.BlockSpec((SGU_W, D_MODEL), lambda i: (0, 0)),
            per_batch,
        ],
        out_specs=pl.BlockSpec((SGU_TM, D_MODEL), lambda i: (i, 0)),
        out_shape=jax.ShapeDtypeStruct((rows, D_MODEL), F32),
        scratch_shapes=[pltpu.VMEM((SGU_TM, SGU_W), BF16)],
        compiler_params=_params("arbitrary"),
        name="sgu",
    )(x2d, g, sc, sh, win, b_in, ng, ws, bs, wout, g1)


ROUTE_TM = 256
PAIR_CELLS = [(a, b) for a in range(PEER_TOPK + 1) for b in range(PEER_TOPK + 1)
              if (a + 1) * (b + 1) <= PEER_TOPK + 1]


def _sort16_network():
    pairs = []

    def merge(lo, n, r):
        step = 2 * r
        if step < n:
            merge(lo, n, step)
            merge(lo + r, n, step)
            pairs.extend((i, i + r) for i in range(lo + r, lo + n - r, step))
        else:
            pairs.append((lo, lo + r))

    def sort(lo, n):
        if n > 1:
            sort(lo, n // 2)
            sort(lo + n // 2, n // 2)
            merge(lo, n, 1)

    sort(0, 16)
    return pairs


SORT16 = _sort16_network()


def _top_sorted(st):
    sub = 8
    groups = [st[g * sub:(g + 1) * sub, :] for g in range(N_KEYS // sub)]
    for i, j in SORT16:
        groups[i], groups[j] = jnp.maximum(groups[i], groups[j]), jnp.minimum(groups[i], groups[j])
    floor = jnp.full_like(groups[0], -jnp.inf)
    vals = []
    for r in range(PEER_TOPK + 1):
        m = jnp.max(groups[0], axis=0, keepdims=True)
        vals.append(m)
        if r < PEER_TOPK:
            pop = groups[0] == m
            for d in range(PEER_TOPK - r):
                below = groups[d + 1] if d + 1 < len(groups) else floor
                groups[d] = jnp.where(pop, below, groups[d])
    return vals


def _count_above(vals, theta):
    def pivot(conds, lo, step):
        if not conds:
            return vals[lo + step - 1]
        return jnp.where(conds[0], pivot(conds[1:], lo + 2 * step * 2 ** (len(conds) - 1), step),
                         pivot(conds[1:], lo, step))

    conds, count = [], None
    for step in (8, 4, 2, 1):
        hit = pivot(conds, 0, step) > theta
        count = jnp.where(hit, float(step), 0.0) if count is None else jnp.where(hit, count + float(step), count)
        conds.append(hit)
    return jnp.where(vals[-1] > theta, count + 1.0, count)


def _route_kernel(x_ref, g_ref, sc_ref, sh_ref, wq_ref, keys_ref,
                  t_ref, cnt_ref, a_ref, r2_ref, b_ref, s_scr):
    x = x_ref[...]
    tf = _rms(x, g_ref[...]) * (1.0 + sc_ref[...]) + sh_ref[...]
    t_ref[...] = tf.T.astype(BF16)
    q = jnp.dot(tf.astype(BF16), wq_ref[...], preferred_element_type=F32)
    neg = jnp.float32(-jnp.inf)
    tops = []
    for hs in range(2 * PEER_HEADS):
        qh = q[:, hs * PEER_HALF:(hs + 1) * PEER_HALF].astype(BF16)
        st = lax.dot_general(keys_ref[hs], qh, NT_DIMS, preferred_element_type=F32)
        s_scr[hs] = st
        vals = _top_sorted(st)
        if hs % 2 == 1:
            r2_ref[hs // 2] = _count_above(vals[:PEER_TOPK], st).astype(BF16)
        tops.append(vals)
    v1 = [jnp.concatenate([tops[2 * h][r] for h in range(PEER_HEADS)], axis=0) for r in range(PEER_TOPK + 1)]
    v2 = [jnp.concatenate([tops[2 * h + 1][r] for h in range(PEER_HEADS)], axis=0) for r in range(PEER_TOPK + 1)]
    cand = [v1[a] + v2[b] for a, b in PAIR_CELLS]
    cur = cand
    kth = None
    for r in range(PEER_TOPK + 1):
        m = functools.reduce(jnp.maximum, cur)
        if r == PEER_TOPK - 1:
            kth = m
        if r < PEER_TOPK:
            cur = [jnp.where(c == m, neg, c) for c in cur]
    thr = 0.5 * (kth + m)
    top = v1[0] + v2[0]
    z = functools.reduce(lambda p, c: p + jnp.where(c >= thr, jnp.exp(c - top), 0.0), cand, jnp.zeros_like(top))
    rz = 1.0 / z
    for h in range(PEER_HEADS):
        s1 = s_scr[2 * h]
        theta = thr[h:h + 1] - s1
        cnt_ref[h] = _count_above([vb[h:h + 1] for vb in v2[:PEER_TOPK]], theta)
        a_ref[h] = jnp.exp(s1 - v1[0][h:h + 1]) * rz[h:h + 1]
        b_ref[h] = jnp.exp(s_scr[2 * h + 1] - v2[0][h:h + 1]).astype(BF16)


def _route(x2d, g, sc, sh, wq, keys):
    rows = x2d.shape[0]
    per_b = SEQ // ROUTE_TM
    per_batch = pl.BlockSpec((None, 1, D_MODEL), lambda i: (i // per_b, 0, 0))
    tbl = pl.BlockSpec((PEER_HEADS, N_KEYS, ROUTE_TM), lambda i: (0, 0, i))
    row_tbl = jax.ShapeDtypeStruct((PEER_HEADS, N_KEYS, rows), F32)
    col_tbl = jax.ShapeDtypeStruct((PEER_HEADS, N_KEYS, rows), BF16)
    return pl.pallas_call(
        _route_kernel,
        grid=(rows // ROUTE_TM,),
        in_specs=[
            pl.BlockSpec((ROUTE_TM, D_MODEL), lambda i: (i, 0)),
            pl.BlockSpec((1, D_MODEL), lambda i: (0, 0)),
            per_batch, per_batch,
            pl.BlockSpec((D_MODEL, 2 * PEER_HEADS * PEER_HALF), lambda i: (0, 0)),
            pl.BlockSpec((2 * PEER_HEADS, N_KEYS, PEER_HALF), lambda i: (0, 0, 0)),
        ],
        out_specs=[pl.BlockSpec((D_MODEL, ROUTE_TM), lambda i: (0, i)), tbl, tbl, tbl, tbl],
        out_shape=[jax.ShapeDtypeStruct((D_MODEL, rows), BF16), row_tbl, row_tbl, col_tbl, col_tbl],
        scratch_shapes=[pltpu.VMEM((2 * PEER_HEADS, N_KEYS, ROUTE_TM), F32)],
        compiler_params=_params("arbitrary"),
        name="peer_route",
    )(x2d, g, sc, sh, wq, keys)


DENSE_TM = 512
DENSE_TE = 2048
DENSE_SUB = 512
DENSE_SHARE = 2
DENSE_JC = 32


def _dense_kernel(t_ref, cnt_ref, a_ref, r2_ref, b_ref, eu_ref, evt_ref, x_ref, g2_ref, fg_ref,
                  out_ref, acc_ref, act_ref, g_ref, rt_ref, *, final_norm):
    e = pl.program_id(1)

    @pl.when(e == 0)
    def _():
        acc_ref[...] = jnp.zeros_like(acc_ref)

    tm = t_ref.shape[1]
    zero = jnp.zeros((), BF16)

    def row_tile(ref, h, i):
        return jnp.broadcast_to(ref[h, pl.ds(i, 1), :], (16, tm)).astype(BF16)

    def tiled(ref, slot, ii, k, h):
        return jnp.concatenate([ref[slot, ii, k, h]] * (DENSE_JC // 16), axis=0)

    rows_per_sub = DENSE_SUB // N_KEYS
    n_sub = DENSE_TE // DENSE_SUB

    def activations(sb, slot):
        start = sb * DENSE_SUB
        if not isinstance(sb, int):
            start = pl.multiple_of(start, DENSE_SUB)
        act_ref[slot] = jnp.dot(eu_ref[pl.ds(start, DENSE_SUB), :], t_ref[...],
                                preferred_element_type=F32)

    def gating(sb, slot):
        i0 = e * (DENSE_TE // N_KEYS) + sb * rows_per_sub
        for ii in range(rows_per_sub):
            for h in range(PEER_HEADS):
                rt_ref[slot, ii, 0, h] = row_tile(cnt_ref, h, i0 + ii)
                rt_ref[slot, ii, 1, h] = row_tile(a_ref, h, i0 + ii)
        for pair in range(0, rows_per_sub, DENSE_SHARE):
            for jc in range(N_KEYS // DENSE_JC):
                js = slice(jc * DENSE_JC, (jc + 1) * DENSE_JC)
                w = [None] * DENSE_SHARE
                for h in range(PEER_HEADS):
                    r2 = r2_ref[h, js, :]
                    bb = b_ref[h, js, :]
                    for s in range(DENSE_SHARE):
                        ii = pair + s
                        term = jnp.where(r2 < tiled(rt_ref, slot, ii, 0, h), bb * tiled(rt_ref, slot, ii, 1, h),
                                         zero)
                        w[s] = term if w[s] is None else w[s] + term
                for s in range(DENSE_SHARE):
                    ii = pair + s
                    rows = slice(ii * N_KEYS + jc * DENSE_JC, ii * N_KEYS + (jc + 1) * DENSE_JC)
                    g_ref[slot, rows, :] = _gelu_packed(act_ref[slot, rows, :].astype(BF16)) * w[s]

    def accumulate(sb, slot):
        acc_ref[...] += jnp.dot(evt_ref[sb], g_ref[slot], preferred_element_type=F32)

    activations(0, 0)
    activations(1, 1)
    gating(0, 0)
    for sb in range(1, n_sub - 1, 2):
        activations(sb + 1, 0)
        gating(sb, 1)
        accumulate(sb - 1, 0)
        activations(sb + 2, 1)
        gating(sb + 1, 0)
        accumulate(sb, 1)
    gating(n_sub - 1, 1)
    accumulate(n_sub - 2, 0)
    accumulate(n_sub - 1, 1)

    @pl.when(e == pl.num_programs(1) - 1)
    def _():
        y = x_ref[...] + g2_ref[...] * acc_ref[...].T
        if final_norm:
            y = _rms(y, fg_ref[...])
        out_ref[...] = y


def _dense(t, cnt, a, r2, b, eu, evt, x2d, g2, fg, layer, final_norm):
    rows = x2d.shape[0]
    per_b = SEQ // DENSE_TM
    tbl = pl.BlockSpec((PEER_HEADS, N_KEYS, DENSE_TM), lambda i, e: (0, 0, i))
    return pl.pallas_call(
        functools.partial(_dense_kernel, final_norm=final_norm),
        grid=(rows // DENSE_TM, N_EXPERTS // DENSE_TE),
        in_specs=[
            pl.BlockSpec((D_MODEL, DENSE_TM), lambda i, e: (0, i)),
            tbl, tbl, tbl, tbl,
            pl.BlockSpec((None, DENSE_TE, D_MODEL), lambda i, e: (layer, e, 0)),
            pl.BlockSpec((None, DENSE_TE // DENSE_SUB, D_MODEL, DENSE_SUB), lambda i, e: (layer, e, 0, 0)),
            pl.BlockSpec((DENSE_TM, D_MODEL), lambda i, e: (i, 0)),
            pl.BlockSpec((None, 1, D_MODEL), lambda i, e: (i // per_b, 0, 0)),
            pl.BlockSpec((1, D_MODEL), lambda i, e: (0, 0)),
        ],
        out_specs=pl.BlockSpec((DENSE_TM, D_MODEL), lambda i, e: (i, 0)),
        out_shape=jax.ShapeDtypeStruct((rows, D_MODEL), F32),
        scratch_shapes=[pltpu.VMEM((D_MODEL, DENSE_TM), F32), pltpu.VMEM((2, DENSE_SUB, DENSE_TM), F32),
                        pltpu.VMEM((2, DENSE_SUB, DENSE_TM), BF16),
                        pltpu.VMEM((2, DENSE_SUB // N_KEYS, 2, PEER_HEADS, 16, DENSE_TM), BF16)],
        compiler_params=_params("arbitrary", "arbitrary"),
        name="peer_dense",
    )(t, cnt, a, r2, b, eu, evt, x2d, g2, fg)


def _peer(x2d, g, sc, sh, g2, wq, keys, eu_all, evt_all, fg, layer, final_norm):
    wq_b = wq.astype(BF16)
    keys_b = keys.reshape(2 * PEER_HEADS, N_KEYS, PEER_HALF).astype(BF16)
    t, cnt, a, r2, b = _route(x2d, g, sc, sh, wq_b, keys_b)
    return _dense(t, cnt, a, r2, b, eu_all, evt_all, x2d, g2, fg, layer, final_norm)


def kernel(x, c, ctx, c_ctx, ada_w, ada_b, norm1_g, norm2_g, final_g, even_w_in, even_w_out, diff_lambda,
           diff_norm_g, odd_w_in, odd_b_in, sgu_norm_g, sgu_w, sgu_b, odd_w_out, peer_wq, peer_keys, peer_u, peer_v):
    cvec = jnp.concatenate([c, c_ctx[None, :], jnp.zeros((8 - BATCH - 1, D_MODEL), F32)], axis=0)
    mod = _ada(cvec, ada_w, ada_b).reshape(2, 8, 6, D_MODEL)
    row = lambda v: v.reshape(1, -1)
    per_batch = lambda l, j: mod[l, :BATCH, j].reshape(BATCH, 1, D_MODEL)
    fg = row(final_g)

    both = lambda j: jnp.stack([mod[0, :BATCH, j], jnp.broadcast_to(mod[0, BATCH, j], (BATCH, D_MODEL))],
                               axis=1).reshape(BATCH, 2, 1, D_MODEL)
    xin = jnp.concatenate([x, ctx], axis=1).reshape(BATCH * SEQ_ALL, D_MODEL)
    cos, sa, sb = _rope_tables()
    qkvf = _inproj(xin, row(norm1_g[0]), both(1), both(0), even_w_in[0].astype(BF16), cos, sa, sb)
    qkvf = qkvf.reshape(BATCH, SEQ_ALL, EVEN_IN)
    o = _attn(qkvf, diff_lambda[0], row(diff_norm_g[0]))
    fm = _fourier(qkvf, _seq_dft())
    eu_all = peer_u.astype(BF16)
    evt_all = peer_v.astype(BF16).reshape(2, N_EXPERTS // DENSE_SUB, DENSE_SUB, D_MODEL).transpose(0, 1, 3, 2)
    x2d = x.reshape(BATCH * SEQ, D_MODEL)
    x2d = _outproj(o.reshape(BATCH * SEQ, DIFF_QK), fm.reshape(BATCH * SEQ, FOURIER_W),
                   even_w_out[0].astype(BF16), x2d, per_batch(0, 2))
    x2d = _peer(x2d, row(norm2_g[0]), per_batch(0, 4), per_batch(0, 3), per_batch(0, 5),
                peer_wq[0], peer_keys[0], eu_all, evt_all, fg, 0, False)

    bs = jnp.broadcast_to(sgu_b[0][:, :, None], (SGU_GROUPS, CHUNK, CHUNK))
    x2d = _sgu(x2d, row(norm1_g[1]), per_batch(1, 1), per_batch(1, 0), odd_w_in[0].astype(BF16),
               row(odd_b_in[0]), row(sgu_norm_g[0]), sgu_w[0].astype(BF16), bs, odd_w_out[0].astype(BF16),
               per_batch(1, 2))
    x2d = _peer(x2d, row(norm2_g[1]), per_batch(1, 4), per_batch(1, 3), per_batch(1, 5),
                peer_wq[1], peer_keys[1], eu_all, evt_all, fg, 1, True)
    return x2d.reshape(BATCH, SEQ, D_MODEL)
```

```python
import functools
import math

import numpy as np
import jax
import jax.numpy as jnp
from jax import lax
from jax.experimental import pallas as pl
from jax.experimental.pallas import tpu as pltpu

F32 = jnp.float32
BF16 = jnp.bfloat16

D_MODEL = 1024
BATCH = 4
SEQ = 4096
GRID_W = 64
CTX_LEN = 256
SEQ_ALL = CTX_LEN + SEQ
EPS = 1e-6

DIFF_HEADS = 6
DIFF_DH = 64
HEAD_W = 2 * DIFF_DH
DIFF_QK = DIFF_HEADS * HEAD_W
FOURIER_GROUPS = 4
FOURIER_GC = 64
FOURIER_W = FOURIER_GROUPS * FOURIER_GC
EVEN_IN = 3 * DIFF_QK + FOURIER_W
ROPE_BASE = 10000.0
ROPE_PAIRS = DIFF_DH // 4
LAM_INIT_L1 = 0.8 - 0.6 * math.exp(-0.3 * 0.0)
LOG2_E = math.log2(math.e)

CHUNK = 128
SGU_GROUPS = 8
SGU_W = SGU_GROUPS * CHUNK

PEER_HEADS = 8
N_KEYS = 128
N_EXPERTS = N_KEYS * N_KEYS
PEER_TOPK = 16
PEER_HALF = 128
LANES = 128

VMEM_LIMIT = 56 * 1024 * 1024

NT_DIMS = (((1,), (1,)), ((), ()))


def _rms(t, g):
    return t * lax.rsqrt(jnp.mean(t * t, axis=-1, keepdims=True) + EPS) * g


def _gelu(x):
    cdf = 0.5 * (1.0 + jnp.tanh(math.sqrt(2.0 / math.pi) * (x + 0.044715 * (x * x * x))))
    return x * cdf


def _gelu_packed_x2(x):
    c = math.sqrt(2.0 / math.pi)
    return x + x * jnp.tanh(x * (c + (c * 0.044715) * (x * x)))


def _params(*sem, flags=None):
    return pltpu.CompilerParams(dimension_semantics=sem, vmem_limit_bytes=VMEM_LIMIT, flags=flags)


def _ada_kernel(c_ref, w_ref, b_ref, o_ref):
    c = c_ref[...]
    s = c * jax.nn.sigmoid(c)
    o_ref[...] = jnp.dot(s, w_ref[...], preferred_element_type=F32) + b_ref[...]


def _ada(cvec, ada_w, ada_b):
    depth = ada_w.shape[0]
    tn = 1536
    return pl.pallas_call(
        _ada_kernel,
        grid=(depth, 6 * D_MODEL // tn),
        in_specs=[
            pl.BlockSpec((8, D_MODEL), lambda l, j: (0, 0)),
            pl.BlockSpec((None, D_MODEL, tn), lambda l, j: (l, 0, j)),
            pl.BlockSpec((None, 1, tn), lambda l, j: (l, 0, j)),
        ],
        out_specs=pl.BlockSpec((None, 8, tn), lambda l, j: (l, 0, j)),
        out_shape=jax.ShapeDtypeStruct((depth, 8, 6 * D_MODEL), F32),
        compiler_params=_params("arbitrary", "arbitrary"),
        name="ada",
    )(cvec, ada_w, ada_b.reshape(depth, 1, 6 * D_MODEL))


IN_TM = 256
TILES_PER_SEQ = SEQ_ALL // IN_TM


def _inproj_kernel(x_ref, g_ref, sc_ref, sh_ref, w_ref, cos_ref, sa_ref, sb_ref, o_ref):
    x = x_ref[...]
    h = _rms(x, g_ref[...]) * (1.0 + sc_ref[...]) + sh_ref[...]
    r = jnp.dot(h.astype(BF16), w_ref[...], preferred_element_type=F32)
    cos, sa, sb = cos_ref[...], sa_ref[...], sb_ref[...]
    for cb in range(2 * DIFF_HEADS):
        t = r[:, cb * HEAD_W:(cb + 1) * HEAD_W]
        rot = t * cos + pltpu.roll(t, HEAD_W - 16, 1) * sa + pltpu.roll(t, 16, 1) * sb
        if cb < DIFF_HEADS:
            rot = rot * (DIFF_DH ** -0.5 * LOG2_E)
        o_ref[:, cb * HEAD_W:(cb + 1) * HEAD_W] = rot.astype(BF16)
    o_ref[:, 2 * DIFF_QK:] = r[:, 2 * DIFF_QK:].astype(BF16)


def _inproj(xin, g, sc, sh, w, cos, sa, sb):
    rows = xin.shape[0]
    mod_spec = pl.BlockSpec((None, None, 1, D_MODEL),
                            lambda i: (i // TILES_PER_SEQ, jnp.where(i % TILES_PER_SEQ == TILES_PER_SEQ - 1, 1, 0), 0, 0))
    rope_spec = pl.BlockSpec((IN_TM, HEAD_W), lambda i: (i % TILES_PER_SEQ, 0))
    return pl.pallas_call(
        _inproj_kernel,
        grid=(rows // IN_TM,),
        in_specs=[
            pl.BlockSpec((IN_TM, D_MODEL), lambda i: (i, 0)),
            pl.BlockSpec((1, D_MODEL), lambda i: (0, 0)),
            mod_spec, mod_spec,
            pl.BlockSpec((D_MODEL, EVEN_IN), lambda i: (0, 0)),
            rope_spec, rope_spec, rope_spec,
        ],
        out_specs=pl.BlockSpec((IN_TM, EVEN_IN), lambda i: (i, 0)),
        out_shape=jax.ShapeDtypeStruct((rows, EVEN_IN), BF16),
        compiler_params=_params("arbitrary"),
        name="inproj",
    )(xin, g, sc, sh, w, cos, sa, sb)


def _rope_tables():
    rows = SEQ // GRID_W
    r = jnp.repeat(jnp.arange(rows, dtype=F32), GRID_W)
    col = jnp.tile(jnp.arange(GRID_W, dtype=F32), rows)
    inv = ROPE_BASE ** (-jnp.arange(ROPE_PAIRS, dtype=F32) / ROPE_PAIRS)
    ar = r[:, None] * inv
    ac = col[:, None] * inv
    ang = jnp.concatenate([ar, ar, ac, ac] * 2, axis=-1)
    cos, sin = jnp.cos(ang), jnp.sin(ang)
    even = ((np.arange(HEAD_W) // ROPE_PAIRS) % 2 == 0)[None, :]
    sa = jnp.where(even, -sin, 0.0)
    sb = jnp.where(even, 0.0, sin)
    ident = jnp.ones((CTX_LEN, HEAD_W), F32)
    zero = jnp.zeros((CTX_LEN, HEAD_W), F32)
    return (jnp.concatenate([cos, ident], 0), jnp.concatenate([sa, zero], 0),
            jnp.concatenate([sb, zero], 0))


ATT_TQ = 1024
ATT_CHAINS = 4
ATT_KC = 256


def _attn_kernel(lam_ref, g_ref, q_ref, k_ref, v_ref, o_ref, vt_ref, st_ref, e_ref):
    @pl.when(pl.program_id(2) == 0)
    def _():
        vt_ref[:HEAD_W, :] = v_ref[...].astype(F32).T.astype(BF16)
        ones_row = lax.broadcasted_iota(jnp.int32, (16, SEQ_ALL), 0) == 0
        vt_ref[HEAD_W:, :] = jnp.where(ones_row, 1.0, 0.0).astype(BF16)

    lane = lax.broadcasted_iota(jnp.int32, (1, HEAD_W), 1)
    lp = lam_ref[...]
    lam = (jnp.exp(jnp.sum(lp[0:1] * lp[1:2], axis=-1, keepdims=True))
           - jnp.exp(jnp.sum(lp[2:3] * lp[3:4], axis=-1, keepdims=True)) + LAM_INIT_L1)
    tq = ATT_TQ // ATT_CHAINS
    n_kc = SEQ_ALL // ATT_KC

    def scores(c):
        q = q_ref[c * tq:(c + 1) * tq, :].astype(F32)
        qq = jnp.concatenate([jnp.where(lane < DIFF_DH, q, 0.0), jnp.where(lane >= DIFF_DH, q, 0.0)], axis=0)
        st_ref[c % 2] = lax.dot_general(k_ref[...], qq.astype(BF16), NT_DIMS, preferred_element_type=F32)

    def softmax(c):
        slot = c % 2
        m = functools.reduce(jnp.maximum, [jnp.max(st_ref[slot, kc * ATT_KC:(kc + 1) * ATT_KC, :], axis=0,
                                                    keepdims=True) for kc in range(n_kc)])
        for kc in range(n_kc):
            ks = slice(kc * ATT_KC, (kc + 1) * ATT_KC)
            e_ref[slot, ks, :] = jnp.exp2(st_ref[slot, ks, :] - m).astype(BF16)

    def values(c):
        ext = jnp.dot(vt_ref[...], e_ref[c % 2], preferred_element_type=F32)
        ot = ext[:HEAD_W] / ext[HEAD_W:HEAD_W + 1]
        o = (ot[:, :tq] - lam * ot[:, tq:]).T
        o_ref[c * tq:(c + 1) * tq, :] = (_rms(o, g_ref[...]) * (1.0 - LAM_INIT_L1)).astype(BF16)

    scores(0)
    for c in range(ATT_CHAINS):
        if c + 1 < ATT_CHAINS:
            scores(c + 1)
        softmax(c)
        if c > 0:
            values(c - 1)
    values(ATT_CHAINS - 1)


def _attn(qkvf, lam_p, head_g):
    nq = SEQ // ATT_TQ
    return pl.pallas_call(
        _attn_kernel,
        grid=(BATCH, DIFF_HEADS, nq),
        in_specs=[
            pl.BlockSpec((4, DIFF_DH), lambda b, h, i: (0, 0)),
            pl.BlockSpec((1, HEAD_W), lambda b, h, i: (0, 0)),
            pl.BlockSpec((None, ATT_TQ, HEAD_W), lambda b, h, i: (b, i, h)),
            pl.BlockSpec((None, SEQ_ALL, HEAD_W), lambda b, h, i: (b, 0, DIFF_HEADS + h)),
            pl.BlockSpec((None, SEQ_ALL, HEAD_W), lambda b, h, i: (b, 0, 2 * DIFF_HEADS + h)),
        ],
        out_specs=pl.BlockSpec((None, ATT_TQ, HEAD_W), lambda b, h, i: (b, i, h)),
        out_shape=jax.ShapeDtypeStruct((BATCH, SEQ, DIFF_QK), BF16),
        scratch_shapes=[pltpu.VMEM((HEAD_W + 16, SEQ_ALL), BF16),
                        pltpu.VMEM((2, SEQ_ALL, 2 * ATT_TQ // ATT_CHAINS), F32),
                        pltpu.VMEM((2, SEQ_ALL, 2 * ATT_TQ // ATT_CHAINS), BF16)],
        compiler_params=_params("arbitrary", "arbitrary", "arbitrary"),
        name="diff_attn",
    )(lam_p, head_g, qkvf, qkvf, qkvf)


def _channel_dft():
    n = np.arange(FOURIER_GC)
    ang = 2.0 * np.pi * np.outer(n, n) / FOURIER_GC
    eye = np.eye(FOURIER_GROUPS)
    scale = FOURIER_GC ** -0.5
    return np.concatenate([np.kron(eye, np.cos(ang)), np.kron(eye, np.sin(ang))], axis=1) * scale


def _seq_dft():
    k = jnp.arange(SEQ, dtype=jnp.int32)[:, None]
    n = jnp.arange(GRID_W, dtype=jnp.int32)[None, :]
    a = ((k * n) % GRID_W).astype(F32) * (2.0 * math.pi / GRID_W)
    b = ((k * n) % SEQ).astype(F32) * (2.0 * math.pi / SEQ)
    ca, sa, cb, sb = jnp.cos(a)[:, :, None], jnp.sin(a)[:, :, None], jnp.cos(b)[:, None, :], jnp.sin(b)[:, None, :]
    scale = SEQ ** -0.5
    cos = ((ca * cb - sa * sb) * scale).reshape(SEQ, SEQ)
    sin = ((sa * cb + ca * sb) * scale).reshape(SEQ, SEQ)
    return jnp.concatenate([cos, -sin], axis=1).astype(BF16)


def _fourier_chan_kernel(f_ref, bd_ref, o_ref):
    f = f_ref[:SEQ, :]
    uv = jnp.dot(f, bd_ref[...], preferred_element_type=F32)
    o_ref[:SEQ, :] = uv[:, :FOURIER_W].astype(BF16)
    o_ref[SEQ:, :] = uv[:, FOURIER_W:].astype(BF16)


def _fourier_seq_kernel(t_ref, uv_ref, o_ref):
    o_ref[...] = jnp.dot(t_ref[...], uv_ref[...], preferred_element_type=F32).astype(BF16)


FOURIER_TM = 512


def _fourier(qkvf, seq_dft):
    bd = jnp.asarray(_channel_dft(), dtype=BF16)
    uv = pl.pallas_call(
        _fourier_chan_kernel,
        grid=(BATCH,),
        in_specs=[
            pl.BlockSpec((None, SEQ_ALL, FOURIER_W), lambda b: (b, 0, 3 * DIFF_QK // FOURIER_W)),
            pl.BlockSpec((FOURIER_W, 2 * FOURIER_W), lambda b: (0, 0)),
        ],
        out_specs=pl.BlockSpec((None, 2 * SEQ, FOURIER_W), lambda b: (b, 0, 0)),
        out_shape=jax.ShapeDtypeStruct((BATCH, 2 * SEQ, FOURIER_W), BF16),
        compiler_params=_params("arbitrary"),
        name="fourier_chan",
    )(qkvf, bd)
    return pl.pallas_call(
        _fourier_seq_kernel,
        grid=(SEQ // FOURIER_TM, BATCH),
        in_specs=[
            pl.BlockSpec((FOURIER_TM, 2 * SEQ), lambda i, b: (i, 0)),
            pl.BlockSpec((None, 2 * SEQ, FOURIER_W), lambda i, b: (b, 0, 0)),
        ],
        out_specs=pl.BlockSpec((None, FOURIER_TM, FOURIER_W), lambda i, b: (b, i, 0)),
        out_shape=jax.ShapeDtypeStruct((BATCH, SEQ, FOURIER_W), BF16),
        compiler_params=_params("arbitrary", "arbitrary"),
        name="fourier_seq",
    )(seq_dft, uv)


OUT_TM = 512


def _outproj_kernel(o_ref, f_ref, w_ref, x_ref, g_ref, out_ref):
    y = jnp.dot(o_ref[...], w_ref[:DIFF_QK, :], preferred_element_type=F32)
    y = y + jnp.dot(f_ref[...], w_ref[DIFF_QK:, :], preferred_element_type=F32)
    out_ref[...] = x_ref[...] + g_ref[...] * y


def _outproj(o, fm, w, x2d, g1):
    rows = x2d.shape[0]
    per_b = SEQ // OUT_TM
    return pl.pallas_call(
        _outproj_kernel,
        grid=(rows // OUT_TM,),
        in_specs=[
            pl.BlockSpec((OUT_TM, DIFF_QK), lambda i: (i, 0)),
            pl.BlockSpec((OUT_TM, FOURIER_W), lambda i: (i, 0)),
            pl.BlockSpec((D_MODEL, D_MODEL), lambda i: (0, 0)),
            pl.BlockSpec((OUT_TM, D_MODEL), lambda i: (i, 0)),
            pl.BlockSpec((None, 1, D_MODEL), lambda i: (i // per_b, 0, 0)),
        ],
        out_specs=pl.BlockSpec((OUT_TM, D_MODEL), lambda i: (i, 0)),
        out_shape=jax.ShapeDtypeStruct((rows, D_MODEL), F32),
        compiler_params=_params("arbitrary"),
        name="outproj",
    )(o, fm, w, x2d, g1)


SGU_TM = 256


def _sgu_kernel(x_ref, g_ref, sc_ref, sh_ref, win_ref, bin_ref, ng_ref, ws_ref, bs_ref, wout_ref, g1_ref,
                out_ref, us_ref):
    x = x_ref[...]
    h = _rms(x, g_ref[...]) * (1.0 + sc_ref[...]) + sh_ref[...]
    z = _gelu(jnp.dot(h.astype(BF16), win_ref[...], preferred_element_type=F32) + bin_ref[...])
    u = z[:, :SGU_W]
    v = _rms(z[:, SGU_W:], ng_ref[...]).astype(BF16)
    for ck in range(SGU_TM // CHUNK):
        rows = slice(ck * CHUNK, (ck + 1) * CHUNK)
        for g in range(SGU_GROUPS):
            cols = slice(g * CHUNK, (g + 1) * CHUNK)
            s = jnp.dot(ws_ref[g], v[rows, cols], preferred_element_type=F32) + bs_ref[g]
            us_ref[rows, cols] = (u[rows, cols] * s).astype(BF16)
    y = jnp.dot(us_ref[...], wout_ref[...], preferred_element_type=F32)
    out_ref[...] = x + g1_ref[...] * y


def _sgu(x2d, g, sc, sh, win, b_in, ng, ws, bs, wout, g1):
    rows = x2d.shape[0]
    per_b = SEQ // SGU_TM
    vec = lambda n: pl.BlockSpec((1, n), lambda i: (0, 0))
    per_batch = pl.BlockSpec((None, 1, D_MODEL), lambda i: (i // per_b, 0, 0))
    return pl.pallas_call(
        _sgu_kernel,
        grid=(rows // SGU_TM,),
        in_specs=[
            pl.BlockSpec((SGU_TM, D_MODEL), lambda i: (i, 0)),
            vec(D_MODEL), per_batch, per_batch,
            pl.BlockSpec((D_MODEL, 2 * SGU_W), lambda i: (0, 0)),
            vec(2 * SGU_W), vec(SGU_W),
            pl.BlockSpec((SGU_GROUPS, CHUNK, CHUNK), lambda i: (0, 0, 0)),
            pl.BlockSpec((SGU_GROUPS, CHUNK, CHUNK), lambda i: (0, 0, 0)),
            pl.BlockSpec((SGU_W, D_MODEL), lambda i: (0, 0)),
            per_batch,
        ],
        out_specs=pl.BlockSpec((SGU_TM, D_MODEL), lambda i: (i, 0)),
        out_shape=jax.ShapeDtypeStruct((rows, D_MODEL), F32),
        scratch_shapes=[pltpu.VMEM((SGU_TM, SGU_W), BF16)],
        compiler_params=_params("arbitrary"),
        name="sgu",
    )(x2d, g, sc, sh, win, b_in, ng, ws, bs, wout, g1)


ROUTE_TM = 512
PAIR_CELLS = [(a, b) for a in range(PEER_TOPK + 1) for b in range(PEER_TOPK + 1)
              if (a + 1) * (b + 1) <= PEER_TOPK + 1]


def _sort16_network():
    pairs = []

    def merge(lo, n, r):
        step = 2 * r
        if step < n:
            merge(lo, n, step)
            merge(lo + r, n, step)
            pairs.extend((i, i + r) for i in range(lo + r, lo + n - r, step))
        else:
            pairs.append((lo, lo + r))

    def sort(lo, n):
        if n > 1:
            sort(lo, n // 2)
            sort(lo + n // 2, n // 2)
            merge(lo, n, 1)

    sort(0, 16)
    return pairs


SORT16 = _sort16_network()


def _top_sorted(st):
    sub = 8
    groups = [st[g * sub:(g + 1) * sub, :] for g in range(N_KEYS // sub)]
    for i, j in SORT16:
        groups[i], groups[j] = jnp.maximum(groups[i], groups[j]), jnp.minimum(groups[i], groups[j])
    floor = jnp.full_like(groups[0], -jnp.inf)
    vals = []
    for r in range(PEER_TOPK + 1):
        m = jnp.max(groups[0], axis=0, keepdims=True)
        vals.append(m)
        if r < PEER_TOPK:
            pop = groups[0] == m
            for d in range(PEER_TOPK - r):
                below = groups[d + 1] if d + 1 < len(groups) else floor
                groups[d] = jnp.where(pop, below, groups[d])
    return vals


def _count_above(vals, theta):
    def pivot(conds, lo, step):
        if not conds:
            return vals[lo + step - 1]
        return jnp.where(conds[0], pivot(conds[1:], lo + 2 * step * 2 ** (len(conds) - 1), step),
                         pivot(conds[1:], lo, step))

    conds, count = [], None
    for step in (8, 4, 2, 1):
        hit = pivot(conds, 0, step) > theta
        count = jnp.where(hit, float(step), 0.0) if count is None else jnp.where(hit, count + float(step), count)
        conds.append(hit)
    return jnp.where(vals[-1] > theta, count + 1.0, count)


def _route_kernel(x_ref, g_ref, sc_ref, sh_ref, wq_ref, keys_ref,
                  t_ref, cnt_ref, a_ref, r2_ref, b_ref, s_scr):
    x = x_ref[...]
    tf = _rms(x, g_ref[...]) * (1.0 + sc_ref[...]) + sh_ref[...]
    t_ref[...] = tf.T.astype(BF16)
    q = jnp.dot(tf.astype(BF16), wq_ref[...], preferred_element_type=F32)
    neg = jnp.float32(-jnp.inf)
    tops = []
    for hs in range(2 * PEER_HEADS):
        qh = q[:, hs * PEER_HALF:(hs + 1) * PEER_HALF].astype(BF16)
        st = lax.dot_general(keys_ref[hs], qh, NT_DIMS, preferred_element_type=F32)
        s_scr[hs] = st
        vals = _top_sorted(st)
        if hs % 2 == 1:
            r2_ref[hs // 2] = _count_above(vals[:PEER_TOPK], st).astype(BF16)
        tops.append(vals)
    v1 = [jnp.concatenate([tops[2 * h][r] for h in range(PEER_HEADS)], axis=0) for r in range(PEER_TOPK + 1)]
    v2 = [jnp.concatenate([tops[2 * h + 1][r] for h in range(PEER_HEADS)], axis=0) for r in range(PEER_TOPK + 1)]
    cand = [v1[a] + v2[b] for a, b in PAIR_CELLS]
    cur = cand
    kth = None
    for r in range(PEER_TOPK + 1):
        m = functools.reduce(jnp.maximum, cur)
        if r == PEER_TOPK - 1:
            kth = m
        if r < PEER_TOPK:
            cur = [jnp.where(c == m, neg, c) for c in cur]
    thr = 0.5 * (kth + m)
    top = v1[0] + v2[0]
    z = functools.reduce(lambda p, c: p + jnp.where(c >= thr, jnp.exp(c - top), 0.0), cand, jnp.zeros_like(top))
    rz = 1.0 / z
    for h in range(PEER_HEADS):
        s1 = s_scr[2 * h]
        theta = thr[h:h + 1] - s1
        cnt = _count_above([vb[h:h + 1] for vb in v2[:PEER_TOPK]], theta)
        gate = jnp.exp(s1 - v1[0][h:h + 1]) * (0.5 * rz[h:h + 1])
        for c in range(ROUTE_TM // LANES):
            cnt_ref[h, c] = cnt[:, c * LANES:(c + 1) * LANES]
            a_ref[h, c] = gate[:, c * LANES:(c + 1) * LANES]
        b_ref[h] = jnp.exp(s_scr[2 * h + 1] - v2[0][h:h + 1]).astype(BF16)


def _route(x2d, g, sc, sh, wq, keys):
    rows = x2d.shape[0]
    per_b = SEQ // ROUTE_TM
    per_batch = pl.BlockSpec((None, 1, D_MODEL), lambda i: (i // per_b, 0, 0))
    tbl = pl.BlockSpec((PEER_HEADS, N_KEYS, ROUTE_TM), lambda i: (0, 0, i))
    row_tbl = jax.ShapeDtypeStruct((PEER_HEADS, rows // LANES, N_KEYS, LANES), F32)
    row_spec = pl.BlockSpec((PEER_HEADS, ROUTE_TM // LANES, N_KEYS, LANES), lambda i: (0, i, 0, 0))
    col_tbl = jax.ShapeDtypeStruct((PEER_HEADS, N_KEYS, rows), BF16)
    return pl.pallas_call(
        _route_kernel,
        grid=(rows // ROUTE_TM,),
        in_specs=[
            pl.BlockSpec((ROUTE_TM, D_MODEL), lambda i: (i, 0)),
            pl.BlockSpec((1, D_MODEL), lambda i: (0, 0)),
            per_batch, per_batch,
            pl.BlockSpec((D_MODEL, 2 * PEER_HEADS * PEER_HALF), lambda i: (0, 0)),
            pl.BlockSpec((2 * PEER_HEADS, N_KEYS, PEER_HALF), lambda i: (0, 0, 0)),
        ],
        out_specs=[pl.BlockSpec((D_MODEL, ROUTE_TM), lambda i: (0, i)), row_spec, row_spec, tbl, tbl],
        out_shape=[jax.ShapeDtypeStruct((D_MODEL, rows), BF16), row_tbl, row_tbl, col_tbl, col_tbl],
        scratch_shapes=[pltpu.VMEM((2 * PEER_HEADS, N_KEYS, ROUTE_TM), F32)],
        compiler_params=_params("arbitrary"),
        name="peer_route",
    )(x2d, g, sc, sh, wq, keys)


DENSE_TM = 512
DENSE_TE = 2048
DENSE_SUB = 512
DENSE_SHARE = 2
DENSE_JC = 32


def _dense_kernel(t_ref, cnt_ref, a_ref, r2_ref, b_ref, eu_ref, evt_ref, x_ref, g2_ref, fg_ref,
                  out_ref, acc_ref, act_ref, g_ref, rt_ref, *, final_norm):
    e = pl.program_id(1)

    @pl.when(e == 0)
    def _():
        acc_ref[...] = jnp.zeros_like(acc_ref)

    tm = t_ref.shape[1]
    zero = jnp.zeros((), BF16)

    def row_tile(ref, h, i):
        return jnp.concatenate([ref[h, c, pl.ds(i, 16, stride=0), :] for c in range(tm // LANES)],
                               axis=1).astype(BF16)

    def tiled(ref, row, k, h):
        return jnp.concatenate([ref[row, k, h]] * (DENSE_JC // 16), axis=0)

    rows_per_sub = DENSE_SUB // N_KEYS
    n_sub = DENSE_TE // DENSE_SUB

    def activations(sb, slot):
        start = sb * DENSE_SUB
        if not isinstance(sb, int):
            start = pl.multiple_of(start, DENSE_SUB)
        act_ref[slot] = jnp.dot(eu_ref[pl.ds(start, DENSE_SUB), :], t_ref[...],
                                preferred_element_type=F32)

    def gating(sb, slot):
        for ii in range(rows_per_sub):
            i = e * (DENSE_TE // N_KEYS) + sb * rows_per_sub + ii
            for h in range(PEER_HEADS):
                rt_ref[slot * rows_per_sub + ii, 0, h] = row_tile(cnt_ref, h, i)
                rt_ref[slot * rows_per_sub + ii, 1, h] = row_tile(a_ref, h, i)
        for pair in range(0, rows_per_sub, DENSE_SHARE):
            for jc in range(N_KEYS // DENSE_JC):
                js = slice(jc * DENSE_JC, (jc + 1) * DENSE_JC)
                w = [None] * DENSE_SHARE
                for h in range(PEER_HEADS):
                    r2 = r2_ref[h, js, :]
                    bb = b_ref[h, js, :]
                    for s in range(DENSE_SHARE):
                        row = slot * rows_per_sub + pair + s
                        term = jnp.where(r2 < tiled(rt_ref, row, 0, h), bb * tiled(rt_ref, row, 1, h), zero)
                        w[s] = term if w[s] is None else w[s] + term
                for s in range(DENSE_SHARE):
                    ii = pair + s
                    rows = slice(ii * N_KEYS + jc * DENSE_JC, ii * N_KEYS + (jc + 1) * DENSE_JC)
                    g_ref[slot, rows, :] = _gelu_packed_x2(act_ref[slot, rows, :].astype(BF16)) * w[s]

    def accumulate(sb, slot):
        acc_ref[...] += jnp.dot(evt_ref[sb], g_ref[slot], preferred_element_type=F32)

    activations(0, 0)
    activations(1, 1)
    gating(0, 0)
    for sb in range(1, n_sub - 1, 2):
        activations(sb + 1, 0)
        gating(sb, 1)
        accumulate(sb - 1, 0)
        activations(sb + 2, 1)
        gating(sb + 1, 0)
        accumulate(sb, 1)
    gating(n_sub - 1, 1)
    accumulate(n_sub - 2, 0)
    accumulate(n_sub - 1, 1)

    @pl.when(e == pl.num_programs(1) - 1)
    def _():
        y = x_ref[...] + g2_ref[...] * acc_ref[...].T
        if final_norm:
            y = _rms(y, fg_ref[...])
        out_ref[...] = y


def _dense(t, cnt, a, r2, b, eu, evt, x2d, g2, fg, layer, final_norm):
    rows = x2d.shape[0]
    per_b = SEQ // DENSE_TM
    tbl = pl.BlockSpec((PEER_HEADS, N_KEYS, DENSE_TM), lambda i, e: (0, 0, i))
    row_spec = pl.BlockSpec((PEER_HEADS, DENSE_TM // LANES, N_KEYS, LANES), lambda i, e: (0, i, 0, 0))
    return pl.pallas_call(
        functools.partial(_dense_kernel, final_norm=final_norm),
        grid=(rows // DENSE_TM, N_EXPERTS // DENSE_TE),
        in_specs=[
            pl.BlockSpec((D_MODEL, DENSE_TM), lambda i, e: (0, i)),
            row_spec, row_spec, tbl, tbl,
            pl.BlockSpec((None, DENSE_TE, D_MODEL), lambda i, e: (layer, e, 0)),
            pl.BlockSpec((None, DENSE_TE // DENSE_SUB, D_MODEL, DENSE_SUB), lambda i, e: (layer, e, 0, 0)),
            pl.BlockSpec((DENSE_TM, D_MODEL), lambda i, e: (i, 0)),
            pl.BlockSpec((None, 1, D_MODEL), lambda i, e: (i // per_b, 0, 0)),
            pl.BlockSpec((1, D_MODEL), lambda i, e: (0, 0)),
        ],
        out_specs=pl.BlockSpec((DENSE_TM, D_MODEL), lambda i, e: (i, 0)),
        out_shape=jax.ShapeDtypeStruct((rows, D_MODEL), F32),
        scratch_shapes=[pltpu.VMEM((D_MODEL, DENSE_TM), F32), pltpu.VMEM((2, DENSE_SUB, DENSE_TM), F32),
                        pltpu.VMEM((2, DENSE_SUB, DENSE_TM), BF16),
                        pltpu.VMEM((2 * DENSE_SUB // N_KEYS, 2, PEER_HEADS, 16, DENSE_TM), BF16)],
        compiler_params=_params("arbitrary", "arbitrary"),
        name="peer_dense",
    )(t, cnt, a, r2, b, eu, evt, x2d, g2, fg)


def _peer(x2d, g, sc, sh, g2, wq, keys, eu_all, evt_all, fg, layer, final_norm):
    wq_b = wq.astype(BF16)
    keys_b = keys.reshape(2 * PEER_HEADS, N_KEYS, PEER_HALF).astype(BF16)
    t, cnt, a, r2, b = _route(x2d, g, sc, sh, wq_b, keys_b)
    return _dense(t, cnt, a, r2, b, eu_all, evt_all, x2d, g2, fg, layer, final_norm)


def kernel(x, c, ctx, c_ctx, ada_w, ada_b, norm1_g, norm2_g, final_g, even_w_in, even_w_out, diff_lambda,
           diff_norm_g, odd_w_in, odd_b_in, sgu_norm_g, sgu_w, sgu_b, odd_w_out, peer_wq, peer_keys, peer_u, peer_v):
    cvec = jnp.concatenate([c, c_ctx[None, :], jnp.zeros((8 - BATCH - 1, D_MODEL), F32)], axis=0)
    mod = _ada(cvec, ada_w, ada_b).reshape(2, 8, 6, D_MODEL)
    row = lambda v: v.reshape(1, -1)
    per_batch = lambda l, j: mod[l, :BATCH, j].reshape(BATCH, 1, D_MODEL)
    fg = row(final_g)

    both = lambda j: jnp.stack([mod[0, :BATCH, j], jnp.broadcast_to(mod[0, BATCH, j], (BATCH, D_MODEL))],
                               axis=1).reshape(BATCH, 2, 1, D_MODEL)
    xin = jnp.concatenate([x, ctx], axis=1).reshape(BATCH * SEQ_ALL, D_MODEL)
    cos, sa, sb = _rope_tables()
    qkvf = _inproj(xin, row(norm1_g[0]), both(1), both(0), even_w_in[0].astype(BF16), cos, sa, sb)
    qkvf = qkvf.reshape(BATCH, SEQ_ALL, EVEN_IN)
    o = _attn(qkvf, diff_lambda[0], row(diff_norm_g[0]))
    fm = _fourier(qkvf, _seq_dft())
    eu_all = peer_u.astype(BF16)
    evt_all = peer_v.astype(BF16).reshape(2, N_EXPERTS // DENSE_SUB, DENSE_SUB, D_MODEL).transpose(0, 1, 3, 2)
    x2d = x.reshape(BATCH * SEQ, D_MODEL)
    x2d = _outproj(o.reshape(BATCH * SEQ, DIFF_QK), fm.reshape(BATCH * SEQ, FOURIER_W),
                   even_w_out[0].astype(BF16), x2d, per_batch(0, 2))
    x2d = _peer(x2d, row(norm2_g[0]), per_batch(0, 4), per_batch(0, 3), per_batch(0, 5),
                peer_wq[0], peer_keys[0], eu_all, evt_all, fg, 0, False)

    bs = jnp.broadcast_to(sgu_b[0][:, :, None], (SGU_GROUPS, CHUNK, CHUNK))
    x2d = _sgu(x2d, row(norm1_g[1]), per_batch(1, 1), per_batch(1, 0), odd_w_in[0].astype(BF16),
               row(odd_b_in[0]), row(sgu_norm_g[0]), sgu_w[0].astype(BF16), bs, odd_w_out[0].astype(BF16),
               per_batch(1, 2))
    x2d = _peer(x2d, row(norm2_g[1]), per_batch(1, 4), per_batch(1, 3), per_batch(1, 5),
                peer_wq[1], peer_keys[1], eu_all, evt_all, fg, 1, True)
    return x2d.reshape(BATCH, SEQ, D_MODEL)
```

```python
import functools
import math

import numpy as np
import jax
import jax.numpy as jnp
from jax import lax
from jax.experimental import pallas as pl
from jax.experimental.pallas import tpu as pltpu

F32 = jnp.float32
BF16 = jnp.bfloat16

D_MODEL = 1024
BATCH = 4
SEQ = 4096
GRID_W = 64
CTX_LEN = 256
SEQ_ALL = CTX_LEN + SEQ
EPS = 1e-6

DIFF_HEADS = 6
DIFF_DH = 64
HEAD_W = 2 * DIFF_DH
DIFF_QK = DIFF_HEADS * HEAD_W
FOURIER_GROUPS = 4
FOURIER_GC = 64
FOURIER_W = FOURIER_GROUPS * FOURIER_GC
EVEN_IN = 3 * DIFF_QK + FOURIER_W
ROPE_BASE = 10000.0
ROPE_PAIRS = DIFF_DH // 4
LAM_INIT_L1 = 0.8 - 0.6 * math.exp(-0.3 * 0.0)
LOG2_E = math.log2(math.e)

CHUNK = 128
SGU_GROUPS = 8
SGU_W = SGU_GROUPS * CHUNK

PEER_HEADS = 8
N_KEYS = 128
N_EXPERTS = N_KEYS * N_KEYS
PEER_TOPK = 16
PEER_HALF = 128
LANES = 128

VMEM_LIMIT = 56 * 1024 * 1024

NT_DIMS = (((1,), (1,)), ((), ()))


def _rms(t, g):
    return t * lax.rsqrt(jnp.mean(t * t, axis=-1, keepdims=True) + EPS) * g


def _gelu(x):
    cdf = 0.5 * (1.0 + jnp.tanh(math.sqrt(2.0 / math.pi) * (x + 0.044715 * (x * x * x))))
    return x * cdf


def _gelu_packed_x2(x):
    c = math.sqrt(2.0 / math.pi)
    return x + x * jnp.tanh(x * (c + (c * 0.044715) * (x * x)))


def _params(*sem, flags=None):
    return pltpu.CompilerParams(dimension_semantics=sem, vmem_limit_bytes=VMEM_LIMIT, flags=flags)


def _ada_kernel(c_ref, w_ref, b_ref, o_ref):
    c = c_ref[...]
    s = c * jax.nn.sigmoid(c)
    o_ref[...] = jnp.dot(s, w_ref[...], preferred_element_type=F32) + b_ref[...]


def _ada(cvec, ada_w, ada_b):
    depth = ada_w.shape[0]
    tn = 1536
    return pl.pallas_call(
        _ada_kernel,
        grid=(depth, 6 * D_MODEL // tn),
        in_specs=[
            pl.BlockSpec((8, D_MODEL), lambda l, j: (0, 0)),
            pl.BlockSpec((None, D_MODEL, tn), lambda l, j: (l, 0, j)),
            pl.BlockSpec((None, 1, tn), lambda l, j: (l, 0, j)),
        ],
        out_specs=pl.BlockSpec((None, 8, tn), lambda l, j: (l, 0, j)),
        out_shape=jax.ShapeDtypeStruct((depth, 8, 6 * D_MODEL), F32),
        compiler_params=_params("arbitrary", "arbitrary"),
        name="ada",
    )(cvec, ada_w, ada_b.reshape(depth, 1, 6 * D_MODEL))


IN_TM = 256
TILES_PER_SEQ = SEQ_ALL // IN_TM


def _inproj_kernel(x_ref, g_ref, sc_ref, sh_ref, w_ref, cos_ref, sa_ref, sb_ref, o_ref):
    x = x_ref[...]
    h = _rms(x, g_ref[...]) * (1.0 + sc_ref[...]) + sh_ref[...]
    r = jnp.dot(h.astype(BF16), w_ref[...], preferred_element_type=F32)
    cos, sa, sb = cos_ref[...], sa_ref[...], sb_ref[...]
    for cb in range(2 * DIFF_HEADS):
        t = r[:, cb * HEAD_W:(cb + 1) * HEAD_W]
        rot = t * cos + pltpu.roll(t, HEAD_W - 16, 1) * sa + pltpu.roll(t, 16, 1) * sb
        if cb < DIFF_HEADS:
            rot = rot * (DIFF_DH ** -0.5 * LOG2_E)
        o_ref[:, cb * HEAD_W:(cb + 1) * HEAD_W] = rot.astype(BF16)
    o_ref[:, 2 * DIFF_QK:] = r[:, 2 * DIFF_QK:].astype(BF16)


def _inproj(xin, g, sc, sh, w, cos, sa, sb):
    rows = xin.shape[0]
    mod_spec = pl.BlockSpec((None, None, 1, D_MODEL),
                            lambda i: (i // TILES_PER_SEQ, jnp.where(i % TILES_PER_SEQ == TILES_PER_SEQ - 1, 1, 0), 0, 0))
    rope_spec = pl.BlockSpec((IN_TM, HEAD_W), lambda i: (i % TILES_PER_SEQ, 0))
    return pl.pallas_call(
        _inproj_kernel,
        grid=(rows // IN_TM,),
        in_specs=[
            pl.BlockSpec((IN_TM, D_MODEL), lambda i: (i, 0)),
            pl.BlockSpec((1, D_MODEL), lambda i: (0, 0)),
            mod_spec, mod_spec,
            pl.BlockSpec((D_MODEL, EVEN_IN), lambda i: (0, 0)),
            rope_spec, rope_spec, rope_spec,
        ],
        out_specs=pl.BlockSpec((IN_TM, EVEN_IN), lambda i: (i, 0)),
        out_shape=jax.ShapeDtypeStruct((rows, EVEN_IN), BF16),
        compiler_params=_params("arbitrary"),
        name="inproj",
    )(xin, g, sc, sh, w, cos, sa, sb)


def _rope_tables():
    rows = SEQ // GRID_W
    r = jnp.repeat(jnp.arange(rows, dtype=F32), GRID_W)
    col = jnp.tile(jnp.arange(GRID_W, dtype=F32), rows)
    inv = ROPE_BASE ** (-jnp.arange(ROPE_PAIRS, dtype=F32) / ROPE_PAIRS)
    ar = r[:, None] * inv
    ac = col[:, None] * inv
    ang = jnp.concatenate([ar, ar, ac, ac] * 2, axis=-1)
    cos, sin = jnp.cos(ang), jnp.sin(ang)
    even = ((np.arange(HEAD_W) // ROPE_PAIRS) % 2 == 0)[None, :]
    sa = jnp.where(even, -sin, 0.0)
    sb = jnp.where(even, 0.0, sin)
    ident = jnp.ones((CTX_LEN, HEAD_W), F32)
    zero = jnp.zeros((CTX_LEN, HEAD_W), F32)
    return (jnp.concatenate([cos, ident], 0), jnp.concatenate([sa, zero], 0),
            jnp.concatenate([sb, zero], 0))


ATT_TQ = 1024
ATT_CHAINS = 4
ATT_KC = 256


def _attn_kernel(lam_ref, g_ref, q_ref, k_ref, v_ref, o_ref, vt_ref, st_ref, e_ref):
    @pl.when(pl.program_id(2) == 0)
    def _():
        vt_ref[:HEAD_W, :] = v_ref[...].astype(F32).T.astype(BF16)
        ones_row = lax.broadcasted_iota(jnp.int32, (16, SEQ_ALL), 0) == 0
        vt_ref[HEAD_W:, :] = jnp.where(ones_row, 1.0, 0.0).astype(BF16)

    lane = lax.broadcasted_iota(jnp.int32, (1, HEAD_W), 1)
    lp = lam_ref[...]
    lam = (jnp.exp(jnp.sum(lp[0:1] * lp[1:2], axis=-1, keepdims=True))
           - jnp.exp(jnp.sum(lp[2:3] * lp[3:4], axis=-1, keepdims=True)) + LAM_INIT_L1)
    tq = ATT_TQ // ATT_CHAINS
    n_kc = SEQ_ALL // ATT_KC

    def scores(c):
        q = q_ref[c * tq:(c + 1) * tq, :].astype(F32)
        qq = jnp.concatenate([jnp.where(lane < DIFF_DH, q, 0.0), jnp.where(lane >= DIFF_DH, q, 0.0)], axis=0)
        st_ref[c % 2] = lax.dot_general(k_ref[...], qq.astype(BF16), NT_DIMS, preferred_element_type=F32)

    def softmax(c):
        slot = c % 2
        m = functools.reduce(jnp.maximum, [jnp.max(st_ref[slot, kc * ATT_KC:(kc + 1) * ATT_KC, :], axis=0,
                                                    keepdims=True) for kc in range(n_kc)])
        for kc in range(n_kc):
            ks = slice(kc * ATT_KC, (kc + 1) * ATT_KC)
            e_ref[slot, ks, :] = jnp.exp2(st_ref[slot, ks, :] - m).astype(BF16)

    def values(c):
        ext = jnp.dot(vt_ref[...], e_ref[c % 2], preferred_element_type=F32)
        ot = ext[:HEAD_W] / ext[HEAD_W:HEAD_W + 1]
        o = (ot[:, :tq] - lam * ot[:, tq:]).T
        o_ref[c * tq:(c + 1) * tq, :] = (_rms(o, g_ref[...]) * (1.0 - LAM_INIT_L1)).astype(BF16)

    scores(0)
    for c in range(ATT_CHAINS):
        if c + 1 < ATT_CHAINS:
            scores(c + 1)
        softmax(c)
        if c > 0:
            values(c - 1)
    values(ATT_CHAINS - 1)


def _attn(qkvf, lam_p, head_g):
    nq = SEQ // ATT_TQ
    return pl.pallas_call(
        _attn_kernel,
        grid=(BATCH, DIFF_HEADS, nq),
        in_specs=[
            pl.BlockSpec((4, DIFF_DH), lambda b, h, i: (0, 0)),
            pl.BlockSpec((1, HEAD_W), lambda b, h, i: (0, 0)),
            pl.BlockSpec((None, ATT_TQ, HEAD_W), lambda b, h, i: (b, i, h)),
            pl.BlockSpec((None, SEQ_ALL, HEAD_W), lambda b, h, i: (b, 0, DIFF_HEADS + h)),
            pl.BlockSpec((None, SEQ_ALL, HEAD_W), lambda b, h, i: (b, 0, 2 * DIFF_HEADS + h)),
        ],
        out_specs=pl.BlockSpec((None, ATT_TQ, HEAD_W), lambda b, h, i: (b, i, h)),
        out_shape=jax.ShapeDtypeStruct((BATCH, SEQ, DIFF_QK), BF16),
        scratch_shapes=[pltpu.VMEM((HEAD_W + 16, SEQ_ALL), BF16),
                        pltpu.VMEM((2, SEQ_ALL, 2 * ATT_TQ // ATT_CHAINS), F32),
                        pltpu.VMEM((2, SEQ_ALL, 2 * ATT_TQ // ATT_CHAINS), BF16)],
        compiler_params=_params("arbitrary", "arbitrary", "arbitrary"),
        name="diff_attn",
    )(lam_p, head_g, qkvf, qkvf, qkvf)


def _channel_dft():
    n = np.arange(FOURIER_GC)
    ang = 2.0 * np.pi * np.outer(n, n) / FOURIER_GC
    eye = np.eye(FOURIER_GROUPS)
    scale = FOURIER_GC ** -0.5
    return np.concatenate([np.kron(eye, np.cos(ang)), np.kron(eye, np.sin(ang))], axis=1) * scale


def _seq_dft():
    k = jnp.arange(SEQ, dtype=jnp.int32)[:, None]
    n = jnp.arange(GRID_W, dtype=jnp.int32)[None, :]
    a = ((k * n) % GRID_W).astype(F32) * (2.0 * math.pi / GRID_W)
    b = ((k * n) % SEQ).astype(F32) * (2.0 * math.pi / SEQ)
    ca, sa, cb, sb = jnp.cos(a)[:, :, None], jnp.sin(a)[:, :, None], jnp.cos(b)[:, None, :], jnp.sin(b)[:, None, :]
    scale = SEQ ** -0.5
    cos = ((ca * cb - sa * sb) * scale).reshape(SEQ, SEQ)
    sin = ((sa * cb + ca * sb) * scale).reshape(SEQ, SEQ)
    return jnp.concatenate([cos, -sin], axis=1).astype(BF16)


def _fourier_chan_kernel(f_ref, bd_ref, o_ref):
    f = f_ref[:SEQ, :]
    uv = jnp.dot(f, bd_ref[...], preferred_element_type=F32)
    o_ref[:SEQ, :] = uv[:, :FOURIER_W].astype(BF16)
    o_ref[SEQ:, :] = uv[:, FOURIER_W:].astype(BF16)


def _fourier_seq_kernel(t_ref, uv_ref, o_ref):
    o_ref[...] = jnp.dot(t_ref[...], uv_ref[...], preferred_element_type=F32).astype(BF16)


FOURIER_TM = 512


def _fourier(qkvf, seq_dft):
    bd = jnp.asarray(_channel_dft(), dtype=BF16)
    uv = pl.pallas_call(
        _fourier_chan_kernel,
        grid=(BATCH,),
        in_specs=[
            pl.BlockSpec((None, SEQ_ALL, FOURIER_W), lambda b: (b, 0, 3 * DIFF_QK // FOURIER_W)),
            pl.BlockSpec((FOURIER_W, 2 * FOURIER_W), lambda b: (0, 0)),
        ],
        out_specs=pl.BlockSpec((None, 2 * SEQ, FOURIER_W), lambda b: (b, 0, 0)),
        out_shape=jax.ShapeDtypeStruct((BATCH, 2 * SEQ, FOURIER_W), BF16),
        compiler_params=_params("arbitrary"),
        name="fourier_chan",
    )(qkvf, bd)
    return pl.pallas_call(
        _fourier_seq_kernel,
        grid=(SEQ // FOURIER_TM, BATCH),
        in_specs=[
            pl.BlockSpec((FOURIER_TM, 2 * SEQ), lambda i, b: (i, 0)),
            pl.BlockSpec((None, 2 * SEQ, FOURIER_W), lambda i, b: (b, 0, 0)),
        ],
        out_specs=pl.BlockSpec((None, FOURIER_TM, FOURIER_W), lambda i, b: (b, i, 0)),
        out_shape=jax.ShapeDtypeStruct((BATCH, SEQ, FOURIER_W), BF16),
        compiler_params=_params("arbitrary", "arbitrary"),
        name="fourier_seq",
    )(seq_dft, uv)


OUT_TM = 512


def _outproj_kernel(o_ref, f_ref, w_ref, x_ref, g_ref, out_ref):
    y = jnp.dot(o_ref[...], w_ref[:DIFF_QK, :], preferred_element_type=F32)
    y = y + jnp.dot(f_ref[...], w_ref[DIFF_QK:, :], preferred_element_type=F32)
    out_ref[...] = x_ref[...] + g_ref[...] * y


def _outproj(o, fm, w, x2d, g1):
    rows = x2d.shape[0]
    per_b = SEQ // OUT_TM
    return pl.pallas_call(
        _outproj_kernel,
        grid=(rows // OUT_TM,),
        in_specs=[
            pl.BlockSpec((OUT_TM, DIFF_QK), lambda i: (i, 0)),
            pl.BlockSpec((OUT_TM, FOURIER_W), lambda i: (i, 0)),
            pl.BlockSpec((D_MODEL, D_MODEL), lambda i: (0, 0)),
            pl.BlockSpec((OUT_TM, D_MODEL), lambda i: (i, 0)),
            pl.BlockSpec((None, 1, D_MODEL), lambda i: (i // per_b, 0, 0)),
        ],
        out_specs=pl.BlockSpec((OUT_TM, D_MODEL), lambda i: (i, 0)),
        out_shape=jax.ShapeDtypeStruct((rows, D_MODEL), F32),
        compiler_params=_params("arbitrary"),
        name="outproj",
    )(o, fm, w, x2d, g1)


SGU_TM = 256


def _sgu_kernel(x_ref, g_ref, sc_ref, sh_ref, win_ref, bin_ref, ng_ref, ws_ref, bs_ref, wout_ref, g1_ref,
                out_ref, us_ref):
    x = x_ref[...]
    h = _rms(x, g_ref[...]) * (1.0 + sc_ref[...]) + sh_ref[...]
    z = _gelu(jnp.dot(h.astype(BF16), win_ref[...], preferred_element_type=F32) + bin_ref[...])
    u = z[:, :SGU_W]
    v = _rms(z[:, SGU_W:], ng_ref[...]).astype(BF16)
    for ck in range(SGU_TM // CHUNK):
        rows = slice(ck * CHUNK, (ck + 1) * CHUNK)
        for g in range(SGU_GROUPS):
            cols = slice(g * CHUNK, (g + 1) * CHUNK)
            s = jnp.dot(ws_ref[g], v[rows, cols], preferred_element_type=F32) + bs_ref[g]
            us_ref[rows, cols] = (u[rows, cols] * s).astype(BF16)
    y = jnp.dot(us_ref[...], wout_ref[...], preferred_element_type=F32)
    out_ref[...] = x + g1_ref[...] * y


def _sgu(x2d, g, sc, sh, win, b_in, ng, ws, bs, wout, g1):
    rows = x2d.shape[0]
    per_b = SEQ // SGU_TM
    vec = lambda n: pl.BlockSpec((1, n), lambda i: (0, 0))
    per_batch = pl.BlockSpec((None, 1, D_MODEL), lambda i: (i // per_b, 0, 0))
    return pl.pallas_call(
        _sgu_kernel,
        grid=(rows // SGU_TM,),
        in_specs=[
            pl.BlockSpec((SGU_TM, D_MODEL), lambda i: (i, 0)),
            vec(D_MODEL), per_batch, per_batch,
            pl.BlockSpec((D_MODEL, 2 * SGU_W), lambda i: (0, 0)),
            vec(2 * SGU_W), vec(SGU_W),
            pl.BlockSpec((SGU_GROUPS, CHUNK, CHUNK), lambda i: (0, 0, 0)),
            pl.BlockSpec((SGU_GROUPS, CHUNK, CHUNK), lambda i: (0, 0, 0)),
            pl.BlockSpec((SGU_W, D_MODEL), lambda i: (0, 0)),
            per_batch,
        ],
        out_specs=pl.BlockSpec((SGU_TM, D_MODEL), lambda i: (i, 0)),
        out_shape=jax.ShapeDtypeStruct((rows, D_MODEL), F32),
        scratch_shapes=[pltpu.VMEM((SGU_TM, SGU_W), BF16)],
        compiler_params=_params("arbitrary"),
        name="sgu",
    )(x2d, g, sc, sh, win, b_in, ng, ws, bs, wout, g1)


ROUTE_TM = 512
PAIR_CELLS = [(a, b) for a in range(PEER_TOPK + 1) for b in range(PEER_TOPK + 1)
              if (a + 1) * (b + 1) <= PEER_TOPK + 1]


def _sort16_network():
    pairs = []

    def merge(lo, n, r):
        step = 2 * r
        if step < n:
            merge(lo, n, step)
            merge(lo + r, n, step)
            pairs.extend((i, i + r) for i in range(lo + r, lo + n - r, step))
        else:
            pairs.append((lo, lo + r))

    def sort(lo, n):
        if n > 1:
            sort(lo, n // 2)
            sort(lo + n // 2, n // 2)
            merge(lo, n, 1)

    sort(0, 16)
    return pairs


SORT16 = _sort16_network()


def _top_sorted(st):
    sub = 8
    groups = [st[g * sub:(g + 1) * sub, :] for g in range(N_KEYS // sub)]
    for i, j in SORT16:
        groups[i], groups[j] = jnp.maximum(groups[i], groups[j]), jnp.minimum(groups[i], groups[j])
    floor = jnp.full_like(groups[0], -jnp.inf)
    vals = []
    for r in range(PEER_TOPK + 1):
        m = jnp.max(groups[0], axis=0, keepdims=True)
        vals.append(m)
        if r < PEER_TOPK:
            pop = groups[0] == m
            for d in range(PEER_TOPK - r):
                below = groups[d + 1] if d + 1 < len(groups) else floor
                groups[d] = jnp.where(pop, below, groups[d])
    return vals


def _count_above(vals, theta):
    def pivot(conds, lo, step):
        if not conds:
            return vals[lo + step - 1]
        return jnp.where(conds[0], pivot(conds[1:], lo + 2 * step * 2 ** (len(conds) - 1), step),
                         pivot(conds[1:], lo, step))

    conds, count = [], None
    for step in (8, 4, 2, 1):
        hit = pivot(conds, 0, step) > theta
        count = jnp.where(hit, float(step), 0.0) if count is None else jnp.where(hit, count + float(step), count)
        conds.append(hit)
    return jnp.where(vals[-1] > theta, count + 1.0, count)


def _route_kernel(x_ref, g_ref, sc_ref, sh_ref, wq_ref, keys_ref,
                  t_ref, cnt_ref, a_ref, r2_ref, b_ref, s_scr):
    x = x_ref[...]
    tf = _rms(x, g_ref[...]) * (1.0 + sc_ref[...]) + sh_ref[...]
    t_ref[...] = tf.T.astype(BF16)
    q = jnp.dot(tf.astype(BF16), wq_ref[...], preferred_element_type=F32)
    neg = jnp.float32(-jnp.inf)
    tops = []
    for hs in range(2 * PEER_HEADS):
        qh = q[:, hs * PEER_HALF:(hs + 1) * PEER_HALF].astype(BF16)
        st = lax.dot_general(keys_ref[hs], qh, NT_DIMS, preferred_element_type=F32)
        s_scr[hs] = st
        vals = _top_sorted(st)
        if hs % 2 == 1:
            r2_ref[hs // 2] = _count_above(vals[:PEER_TOPK], st).astype(BF16)
        tops.append(vals)
    v1 = [jnp.concatenate([tops[2 * h][r] for h in range(PEER_HEADS)], axis=0) for r in range(PEER_TOPK + 1)]
    v2 = [jnp.concatenate([tops[2 * h + 1][r] for h in range(PEER_HEADS)], axis=0) for r in range(PEER_TOPK + 1)]
    cand = [v1[a] + v2[b] for a, b in PAIR_CELLS]
    cur = cand
    kth = None
    for r in range(PEER_TOPK + 1):
        m = functools.reduce(jnp.maximum, cur)
        if r == PEER_TOPK - 1:
            kth = m
        if r < PEER_TOPK:
            cur = [jnp.where(c == m, neg, c) for c in cur]
    thr = 0.5 * (kth + m)
    top = v1[0] + v2[0]
    z = functools.reduce(lambda p, c: p + jnp.where(c >= thr, jnp.exp(c - top), 0.0), cand, jnp.zeros_like(top))
    rz = 1.0 / z
    for h in range(PEER_HEADS):
        s1 = s_scr[2 * h]
        theta = thr[h:h + 1] - s1
        cnt = _count_above([vb[h:h + 1] for vb in v2[:PEER_TOPK]], theta)
        gate = jnp.exp(s1 - v1[0][h:h + 1]) * (0.5 * rz[h:h + 1])
        for c in range(ROUTE_TM // LANES):
            cnt_ref[h, c] = cnt[:, c * LANES:(c + 1) * LANES]
            a_ref[h, c] = gate[:, c * LANES:(c + 1) * LANES]
        b_ref[h] = jnp.exp(s_scr[2 * h + 1] - v2[0][h:h + 1]).astype(BF16)


def _route(x2d, g, sc, sh, wq, keys):
    rows = x2d.shape[0]
    per_b = SEQ // ROUTE_TM
    per_batch = pl.BlockSpec((None, 1, D_MODEL), lambda i: (i // per_b, 0, 0))
    tbl = pl.BlockSpec((PEER_HEADS, N_KEYS, ROUTE_TM), lambda i: (0, 0, i))
    row_tbl = jax.ShapeDtypeStruct((PEER_HEADS, rows // LANES, N_KEYS, LANES), F32)
    row_spec = pl.BlockSpec((PEER_HEADS, ROUTE_TM // LANES, N_KEYS, LANES), lambda i: (0, i, 0, 0))
    col_tbl = jax.ShapeDtypeStruct((PEER_HEADS, N_KEYS, rows), BF16)
    return pl.pallas_call(
        _route_kernel,
        grid=(rows // ROUTE_TM,),
        in_specs=[
            pl.BlockSpec((ROUTE_TM, D_MODEL), lambda i: (i, 0)),
            pl.BlockSpec((1, D_MODEL), lambda i: (0, 0)),
            per_batch, per_batch,
            pl.BlockSpec((D_MODEL, 2 * PEER_HEADS * PEER_HALF), lambda i: (0, 0)),
            pl.BlockSpec((2 * PEER_HEADS, N_KEYS, PEER_HALF), lambda i: (0, 0, 0)),
        ],
        out_specs=[pl.BlockSpec((D_MODEL, ROUTE_TM), lambda i: (0, i)), row_spec, row_spec, tbl, tbl],
        out_shape=[jax.ShapeDtypeStruct((D_MODEL, rows), BF16), row_tbl, row_tbl, col_tbl, col_tbl],
        scratch_shapes=[pltpu.VMEM((2 * PEER_HEADS, N_KEYS, ROUTE_TM), F32)],
        compiler_params=_params("arbitrary"),
        name="peer_route",
    )(x2d, g, sc, sh, wq, keys)


DENSE_TM = 512
DENSE_TE = 2048
DENSE_SUB = 512
DENSE_SHARE = 2
DENSE_JC = 32


def _dense_kernel(t_ref, cnt_ref, a_ref, r2_ref, b_ref, eu_ref, evt_ref, x_ref, g2_ref, fg_ref,
                  out_ref, acc_ref, act_ref, g_ref, rt_ref, *, final_norm):
    e = pl.program_id(1)

    @pl.when(e == 0)
    def _():
        acc_ref[...] = jnp.zeros_like(acc_ref)

    tm = t_ref.shape[1]
    zero = jnp.zeros((), BF16)

    def row_tile(ref, h, i):
        return jnp.concatenate([ref[h, c, pl.ds(i, 16, stride=0), :] for c in range(tm // LANES)],
                               axis=1).astype(BF16)

    def tiled(ref, slot, row, k, h):
        return jnp.concatenate([ref[slot, row, k, h]] * (DENSE_JC // 16), axis=0)

    rows_per_sub = DENSE_SUB // N_KEYS
    n_sub = DENSE_TE // DENSE_SUB

    def activations(sb, slot):
        act_ref[slot] = jnp.dot(eu_ref[sb * DENSE_SUB:(sb + 1) * DENSE_SUB, :], t_ref[...],
                                preferred_element_type=F32)

    def gating(sb, slot):
        for ii in range(rows_per_sub):
            i = e * (DENSE_TE // N_KEYS) + sb * rows_per_sub + ii
            for h in range(PEER_HEADS):
                rt_ref[slot, ii, 0, h] = row_tile(cnt_ref, h, i)
                rt_ref[slot, ii, 1, h] = row_tile(a_ref, h, i)
        for pair in range(0, rows_per_sub, DENSE_SHARE):
            for jc in range(N_KEYS // DENSE_JC):
                js = slice(jc * DENSE_JC, (jc + 1) * DENSE_JC)
                w = [None] * DENSE_SHARE
                for h in range(PEER_HEADS):
                    r2 = r2_ref[h, js, :]
                    bb = b_ref[h, js, :]
                    for s in range(DENSE_SHARE):
                        term = jnp.where(r2 < tiled(rt_ref, slot, pair + s, 0, h),
                                         bb * tiled(rt_ref, slot, pair + s, 1, h), zero)
                        w[s] = term if w[s] is None else w[s] + term
                for s in range(DENSE_SHARE):
                    ii = pair + s
                    rows = slice(ii * N_KEYS + jc * DENSE_JC, ii * N_KEYS + (jc + 1) * DENSE_JC)
                    g_ref[slot, rows, :] = _gelu_packed_x2(act_ref[slot, rows, :].astype(BF16)) * w[s]

    def accumulate(sb, slot):
        acc_ref[...] += jnp.dot(evt_ref[sb], g_ref[slot], preferred_element_type=F32)

    for sb in range(n_sub):
        activations(sb, sb)
    for sb in range(n_sub):
        gating(sb, sb)
        accumulate(sb, sb)

    @pl.when(e == pl.num_programs(1) - 1)
    def _():
        y = x_ref[...] + g2_ref[...] * acc_ref[...].T
        if final_norm:
            y = _rms(y, fg_ref[...])
        out_ref[...] = y


def _dense(t, cnt, a, r2, b, eu, evt, x2d, g2, fg, layer, final_norm):
    rows = x2d.shape[0]
    per_b = SEQ // DENSE_TM
    tbl = pl.BlockSpec((PEER_HEADS, N_KEYS, DENSE_TM), lambda i, e: (0, 0, i))
    row_spec = pl.BlockSpec((PEER_HEADS, DENSE_TM // LANES, N_KEYS, LANES), lambda i, e: (0, i, 0, 0))
    return pl.pallas_call(
        functools.partial(_dense_kernel, final_norm=final_norm),
        grid=(rows // DENSE_TM, N_EXPERTS // DENSE_TE),
        in_specs=[
            pl.BlockSpec((D_MODEL, DENSE_TM), lambda i, e: (0, i)),
            row_spec, row_spec, tbl, tbl,
            pl.BlockSpec((None, DENSE_TE, D_MODEL), lambda i, e: (layer, e, 0)),
            pl.BlockSpec((None, DENSE_TE // DENSE_SUB, D_MODEL, DENSE_SUB), lambda i, e: (layer, e, 0, 0)),
            pl.BlockSpec((DENSE_TM, D_MODEL), lambda i, e: (i, 0)),
            pl.BlockSpec((None, 1, D_MODEL), lambda i, e: (i // per_b, 0, 0)),
            pl.BlockSpec((1, D_MODEL), lambda i, e: (0, 0)),
        ],
        out_specs=pl.BlockSpec((DENSE_TM, D_MODEL), lambda i, e: (i, 0)),
        out_shape=jax.ShapeDtypeStruct((rows, D_MODEL), F32),
        scratch_shapes=[pltpu.VMEM((D_MODEL, DENSE_TM), F32),
                        pltpu.VMEM((DENSE_TE // DENSE_SUB, DENSE_SUB, DENSE_TM), F32),
                        pltpu.VMEM((DENSE_TE // DENSE_SUB, DENSE_SUB, DENSE_TM), BF16),
                        pltpu.VMEM((DENSE_TE // DENSE_SUB, DENSE_SUB // N_KEYS, 2, PEER_HEADS, 16, DENSE_TM), BF16)],
        compiler_params=_params("arbitrary", "arbitrary"),
        name="peer_dense",
    )(t, cnt, a, r2, b, eu, evt, x2d, g2, fg)


def _peer(x2d, g, sc, sh, g2, wq, keys, eu_all, evt_all, fg, layer, final_norm):
    wq_b = wq.astype(BF16)
    keys_b = keys.reshape(2 * PEER_HEADS, N_KEYS, PEER_HALF).astype(BF16)
    t, cnt, a, r2, b = _route(x2d, g, sc, sh, wq_b, keys_b)
    return _dense(t, cnt, a, r2, b, eu_all, evt_all, x2d, g2, fg, layer, final_norm)


def kernel(x, c, ctx, c_ctx, ada_w, ada_b, norm1_g, norm2_g, final_g, even_w_in, even_w_out, diff_lambda,
           diff_norm_g, odd_w_in, odd_b_in, sgu_norm_g, sgu_w, sgu_b, odd_w_out, peer_wq, peer_keys, peer_u, peer_v):
    cvec = jnp.concatenate([c, c_ctx[None, :], jnp.zeros((8 - BATCH - 1, D_MODEL), F32)], axis=0)
    mod = _ada(cvec, ada_w, ada_b).reshape(2, 8, 6, D_MODEL)
    row = lambda v: v.reshape(1, -1)
    per_batch = lambda l, j: mod[l, :BATCH, j].reshape(BATCH, 1, D_MODEL)
    fg = row(final_g)

    both = lambda j: jnp.stack([mod[0, :BATCH, j], jnp.broadcast_to(mod[0, BATCH, j], (BATCH, D_MODEL))],
                               axis=1).reshape(BATCH, 2, 1, D_MODEL)
    xin = jnp.concatenate([x, ctx], axis=1).reshape(BATCH * SEQ_ALL, D_MODEL)
    cos, sa, sb = _rope_tables()
    qkvf = _inproj(xin, row(norm1_g[0]), both(1), both(0), even_w_in[0].astype(BF16), cos, sa, sb)
    qkvf = qkvf.reshape(BATCH, SEQ_ALL, EVEN_IN)
    o = _attn(qkvf, diff_lambda[0], row(diff_norm_g[0]))
    fm = _fourier(qkvf, _seq_dft())
    eu_all = peer_u.astype(BF16)
    evt_all = peer_v.astype(BF16).reshape(2, N_EXPERTS // DENSE_SUB, DENSE_SUB, D_MODEL).transpose(0, 1, 3, 2)
    x2d = x.reshape(BATCH * SEQ, D_MODEL)
    x2d = _outproj(o.reshape(BATCH * SEQ, DIFF_QK), fm.reshape(BATCH * SEQ, FOURIER_W),
                   even_w_out[0].astype(BF16), x2d, per_batch(0, 2))
    x2d = _peer(x2d, row(norm2_g[0]), per_batch(0, 4), per_batch(0, 3), per_batch(0, 5),
                peer_wq[0], peer_keys[0], eu_all, evt_all, fg, 0, False)

    bs = jnp.broadcast_to(sgu_b[0][:, :, None], (SGU_GROUPS, CHUNK, CHUNK))
    x2d = _sgu(x2d, row(norm1_g[1]), per_batch(1, 1), per_batch(1, 0), odd_w_in[0].astype(BF16),
               row(odd_b_in[0]), row(sgu_norm_g[0]), sgu_w[0].astype(BF16), bs, odd_w_out[0].astype(BF16),
               per_batch(1, 2))
    x2d = _peer(x2d, row(norm2_g[1]), per_batch(1, 4), per_batch(1, 3), per_batch(1, 5),
                peer_wq[1], peer_keys[1], eu_all, evt_all, fg, 1, True)
    return x2d.reshape(BATCH, SEQ, D_MODEL)
```

```python
import functools
import math

import numpy as np
import jax
import jax.numpy as jnp
from jax import lax
from jax.experimental import pallas as pl
from jax.experimental.pallas import tpu as pltpu

F32 = jnp.float32
BF16 = jnp.bfloat16

D_MODEL = 1024
BATCH = 4
SEQ = 4096
GRID_W = 64
CTX_LEN = 256
SEQ_ALL = CTX_LEN + SEQ
EPS = 1e-6

DIFF_HEADS = 6
DIFF_DH = 64
HEAD_W = 2 * DIFF_DH
DIFF_QK = DIFF_HEADS * HEAD_W
FOURIER_GROUPS = 4
FOURIER_GC = 64
FOURIER_W = FOURIER_GROUPS * FOURIER_GC
EVEN_IN = 3 * DIFF_QK + FOURIER_W
ROPE_BASE = 10000.0
ROPE_PAIRS = DIFF_DH // 4
LAM_INIT_L1 = 0.8 - 0.6 * math.exp(-0.3 * 0.0)
LOG2_E = math.log2(math.e)

CHUNK = 128
SGU_GROUPS = 8
SGU_W = SGU_GROUPS * CHUNK

PEER_HEADS = 8
N_KEYS = 128
N_EXPERTS = N_KEYS * N_KEYS
PEER_TOPK = 16
PEER_HALF = 128
LANES = 128

VMEM_LIMIT = 56 * 1024 * 1024

NT_DIMS = (((1,), (1,)), ((), ()))


def _rms(t, g):
    return t * lax.rsqrt(jnp.mean(t * t, axis=-1, keepdims=True) + EPS) * g


def _gelu(x):
    cdf = 0.5 * (1.0 + jnp.tanh(math.sqrt(2.0 / math.pi) * (x + 0.044715 * (x * x * x))))
    return x * cdf


def _gelu_packed_x2(x):
    c = math.sqrt(2.0 / math.pi)
    return x + x * jnp.tanh(x * (c + (c * 0.044715) * (x * x)))


def _params(*sem, flags=None):
    return pltpu.CompilerParams(dimension_semantics=sem, vmem_limit_bytes=VMEM_LIMIT, flags=flags)


def _ada_kernel(c_ref, w_ref, b_ref, o_ref):
    c = c_ref[...]
    s = c * jax.nn.sigmoid(c)
    o_ref[...] = jnp.dot(s, w_ref[...], preferred_element_type=F32) + b_ref[...]


def _ada(cvec, ada_w, ada_b):
    depth = ada_w.shape[0]
    tn = 1536
    return pl.pallas_call(
        _ada_kernel,
        grid=(depth, 6 * D_MODEL // tn),
        in_specs=[
            pl.BlockSpec((8, D_MODEL), lambda l, j: (0, 0)),
            pl.BlockSpec((None, D_MODEL, tn), lambda l, j: (l, 0, j)),
            pl.BlockSpec((None, 1, tn), lambda l, j: (l, 0, j)),
        ],
        out_specs=pl.BlockSpec((None, 8, tn), lambda l, j: (l, 0, j)),
        out_shape=jax.ShapeDtypeStruct((depth, 8, 6 * D_MODEL), F32),
        compiler_params=_params("arbitrary", "arbitrary"),
        name="ada",
    )(cvec, ada_w, ada_b.reshape(depth, 1, 6 * D_MODEL))


IN_TM = 256
TILES_PER_SEQ = SEQ_ALL // IN_TM
assert CTX_LEN == IN_TM


def _inproj_kernel(x_ref, ctx_ref, g_ref, sc_ref, sh_ref, w_ref, cos_ref, sa_ref, sb_ref, o_ref):
    is_ctx = pl.program_id(0) % TILES_PER_SEQ == TILES_PER_SEQ - 1
    x = jnp.where(is_ctx, ctx_ref[...], x_ref[...])
    h = _rms(x, g_ref[...]) * (1.0 + sc_ref[...]) + sh_ref[...]
    r = jnp.dot(h.astype(BF16), w_ref[...], preferred_element_type=F32)
    cos, sa, sb = cos_ref[...], sa_ref[...], sb_ref[...]
    for cb in range(2 * DIFF_HEADS):
        t = r[:, cb * HEAD_W:(cb + 1) * HEAD_W]
        rot = t * cos + pltpu.roll(t, HEAD_W - 16, 1) * sa + pltpu.roll(t, 16, 1) * sb
        if cb < DIFF_HEADS:
            rot = rot * (DIFF_DH ** -0.5 * LOG2_E)
        o_ref[:, cb * HEAD_W:(cb + 1) * HEAD_W] = rot.astype(BF16)
    o_ref[:, 2 * DIFF_QK:] = r[:, 2 * DIFF_QK:].astype(BF16)


def _inproj(x2d, ctx2d, g, sc, sh, w, cos, sa, sb):
    rows = x2d.shape[0] + ctx2d.shape[0]
    lat_tiles = SEQ // IN_TM

    def lat_index(i):
        return (i // TILES_PER_SEQ) * lat_tiles + jnp.minimum(i % TILES_PER_SEQ, lat_tiles - 1)

    mod_spec = pl.BlockSpec((None, None, 1, D_MODEL),
                            lambda i: (i // TILES_PER_SEQ, jnp.where(i % TILES_PER_SEQ == TILES_PER_SEQ - 1, 1, 0), 0, 0))
    rope_spec = pl.BlockSpec((IN_TM, HEAD_W), lambda i: (i % TILES_PER_SEQ, 0))
    return pl.pallas_call(
        _inproj_kernel,
        grid=(rows // IN_TM,),
        in_specs=[
            pl.BlockSpec((IN_TM, D_MODEL), lambda i: (lat_index(i), 0)),
            pl.BlockSpec((CTX_LEN, D_MODEL), lambda i: (i // TILES_PER_SEQ, 0)),
            pl.BlockSpec((1, D_MODEL), lambda i: (0, 0)),
            mod_spec, mod_spec,
            pl.BlockSpec((D_MODEL, EVEN_IN), lambda i: (0, 0)),
            rope_spec, rope_spec, rope_spec,
        ],
        out_specs=pl.BlockSpec((IN_TM, EVEN_IN), lambda i: (i, 0)),
        out_shape=jax.ShapeDtypeStruct((rows, EVEN_IN), BF16),
        compiler_params=_params("arbitrary"),
        name="inproj",
    )(x2d, ctx2d, g, sc, sh, w, cos, sa, sb)


def _rope_tables():
    rows = SEQ // GRID_W
    r = jnp.repeat(jnp.arange(rows, dtype=F32), GRID_W)
    col = jnp.tile(jnp.arange(GRID_W, dtype=F32), rows)
    inv = ROPE_BASE ** (-jnp.arange(ROPE_PAIRS, dtype=F32) / ROPE_PAIRS)
    ar = r[:, None] * inv
    ac = col[:, None] * inv
    ang = jnp.concatenate([ar, ar, ac, ac] * 2, axis=-1)
    cos, sin = jnp.cos(ang), jnp.sin(ang)
    even = ((np.arange(HEAD_W) // ROPE_PAIRS) % 2 == 0)[None, :]
    sa = jnp.where(even, -sin, 0.0)
    sb = jnp.where(even, 0.0, sin)
    ident = jnp.ones((CTX_LEN, HEAD_W), F32)
    zero = jnp.zeros((CTX_LEN, HEAD_W), F32)
    return (jnp.concatenate([cos, ident], 0), jnp.concatenate([sa, zero], 0),
            jnp.concatenate([sb, zero], 0))


ATT_TQ = 2048
ATT_CHAINS = 8
ATT_KC = 256


def _attn_kernel(lam_ref, g_ref, q_ref, k_ref, v_ref, o_ref, vt_ref, st_ref, e_ref):
    @pl.when(pl.program_id(2) == 0)
    def _():
        vt_ref[:HEAD_W, :] = v_ref[...].astype(F32).T.astype(BF16)
        ones_row = lax.broadcasted_iota(jnp.int32, (16, SEQ_ALL), 0) == 0
        vt_ref[HEAD_W:, :] = jnp.where(ones_row, 1.0, 0.0).astype(BF16)

    lane = lax.broadcasted_iota(jnp.int32, (1, HEAD_W), 1)
    lp = lam_ref[...]
    lam = (jnp.exp(jnp.sum(lp[0:1] * lp[1:2], axis=-1, keepdims=True))
           - jnp.exp(jnp.sum(lp[2:3] * lp[3:4], axis=-1, keepdims=True)) + LAM_INIT_L1)
    tq = ATT_TQ // ATT_CHAINS
    n_kc = SEQ_ALL // ATT_KC

    def scores(c):
        q = q_ref[c * tq:(c + 1) * tq, :].astype(F32)
        qq = jnp.concatenate([jnp.where(lane < DIFF_DH, q, 0.0), jnp.where(lane >= DIFF_DH, q, 0.0)], axis=0)
        st_ref[c % 2] = lax.dot_general(k_ref[...], qq.astype(BF16), NT_DIMS, preferred_element_type=F32)

    def softmax(c):
        slot = c % 2
        m = functools.reduce(jnp.maximum, [jnp.max(st_ref[slot, kc * ATT_KC:(kc + 1) * ATT_KC, :], axis=0,
                                                    keepdims=True) for kc in range(n_kc)])
        for kc in range(n_kc):
            ks = slice(kc * ATT_KC, (kc + 1) * ATT_KC)
            e_ref[slot, ks, :] = jnp.exp2(st_ref[slot, ks, :] - m).astype(BF16)

    def values(c):
        ext = jnp.dot(vt_ref[...], e_ref[c % 2], preferred_element_type=F32)
        ot = ext[:HEAD_W] / ext[HEAD_W:HEAD_W + 1]
        o = (ot[:, :tq] - lam * ot[:, tq:]).T
        o_ref[c * tq:(c + 1) * tq, :] = (_rms(o, g_ref[...]) * (1.0 - LAM_INIT_L1)).astype(BF16)

    scores(0)
    for c in range(ATT_CHAINS):
        if c + 1 < ATT_CHAINS:
            scores(c + 1)
        softmax(c)
        if c > 0:
            values(c - 1)
    values(ATT_CHAINS - 1)


def _attn(qkvf, lam_p, head_g):
    nq = SEQ // ATT_TQ
    return pl.pallas_call(
        _attn_kernel,
        grid=(BATCH, DIFF_HEADS, nq),
        in_specs=[
            pl.BlockSpec((4, DIFF_DH), lambda b, h, i: (0, 0)),
            pl.BlockSpec((1, HEAD_W), lambda b, h, i: (0, 0)),
            pl.BlockSpec((None, ATT_TQ, HEAD_W), lambda b, h, i: (b, i, h)),
            pl.BlockSpec((None, SEQ_ALL, HEAD_W), lambda b, h, i: (b, 0, DIFF_HEADS + h)),
            pl.BlockSpec((None, SEQ_ALL, HEAD_W), lambda b, h, i: (b, 0, 2 * DIFF_HEADS + h)),
        ],
        out_specs=pl.BlockSpec((None, ATT_TQ, HEAD_W), lambda b, h, i: (b, i, h)),
        out_shape=jax.ShapeDtypeStruct((BATCH, SEQ, DIFF_QK), BF16),
        scratch_shapes=[pltpu.VMEM((HEAD_W + 16, SEQ_ALL), BF16),
                        pltpu.VMEM((2, SEQ_ALL, 2 * ATT_TQ // ATT_CHAINS), F32),
                        pltpu.VMEM((2, SEQ_ALL, 2 * ATT_TQ // ATT_CHAINS), BF16)],
        compiler_params=_params("arbitrary", "arbitrary", "arbitrary"),
        name="diff_attn",
    )(lam_p, head_g, qkvf, qkvf, qkvf)


def _channel_dft():
    n = np.arange(FOURIER_GC)
    ang = 2.0 * np.pi * np.outer(n, n) / FOURIER_GC
    eye = np.eye(FOURIER_GROUPS)
    scale = FOURIER_GC ** -0.5
    return np.concatenate([np.kron(eye, np.cos(ang)), np.kron(eye, np.sin(ang))], axis=1) * scale


def _seq_dft():
    k = jnp.arange(SEQ, dtype=jnp.int32)[:, None]
    n = jnp.arange(GRID_W, dtype=jnp.int32)[None, :]
    a = ((k * n) % GRID_W).astype(F32) * (2.0 * math.pi / GRID_W)
    b = ((k * n) % SEQ).astype(F32) * (2.0 * math.pi / SEQ)
    ca, sa, cb, sb = jnp.cos(a)[:, :, None], jnp.sin(a)[:, :, None], jnp.cos(b)[:, None, :], jnp.sin(b)[:, None, :]
    scale = SEQ ** -0.5
    cos = ((ca * cb - sa * sb) * scale).reshape(SEQ, SEQ)
    sin = ((sa * cb + ca * sb) * scale).reshape(SEQ, SEQ)
    return jnp.concatenate([cos, -sin], axis=1).astype(BF16)


def _fourier_chan_kernel(f_ref, bd_ref, o_ref):
    f = f_ref[:SEQ, :]
    uv = jnp.dot(f, bd_ref[...], preferred_element_type=F32)
    o_ref[:SEQ, :] = uv[:, :FOURIER_W].astype(BF16)
    o_ref[SEQ:, :] = uv[:, FOURIER_W:].astype(BF16)


def _fourier_seq_kernel(t_ref, uv_ref, o_ref):
    o_ref[...] = jnp.dot(t_ref[...], uv_ref[...], preferred_element_type=F32).astype(BF16)


FOURIER_TM = 512


def _fourier(qkvf, seq_dft):
    bd = jnp.asarray(_channel_dft(), dtype=BF16)
    uv = pl.pallas_call(
        _fourier_chan_kernel,
        grid=(BATCH,),
        in_specs=[
            pl.BlockSpec((None, SEQ_ALL, FOURIER_W), lambda b: (b, 0, 3 * DIFF_QK // FOURIER_W)),
            pl.BlockSpec((FOURIER_W, 2 * FOURIER_W), lambda b: (0, 0)),
        ],
        out_specs=pl.BlockSpec((None, 2 * SEQ, FOURIER_W), lambda b: (b, 0, 0)),
        out_shape=jax.ShapeDtypeStruct((BATCH, 2 * SEQ, FOURIER_W), BF16),
        compiler_params=_params("arbitrary"),
        name="fourier_chan",
    )(qkvf, bd)
    return pl.pallas_call(
        _fourier_seq_kernel,
        grid=(SEQ // FOURIER_TM, BATCH),
        in_specs=[
            pl.BlockSpec((FOURIER_TM, 2 * SEQ), lambda i, b: (i, 0)),
            pl.BlockSpec((None, 2 * SEQ, FOURIER_W), lambda i, b: (b, 0, 0)),
        ],
        out_specs=pl.BlockSpec((None, FOURIER_TM, FOURIER_W), lambda i, b: (b, i, 0)),
        out_shape=jax.ShapeDtypeStruct((BATCH, SEQ, FOURIER_W), BF16),
        compiler_params=_params("arbitrary", "arbitrary"),
        name="fourier_seq",
    )(seq_dft, uv)


OUT_TM = 512


def _outproj_kernel(o_ref, f_ref, w_ref, x_ref, g_ref, out_ref):
    y = jnp.dot(o_ref[...], w_ref[:DIFF_QK, :], preferred_element_type=F32)
    y = y + jnp.dot(f_ref[...], w_ref[DIFF_QK:, :], preferred_element_type=F32)
    out_ref[...] = x_ref[...] + g_ref[...] * y


def _outproj(o, fm, w, x2d, g1):
    rows = x2d.shape[0]
    per_b = SEQ // OUT_TM
    return pl.pallas_call(
        _outproj_kernel,
        grid=(rows // OUT_TM,),
        in_specs=[
            pl.BlockSpec((OUT_TM, DIFF_QK), lambda i: (i, 0)),
            pl.BlockSpec((OUT_TM, FOURIER_W), lambda i: (i, 0)),
            pl.BlockSpec((D_MODEL, D_MODEL), lambda i: (0, 0)),
            pl.BlockSpec((OUT_TM, D_MODEL), lambda i: (i, 0)),
            pl.BlockSpec((None, 1, D_MODEL), lambda i: (i // per_b, 0, 0)),
        ],
        out_specs=pl.BlockSpec((OUT_TM, D_MODEL), lambda i: (i, 0)),
        out_shape=jax.ShapeDtypeStruct((rows, D_MODEL), F32),
        compiler_params=_params("arbitrary"),
        name="outproj",
    )(o, fm, w, x2d, g1)


SGU_TM = 256


def _sgu_kernel(x_ref, g_ref, sc_ref, sh_ref, win_ref, bin_ref, ng_ref, ws_ref, bs_ref, wout_ref, g1_ref,
                out_ref, us_ref):
    x = x_ref[...]
    h = _rms(x, g_ref[...]) * (1.0 + sc_ref[...]) + sh_ref[...]
    z = _gelu(jnp.dot(h.astype(BF16), win_ref[...], preferred_element_type=F32) + bin_ref[...])
    u = z[:, :SGU_W]
    v = _rms(z[:, SGU_W:], ng_ref[...]).astype(BF16)
    for ck in range(SGU_TM // CHUNK):
        rows = slice(ck * CHUNK, (ck + 1) * CHUNK)
        for g in range(SGU_GROUPS):
            cols = slice(g * CHUNK, (g + 1) * CHUNK)
            s = jnp.dot(ws_ref[g], v[rows, cols], preferred_element_type=F32) + bs_ref[g]
            us_ref[rows, cols] = (u[rows, cols] * s).astype(BF16)
    y = jnp.dot(us_ref[...], wout_ref[...], preferred_element_type=F32)
    out_ref[...] = x + g1_ref[...] * y


def _sgu(x2d, g, sc, sh, win, b_in, ng, ws, bs, wout, g1):
    rows = x2d.shape[0]
    per_b = SEQ // SGU_TM
    vec = lambda n: pl.BlockSpec((1, n), lambda i: (0, 0))
    per_batch = pl.BlockSpec((None, 1, D_MODEL), lambda i: (i // per_b, 0, 0))
    return pl.pallas_call(
        _sgu_kernel,
        grid=(rows // SGU_TM,),
        in_specs=[
            pl.BlockSpec((SGU_TM, D_MODEL), lambda i: (i, 0)),
            vec(D_MODEL), per_batch, per_batch,
            pl.BlockSpec((D_MODEL, 2 * SGU_W), lambda i: (0, 0)),
            vec(2 * SGU_W), vec(SGU_W),
            pl.BlockSpec((SGU_GROUPS, CHUNK, CHUNK), lambda i: (0, 0, 0)),
            pl.BlockSpec((SGU_GROUPS, CHUNK, CHUNK), lambda i: (0, 0, 0)),
            pl.BlockSpec((SGU_W, D_MODEL), lambda i: (0, 0)),
            per_batch,
        ],
        out_specs=pl.BlockSpec((SGU_TM, D_MODEL), lambda i: (i, 0)),
        out_shape=jax.ShapeDtypeStruct((rows, D_MODEL), F32),
        scratch_shapes=[pltpu.VMEM((SGU_TM, SGU_W), BF16)],
        compiler_params=_params("arbitrary"),
        name="sgu",
    )(x2d, g, sc, sh, win, b_in, ng, ws, bs, wout, g1)


ROUTE_TM = 512
PAIR_CELLS = [(a, b) for a in range(PEER_TOPK + 1) for b in range(PEER_TOPK + 1)
              if (a + 1) * (b + 1) <= PEER_TOPK + 1]


def _sort16_network():
    pairs = []

    def merge(lo, n, r):
        step = 2 * r
        if step < n:
            merge(lo, n, step)
            merge(lo + r, n, step)
            pairs.extend((i, i + r) for i in range(lo + r, lo + n - r, step))
        else:
            pairs.append((lo, lo + r))

    def sort(lo, n):
        if n > 1:
            sort(lo, n // 2)
            sort(lo + n // 2, n // 2)
            merge(lo, n, 1)

    sort(0, 16)
    return pairs


SORT16 = _sort16_network()


def _top_sorted(st):
    sub = 8
    groups = [st[g * sub:(g + 1) * sub, :] for g in range(N_KEYS // sub)]
    for i, j in SORT16:
        groups[i], groups[j] = jnp.maximum(groups[i], groups[j]), jnp.minimum(groups[i], groups[j])
    floor = jnp.full_like(groups[0], -jnp.inf)
    vals = []
    for r in range(PEER_TOPK + 1):
        m = jnp.max(groups[0], axis=0, keepdims=True)
        vals.append(m)
        if r < PEER_TOPK:
            pop = groups[0] == m
            for d in range(PEER_TOPK - r):
                below = groups[d + 1] if d + 1 < len(groups) else floor
                groups[d] = jnp.where(pop, below, groups[d])
    return vals


def _count_above(vals, theta):
    def pivot(conds, lo, step):
        if not conds:
            return vals[lo + step - 1]
        return jnp.where(conds[0], pivot(conds[1:], lo + 2 * step * 2 ** (len(conds) - 1), step),
                         pivot(conds[1:], lo, step))

    conds, count = [], None
    for step in (8, 4, 2, 1):
        hit = pivot(conds, 0, step) > theta
        count = jnp.where(hit, float(step), 0.0) if count is None else jnp.where(hit, count + float(step), count)
        conds.append(hit)
    return jnp.where(vals[-1] > theta, count + 1.0, count)


def _route_kernel(x_ref, g_ref, sc_ref, sh_ref, wq_ref, keys_ref,
                  t_ref, cnt_ref, a_ref, r2_ref, b_ref, s_scr):
    x = x_ref[...]
    tf = _rms(x, g_ref[...]) * (1.0 + sc_ref[...]) + sh_ref[...]
    t_ref[...] = tf.T.astype(BF16)
    q = jnp.dot(tf.astype(BF16), wq_ref[...], preferred_element_type=F32)
    neg = jnp.float32(-jnp.inf)
    tops = []
    for hs in range(2 * PEER_HEADS):
        qh = q[:, hs * PEER_HALF:(hs + 1) * PEER_HALF].astype(BF16)
        st = lax.dot_general(keys_ref[hs], qh, NT_DIMS, preferred_element_type=F32)
        s_scr[hs] = st
        vals = _top_sorted(st)
        if hs % 2 == 1:
            r2_ref[hs // 2] = _count_above(vals[:PEER_TOPK], st).astype(BF16)
        tops.append(vals)
    v1 = [jnp.concatenate([tops[2 * h][r] for h in range(PEER_HEADS)], axis=0) for r in range(PEER_TOPK + 1)]
    v2 = [jnp.concatenate([tops[2 * h + 1][r] for h in range(PEER_HEADS)], axis=0) for r in range(PEER_TOPK + 1)]
    cand = [v1[a] + v2[b] for a, b in PAIR_CELLS]
    cur = cand
    kth = None
    for r in range(PEER_TOPK + 1):
        m = functools.reduce(jnp.maximum, cur)
        if r == PEER_TOPK - 1:
            kth = m
        if r < PEER_TOPK:
            cur = [jnp.where(c == m, neg, c) for c in cur]
    thr = 0.5 * (kth + m)
    top = v1[0] + v2[0]
    z = functools.reduce(lambda p, c: p + jnp.where(c >= thr, jnp.exp(c - top), 0.0), cand, jnp.zeros_like(top))
    rz = 1.0 / z
    for h in range(PEER_HEADS):
        s1 = s_scr[2 * h]
        theta = thr[h:h + 1] - s1
        cnt = _count_above([vb[h:h + 1] for vb in v2[:PEER_TOPK]], theta)
        gate = jnp.exp(s1 - v1[0][h:h + 1]) * (0.5 * rz[h:h + 1])
        for c in range(ROUTE_TM // LANES):
            cnt_ref[h, c] = cnt[:, c * LANES:(c + 1) * LANES]
            a_ref[h, c] = gate[:, c * LANES:(c + 1) * LANES]
        b_ref[h] = jnp.exp(s_scr[2 * h + 1] - v2[0][h:h + 1]).astype(BF16)


def _route(x2d, g, sc, sh, wq, keys):
    rows = x2d.shape[0]
    per_b = SEQ // ROUTE_TM
    per_batch = pl.BlockSpec((None, 1, D_MODEL), lambda i: (i // per_b, 0, 0))
    tbl = pl.BlockSpec((PEER_HEADS, N_KEYS, ROUTE_TM), lambda i: (0, 0, i))
    row_tbl = jax.ShapeDtypeStruct((PEER_HEADS, rows // LANES, N_KEYS, LANES), F32)
    row_spec = pl.BlockSpec((PEER_HEADS, ROUTE_TM // LANES, N_KEYS, LANES), lambda i: (0, i, 0, 0))
    col_tbl = jax.ShapeDtypeStruct((PEER_HEADS, N_KEYS, rows), BF16)
    return pl.pallas_call(
        _route_kernel,
        grid=(rows // ROUTE_TM,),
        in_specs=[
            pl.BlockSpec((ROUTE_TM, D_MODEL), lambda i: (i, 0)),
            pl.BlockSpec((1, D_MODEL), lambda i: (0, 0)),
            per_batch, per_batch,
            pl.BlockSpec((D_MODEL, 2 * PEER_HEADS * PEER_HALF), lambda i: (0, 0)),
            pl.BlockSpec((2 * PEER_HEADS, N_KEYS, PEER_HALF), lambda i: (0, 0, 0)),
        ],
        out_specs=[pl.BlockSpec((D_MODEL, ROUTE_TM), lambda i: (0, i)), row_spec, row_spec, tbl, tbl],
        out_shape=[jax.ShapeDtypeStruct((D_MODEL, rows), BF16), row_tbl, row_tbl, col_tbl, col_tbl],
        scratch_shapes=[pltpu.VMEM((2 * PEER_HEADS, N_KEYS, ROUTE_TM), F32)],
        compiler_params=_params("arbitrary"),
        name="peer_route",
    )(x2d, g, sc, sh, wq, keys)


DENSE_TM = 512
DENSE_TE = 2048
DENSE_SUB = 512
DENSE_SHARE = 2
DENSE_JC = 32


def _dense_kernel(t_ref, cnt_ref, a_ref, r2_ref, b_ref, eu_ref, evt_ref, x_ref, g2_ref, fg_ref,
                  out_ref, acc_ref, act_ref, g_ref, rt_ref, *, final_norm):
    e = pl.program_id(1)

    @pl.when(e == 0)
    def _():
        acc_ref[...] = jnp.zeros_like(acc_ref)

    tm = t_ref.shape[1]
    zero = jnp.zeros((), BF16)

    def row_tile(ref, h, i):
        return jnp.concatenate([ref[h, c, pl.ds(i, 16, stride=0), :] for c in range(tm // LANES)],
                               axis=1).astype(BF16)

    def tiled(ref, slot, row, k, h):
        return jnp.concatenate([ref[slot, row, k, h]] * (DENSE_JC // 16), axis=0)

    rows_per_sub = DENSE_SUB // N_KEYS
    n_sub = DENSE_TE // DENSE_SUB

    def activations(sb, slot):
        act_ref[slot] = jnp.dot(eu_ref[sb * DENSE_SUB:(sb + 1) * DENSE_SUB, :], t_ref[...],
                                preferred_element_type=F32)

    def gating(sb, slot):
        for ii in range(rows_per_sub):
            i = e * (DENSE_TE // N_KEYS) + sb * rows_per_sub + ii
            for h in range(PEER_HEADS):
                rt_ref[slot, ii, 0, h] = row_tile(cnt_ref, h, i)
                rt_ref[slot, ii, 1, h] = row_tile(a_ref, h, i)
        for pair in range(0, rows_per_sub, DENSE_SHARE):
            for jc in range(N_KEYS // DENSE_JC):
                js = slice(jc * DENSE_JC, (jc + 1) * DENSE_JC)
                w = [None] * DENSE_SHARE
                for h in range(PEER_HEADS):
                    r2 = r2_ref[h, js, :]
                    bb = b_ref[h, js, :]
                    for s in range(DENSE_SHARE):
                        term = jnp.where(r2 < tiled(rt_ref, slot, pair + s, 0, h),
                                         bb * tiled(rt_ref, slot, pair + s, 1, h), zero)
                        w[s] = term if w[s] is None else w[s] + term
                for s in range(DENSE_SHARE):
                    ii = pair + s
                    rows = slice(ii * N_KEYS + jc * DENSE_JC, ii * N_KEYS + (jc + 1) * DENSE_JC)
                    g_ref[slot, rows, :] = _gelu_packed_x2(act_ref[slot, rows, :].astype(BF16)) * w[s]

    def accumulate(sb, slot):
        acc_ref[...] += jnp.dot(evt_ref[sb], g_ref[slot], preferred_element_type=F32)

    for sb in range(n_sub):
        activations(sb, sb)
    for sb in range(n_sub):
        gating(sb, sb)
        accumulate(sb, sb)

    @pl.when(e == pl.num_programs(1) - 1)
    def _():
        y = x_ref[...] + g2_ref[...] * acc_ref[...].T
        if final_norm:
            y = _rms(y, fg_ref[...])
        out_ref[...] = y


def _dense(t, cnt, a, r2, b, eu, evt, x2d, g2, fg, layer, final_norm):
    rows = x2d.shape[0]
    per_b = SEQ // DENSE_TM
    tbl = pl.BlockSpec((PEER_HEADS, N_KEYS, DENSE_TM), lambda i, e: (0, 0, i))
    row_spec = pl.BlockSpec((PEER_HEADS, DENSE_TM // LANES, N_KEYS, LANES), lambda i, e: (0, i, 0, 0))
    return pl.pallas_call(
        functools.partial(_dense_kernel, final_norm=final_norm),
        grid=(rows // DENSE_TM, N_EXPERTS // DENSE_TE),
        in_specs=[
            pl.BlockSpec((D_MODEL, DENSE_TM), lambda i, e: (0, i)),
            row_spec, row_spec, tbl, tbl,
            pl.BlockSpec((None, DENSE_TE, D_MODEL), lambda i, e: (layer, e, 0)),
            pl.BlockSpec((None, DENSE_TE // DENSE_SUB, D_MODEL, DENSE_SUB), lambda i, e: (layer, e, 0, 0)),
            pl.BlockSpec((DENSE_TM, D_MODEL), lambda i, e: (i, 0)),
            pl.BlockSpec((None, 1, D_MODEL), lambda i, e: (i // per_b, 0, 0)),
            pl.BlockSpec((1, D_MODEL), lambda i, e: (0, 0)),
        ],
        out_specs=pl.BlockSpec((DENSE_TM, D_MODEL), lambda i, e: (i, 0)),
        out_shape=jax.ShapeDtypeStruct((rows, D_MODEL), F32),
        scratch_shapes=[pltpu.VMEM((D_MODEL, DENSE_TM), F32),
                        pltpu.VMEM((DENSE_TE // DENSE_SUB, DENSE_SUB, DENSE_TM), F32),
                        pltpu.VMEM((DENSE_TE // DENSE_SUB, DENSE_SUB, DENSE_TM), BF16),
                        pltpu.VMEM((DENSE_TE // DENSE_SUB, DENSE_SUB // N_KEYS, 2, PEER_HEADS, 16, DENSE_TM), BF16)],
        compiler_params=_params("arbitrary", "arbitrary"),
        name="peer_dense",
    )(t, cnt, a, r2, b, eu, evt, x2d, g2, fg)


def _peer(x2d, g, sc, sh, g2, wq, keys, eu_all, evt_all, fg, layer, final_norm):
    wq_b = wq.astype(BF16)
    keys_b = keys.reshape(2 * PEER_HEADS, N_KEYS, PEER_HALF).astype(BF16)
    t, cnt, a, r2, b = _route(x2d, g, sc, sh, wq_b, keys_b)
    return _dense(t, cnt, a, r2, b, eu_all, evt_all, x2d, g2, fg, layer, final_norm)


def kernel(x, c, ctx, c_ctx, ada_w, ada_b, norm1_g, norm2_g, final_g, even_w_in, even_w_out, diff_lambda,
           diff_norm_g, odd_w_in, odd_b_in, sgu_norm_g, sgu_w, sgu_b, odd_w_out, peer_wq, peer_keys, peer_u, peer_v):
    cvec = jnp.concatenate([c, c_ctx[None, :], jnp.zeros((8 - BATCH - 1, D_MODEL), F32)], axis=0)
    mod = _ada(cvec, ada_w, ada_b).reshape(2, 8, 6, D_MODEL)
    row = lambda v: v.reshape(1, -1)
    per_batch = lambda l, j: mod[l, :BATCH, j].reshape(BATCH, 1, D_MODEL)
    fg = row(final_g)

    both = lambda j: jnp.stack([mod[0, :BATCH, j], jnp.broadcast_to(mod[0, BATCH, j], (BATCH, D_MODEL))],
                               axis=1).reshape(BATCH, 2, 1, D_MODEL)
    cos, sa, sb = _rope_tables()
    qkvf = _inproj(x.reshape(BATCH * SEQ, D_MODEL), ctx.reshape(BATCH * CTX_LEN, D_MODEL), row(norm1_g[0]),
                   both(1), both(0), even_w_in[0].astype(BF16), cos, sa, sb)
    qkvf = qkvf.reshape(BATCH, SEQ_ALL, EVEN_IN)
    o = _attn(qkvf, diff_lambda[0], row(diff_norm_g[0]))
    fm = _fourier(qkvf, _seq_dft())
    eu_all = peer_u.astype(BF16)
    evt_all = peer_v.astype(BF16).reshape(2, N_EXPERTS // DENSE_SUB, DENSE_SUB, D_MODEL).transpose(0, 1, 3, 2)
    x2d = x.reshape(BATCH * SEQ, D_MODEL)
    x2d = _outproj(o.reshape(BATCH * SEQ, DIFF_QK), fm.reshape(BATCH * SEQ, FOURIER_W),
                   even_w_out[0].astype(BF16), x2d, per_batch(0, 2))
    x2d = _peer(x2d, row(norm2_g[0]), per_batch(0, 4), per_batch(0, 3), per_batch(0, 5),
                peer_wq[0], peer_keys[0], eu_all, evt_all, fg, 0, False)

    bs = jnp.broadcast_to(sgu_b[0][:, :, None], (SGU_GROUPS, CHUNK, CHUNK))
    x2d = _sgu(x2d, row(norm1_g[1]), per_batch(1, 1), per_batch(1, 0), odd_w_in[0].astype(BF16),
               row(odd_b_in[0]), row(sgu_norm_g[0]), sgu_w[0].astype(BF16), bs, odd_w_out[0].astype(BF16),
               per_batch(1, 2))
    x2d = _peer(x2d, row(norm2_g[1]), per_batch(1, 4), per_batch(1, 3), per_batch(1, 5),
                peer_wq[1], peer_keys[1], eu_all, evt_all, fg, 1, True)
    return x2d.reshape(BATCH, SEQ, D_MODEL)
```

```python
import functools
import math

import numpy as np
import jax
import jax.numpy as jnp
from jax import lax
from jax.experimental import pallas as pl
from jax.experimental.pallas import tpu as pltpu

F32 = jnp.float32
BF16 = jnp.bfloat16

D_MODEL = 1024
BATCH = 4
SEQ = 4096
GRID_W = 64
CTX_LEN = 256
SEQ_ALL = CTX_LEN + SEQ
EPS = 1e-6

DIFF_HEADS = 6
DIFF_DH = 64
HEAD_W = 2 * DIFF_DH
DIFF_QK = DIFF_HEADS * HEAD_W
FOURIER_GROUPS = 4
FOURIER_GC = 64
FOURIER_W = FOURIER_GROUPS * FOURIER_GC
EVEN_IN = 3 * DIFF_QK + FOURIER_W
ROPE_BASE = 10000.0
ROPE_PAIRS = DIFF_DH // 4
LAM_INIT_L1 = 0.8 - 0.6 * math.exp(-0.3 * 0.0)
LOG2_E = math.log2(math.e)

CHUNK = 128
SGU_GROUPS = 8
SGU_W = SGU_GROUPS * CHUNK

PEER_HEADS = 8
N_KEYS = 128
N_EXPERTS = N_KEYS * N_KEYS
PEER_TOPK = 16
PEER_HALF = 128
LANES = 128

VMEM_LIMIT = 56 * 1024 * 1024

NT_DIMS = (((1,), (1,)), ((), ()))


def _rms(t, g):
    return t * lax.rsqrt(jnp.mean(t * t, axis=-1, keepdims=True) + EPS) * g


def _gelu(x):
    cdf = 0.5 * (1.0 + jnp.tanh(math.sqrt(2.0 / math.pi) * (x + 0.044715 * (x * x * x))))
    return x * cdf


def _gelu_packed_x2(x):
    c = math.sqrt(2.0 / math.pi)
    return x + x * jnp.tanh(x * (c + (c * 0.044715) * (x * x)))


def _params(*sem, flags=None):
    return pltpu.CompilerParams(dimension_semantics=sem, vmem_limit_bytes=VMEM_LIMIT, flags=flags)


def _ada_kernel(c_ref, w_ref, b_ref, o_ref):
    c = c_ref[...]
    s = c * jax.nn.sigmoid(c)
    o_ref[...] = jnp.dot(s, w_ref[...], preferred_element_type=F32) + b_ref[...]


def _ada(cvec, ada_w, ada_b):
    depth = ada_w.shape[0]
    tn = 1536
    return pl.pallas_call(
        _ada_kernel,
        grid=(depth, 6 * D_MODEL // tn),
        in_specs=[
            pl.BlockSpec((8, D_MODEL), lambda l, j: (0, 0)),
            pl.BlockSpec((None, D_MODEL, tn), lambda l, j: (l, 0, j)),
            pl.BlockSpec((None, 1, tn), lambda l, j: (l, 0, j)),
        ],
        out_specs=pl.BlockSpec((None, 8, tn), lambda l, j: (l, 0, j)),
        out_shape=jax.ShapeDtypeStruct((depth, 8, 6 * D_MODEL), F32),
        compiler_params=_params("arbitrary", "arbitrary"),
        name="ada",
    )(cvec, ada_w, ada_b.reshape(depth, 1, 6 * D_MODEL))


IN_TM = 256
TILES_PER_SEQ = SEQ_ALL // IN_TM
assert CTX_LEN == IN_TM


def _inproj_kernel(x_ref, ctx_ref, g_ref, sc_ref, sh_ref, w_ref, cos_ref, sa_ref, sb_ref, o_ref):
    is_ctx = pl.program_id(0) % TILES_PER_SEQ == TILES_PER_SEQ - 1
    x = jnp.where(is_ctx, ctx_ref[...], x_ref[...])
    h = _rms(x, g_ref[...]) * (1.0 + sc_ref[...]) + sh_ref[...]
    r = jnp.dot(h.astype(BF16), w_ref[...], preferred_element_type=F32)
    cos, sa, sb = cos_ref[...], sa_ref[...], sb_ref[...]
    for cb in range(2 * DIFF_HEADS):
        t = r[:, cb * HEAD_W:(cb + 1) * HEAD_W]
        rot = t * cos + pltpu.roll(t, HEAD_W - 16, 1) * sa + pltpu.roll(t, 16, 1) * sb
        if cb < DIFF_HEADS:
            rot = rot * (DIFF_DH ** -0.5 * LOG2_E)
        o_ref[:, cb * HEAD_W:(cb + 1) * HEAD_W] = rot.astype(BF16)
    o_ref[:, 2 * DIFF_QK:] = r[:, 2 * DIFF_QK:].astype(BF16)


def _inproj(x2d, ctx2d, g, sc, sh, w, cos, sa, sb):
    rows = x2d.shape[0] + ctx2d.shape[0]
    lat_tiles = SEQ // IN_TM

    def lat_index(i):
        return (i // TILES_PER_SEQ) * lat_tiles + jnp.minimum(i % TILES_PER_SEQ, lat_tiles - 1)

    mod_spec = pl.BlockSpec((None, None, 1, D_MODEL),
                            lambda i: (i // TILES_PER_SEQ, jnp.where(i % TILES_PER_SEQ == TILES_PER_SEQ - 1, 1, 0), 0, 0))
    rope_spec = pl.BlockSpec((IN_TM, HEAD_W), lambda i: (i % TILES_PER_SEQ, 0))
    return pl.pallas_call(
        _inproj_kernel,
        grid=(rows // IN_TM,),
        in_specs=[
            pl.BlockSpec((IN_TM, D_MODEL), lambda i: (lat_index(i), 0)),
            pl.BlockSpec((CTX_LEN, D_MODEL), lambda i: (i // TILES_PER_SEQ, 0)),
            pl.BlockSpec((1, D_MODEL), lambda i: (0, 0)),
            mod_spec, mod_spec,
            pl.BlockSpec((D_MODEL, EVEN_IN), lambda i: (0, 0)),
            rope_spec, rope_spec, rope_spec,
        ],
        out_specs=pl.BlockSpec((IN_TM, EVEN_IN), lambda i: (i, 0)),
        out_shape=jax.ShapeDtypeStruct((rows, EVEN_IN), BF16),
        compiler_params=_params("arbitrary"),
        name="inproj",
    )(x2d, ctx2d, g, sc, sh, w, cos, sa, sb)


def _rope_tables():
    rows = SEQ // GRID_W
    r = jnp.repeat(jnp.arange(rows, dtype=F32), GRID_W)
    col = jnp.tile(jnp.arange(GRID_W, dtype=F32), rows)
    inv = ROPE_BASE ** (-jnp.arange(ROPE_PAIRS, dtype=F32) / ROPE_PAIRS)
    ar = r[:, None] * inv
    ac = col[:, None] * inv
    ang = jnp.concatenate([ar, ar, ac, ac] * 2, axis=-1)
    cos, sin = jnp.cos(ang), jnp.sin(ang)
    even = ((np.arange(HEAD_W) // ROPE_PAIRS) % 2 == 0)[None, :]
    sa = jnp.where(even, -sin, 0.0)
    sb = jnp.where(even, 0.0, sin)
    ident = jnp.ones((CTX_LEN, HEAD_W), F32)
    zero = jnp.zeros((CTX_LEN, HEAD_W), F32)
    return (jnp.concatenate([cos, ident], 0), jnp.concatenate([sa, zero], 0),
            jnp.concatenate([sb, zero], 0))


ATT_TQ = 2048
ATT_CHAINS = 8
ATT_KC = 256


def _attn_kernel(lam_ref, g_ref, q_ref, k_ref, v_ref, o_ref, vt_ref, st_ref, e_ref):
    @pl.when(pl.program_id(2) == 0)
    def _():
        vt_ref[:HEAD_W, :] = v_ref[...].astype(F32).T.astype(BF16)
        ones_row = lax.broadcasted_iota(jnp.int32, (16, SEQ_ALL), 0) == 0
        vt_ref[HEAD_W:, :] = jnp.where(ones_row, 1.0, 0.0).astype(BF16)

    lane = lax.broadcasted_iota(jnp.int32, (1, HEAD_W), 1)
    lp = lam_ref[...]
    lam = (jnp.exp(jnp.sum(lp[0:1] * lp[1:2], axis=-1, keepdims=True))
           - jnp.exp(jnp.sum(lp[2:3] * lp[3:4], axis=-1, keepdims=True)) + LAM_INIT_L1)
    tq = ATT_TQ // ATT_CHAINS
    n_kc = SEQ_ALL // ATT_KC

    def scores(c):
        q = q_ref[c * tq:(c + 1) * tq, :].astype(F32)
        qq = jnp.concatenate([jnp.where(lane < DIFF_DH, q, 0.0), jnp.where(lane >= DIFF_DH, q, 0.0)], axis=0)
        st_ref[c % 2] = lax.dot_general(k_ref[...], qq.astype(BF16), NT_DIMS, preferred_element_type=F32)

    def softmax(c):
        slot = c % 2
        m = functools.reduce(jnp.maximum, [jnp.max(st_ref[slot, kc * ATT_KC:(kc + 1) * ATT_KC, :], axis=0,
                                                    keepdims=True) for kc in range(n_kc)])
        for kc in range(n_kc):
            ks = slice(kc * ATT_KC, (kc + 1) * ATT_KC)
            e_ref[slot, ks, :] = jnp.exp2(st_ref[slot, ks, :] - m).astype(BF16)

    def values(c):
        ext = jnp.dot(vt_ref[...], e_ref[c % 2], preferred_element_type=F32)
        ot = ext[:HEAD_W] / ext[HEAD_W:HEAD_W + 1]
        o = (ot[:, :tq] - lam * ot[:, tq:]).T
        o_ref[c * tq:(c + 1) * tq, :] = (_rms(o, g_ref[...]) * (1.0 - LAM_INIT_L1)).astype(BF16)

    scores(0)
    for c in range(ATT_CHAINS):
        if c + 1 < ATT_CHAINS:
            scores(c + 1)
        softmax(c)
        if c > 0:
            values(c - 1)
    values(ATT_CHAINS - 1)


def _attn(qkvf, lam_p, head_g):
    nq = SEQ // ATT_TQ
    return pl.pallas_call(
        _attn_kernel,
        grid=(BATCH, DIFF_HEADS, nq),
        in_specs=[
            pl.BlockSpec((4, DIFF_DH), lambda b, h, i: (0, 0)),
            pl.BlockSpec((1, HEAD_W), lambda b, h, i: (0, 0)),
            pl.BlockSpec((None, ATT_TQ, HEAD_W), lambda b, h, i: (b, i, h)),
            pl.BlockSpec((None, SEQ_ALL, HEAD_W), lambda b, h, i: (b, 0, DIFF_HEADS + h)),
            pl.BlockSpec((None, SEQ_ALL, HEAD_W), lambda b, h, i: (b, 0, 2 * DIFF_HEADS + h)),
        ],
        out_specs=pl.BlockSpec((None, ATT_TQ, HEAD_W), lambda b, h, i: (b, i, h)),
        out_shape=jax.ShapeDtypeStruct((BATCH, SEQ, DIFF_QK), BF16),
        scratch_shapes=[pltpu.VMEM((HEAD_W + 16, SEQ_ALL), BF16),
                        pltpu.VMEM((2, SEQ_ALL, 2 * ATT_TQ // ATT_CHAINS), F32),
                        pltpu.VMEM((2, SEQ_ALL, 2 * ATT_TQ // ATT_CHAINS), BF16)],
        compiler_params=_params("arbitrary", "arbitrary", "arbitrary"),
        name="diff_attn",
    )(lam_p, head_g, qkvf, qkvf, qkvf)


def _channel_dft():
    n = np.arange(FOURIER_GC)
    ang = 2.0 * np.pi * np.outer(n, n) / FOURIER_GC
    eye = np.eye(FOURIER_GROUPS)
    scale = FOURIER_GC ** -0.5
    return np.concatenate([np.kron(eye, np.cos(ang)), np.kron(eye, np.sin(ang))], axis=1) * scale


def _seq_dft():
    k = jnp.arange(SEQ, dtype=jnp.int32)[:, None]
    n = jnp.arange(GRID_W, dtype=jnp.int32)[None, :]
    a = ((k * n) % GRID_W).astype(F32) * (2.0 * math.pi / GRID_W)
    b = ((k * n) % SEQ).astype(F32) * (2.0 * math.pi / SEQ)
    ca, sa, cb, sb = jnp.cos(a)[:, :, None], jnp.sin(a)[:, :, None], jnp.cos(b)[:, None, :], jnp.sin(b)[:, None, :]
    scale = SEQ ** -0.5
    cos = ((ca * cb - sa * sb) * scale).reshape(SEQ, SEQ)
    sin = ((sa * cb + ca * sb) * scale).reshape(SEQ, SEQ)
    return jnp.concatenate([cos, -sin], axis=1).astype(BF16)


def _fourier_chan_kernel(f_ref, bd_ref, o_ref):
    f = f_ref[:SEQ, :]
    uv = jnp.dot(f, bd_ref[...], preferred_element_type=F32)
    o_ref[:SEQ, :] = uv[:, :FOURIER_W].astype(BF16)
    o_ref[SEQ:, :] = uv[:, FOURIER_W:].astype(BF16)


def _fourier_seq_kernel(t_ref, uv_ref, o_ref):
    o_ref[...] = jnp.dot(t_ref[...], uv_ref[...], preferred_element_type=F32).astype(BF16)


FOURIER_TM = 512


def _fourier(qkvf, seq_dft):
    bd = jnp.asarray(_channel_dft(), dtype=BF16)
    uv = pl.pallas_call(
        _fourier_chan_kernel,
        grid=(BATCH,),
        in_specs=[
            pl.BlockSpec((None, SEQ_ALL, FOURIER_W), lambda b: (b, 0, 3 * DIFF_QK // FOURIER_W)),
            pl.BlockSpec((FOURIER_W, 2 * FOURIER_W), lambda b: (0, 0)),
        ],
        out_specs=pl.BlockSpec((None, 2 * SEQ, FOURIER_W), lambda b: (b, 0, 0)),
        out_shape=jax.ShapeDtypeStruct((BATCH, 2 * SEQ, FOURIER_W), BF16),
        compiler_params=_params("arbitrary"),
        name="fourier_chan",
    )(qkvf, bd)
    return pl.pallas_call(
        _fourier_seq_kernel,
        grid=(SEQ // FOURIER_TM, BATCH),
        in_specs=[
            pl.BlockSpec((FOURIER_TM, 2 * SEQ), lambda i, b: (i, 0)),
            pl.BlockSpec((None, 2 * SEQ, FOURIER_W), lambda i, b: (b, 0, 0)),
        ],
        out_specs=pl.BlockSpec((None, FOURIER_TM, FOURIER_W), lambda i, b: (b, i, 0)),
        out_shape=jax.ShapeDtypeStruct((BATCH, SEQ, FOURIER_W), BF16),
        compiler_params=_params("arbitrary", "arbitrary"),
        name="fourier_seq",
    )(seq_dft, uv)


OUT_TM = 512


def _outproj_kernel(o_ref, f_ref, w_ref, x_ref, g_ref, out_ref):
    y = jnp.dot(o_ref[...], w_ref[:DIFF_QK, :], preferred_element_type=F32)
    y = y + jnp.dot(f_ref[...], w_ref[DIFF_QK:, :], preferred_element_type=F32)
    out_ref[...] = x_ref[...] + g_ref[...] * y


def _outproj(o, fm, w, x2d, g1):
    rows = x2d.shape[0]
    per_b = SEQ // OUT_TM
    return pl.pallas_call(
        _outproj_kernel,
        grid=(rows // OUT_TM,),
        in_specs=[
            pl.BlockSpec((OUT_TM, DIFF_QK), lambda i: (i, 0)),
            pl.BlockSpec((OUT_TM, FOURIER_W), lambda i: (i, 0)),
            pl.BlockSpec((D_MODEL, D_MODEL), lambda i: (0, 0)),
            pl.BlockSpec((OUT_TM, D_MODEL), lambda i: (i, 0)),
            pl.BlockSpec((None, 1, D_MODEL), lambda i: (i // per_b, 0, 0)),
        ],
        out_specs=pl.BlockSpec((OUT_TM, D_MODEL), lambda i: (i, 0)),
        out_shape=jax.ShapeDtypeStruct((rows, D_MODEL), F32),
        compiler_params=_params("arbitrary"),
        name="outproj",
    )(o, fm, w, x2d, g1)


SGU_TM = 256


def _sgu_kernel(x_ref, g_ref, sc_ref, sh_ref, win_ref, bin_ref, ng_ref, ws_ref, bs_ref, wout_ref, g1_ref,
                out_ref, us_ref):
    x = x_ref[...]
    h = _rms(x, g_ref[...]) * (1.0 + sc_ref[...]) + sh_ref[...]
    z = _gelu(jnp.dot(h.astype(BF16), win_ref[...], preferred_element_type=F32) + bin_ref[...])
    u = z[:, :SGU_W]
    v = _rms(z[:, SGU_W:], ng_ref[...]).astype(BF16)
    for ck in range(SGU_TM // CHUNK):
        rows = slice(ck * CHUNK, (ck + 1) * CHUNK)
        for g in range(SGU_GROUPS):
            cols = slice(g * CHUNK, (g + 1) * CHUNK)
            s = jnp.dot(ws_ref[g], v[rows, cols], preferred_element_type=F32) + bs_ref[g]
            us_ref[rows, cols] = (u[rows, cols] * s).astype(BF16)
    y = jnp.dot(us_ref[...], wout_ref[...], preferred_element_type=F32)
    out_ref[...] = x + g1_ref[...] * y


def _sgu(x2d, g, sc, sh, win, b_in, ng, ws, bs, wout, g1):
    rows = x2d.shape[0]
    per_b = SEQ // SGU_TM
    vec = lambda n: pl.BlockSpec((1, n), lambda i: (0, 0))
    per_batch = pl.BlockSpec((None, 1, D_MODEL), lambda i: (i // per_b, 0, 0))
    return pl.pallas_call(
        _sgu_kernel,
        grid=(rows // SGU_TM,),
        in_specs=[
            pl.BlockSpec((SGU_TM, D_MODEL), lambda i: (i, 0)),
            vec(D_MODEL), per_batch, per_batch,
            pl.BlockSpec((D_MODEL, 2 * SGU_W), lambda i: (0, 0)),
            vec(2 * SGU_W), vec(SGU_W),
            pl.BlockSpec((SGU_GROUPS, CHUNK, CHUNK), lambda i: (0, 0, 0)),
            pl.BlockSpec((SGU_GROUPS, CHUNK, CHUNK), lambda i: (0, 0, 0)),
            pl.BlockSpec((SGU_W, D_MODEL), lambda i: (0, 0)),
            per_batch,
        ],
        out_specs=pl.BlockSpec((SGU_TM, D_MODEL), lambda i: (i, 0)),
        out_shape=jax.ShapeDtypeStruct((rows, D_MODEL), F32),
        scratch_shapes=[pltpu.VMEM((SGU_TM, SGU_W), BF16)],
        compiler_params=_params("arbitrary"),
        name="sgu",
    )(x2d, g, sc, sh, win, b_in, ng, ws, bs, wout, g1)


ROUTE_TM = 512
PAIR_CELLS = [(a, b) for a in range(PEER_TOPK + 1) for b in range(PEER_TOPK + 1)
              if (a + 1) * (b + 1) <= PEER_TOPK + 1]
PAIR_STREAMS = ([[ab for ab in PAIR_CELLS if ab[0] == a] for a in range(4)]
                + [[ab for ab in PAIR_CELLS if ab[0] >= 4 and ab[1] == b] for b in range(3)])
assert sorted(ab for s in PAIR_STREAMS for ab in s) == sorted(PAIR_CELLS)


def _sort16_network():
    pairs = []

    def merge(lo, n, r):
        step = 2 * r
        if step < n:
            merge(lo, n, step)
            merge(lo + r, n, step)
            pairs.extend((i, i + r) for i in range(lo + r, lo + n - r, step))
        else:
            pairs.append((lo, lo + r))

    def sort(lo, n):
        if n > 1:
            sort(lo, n // 2)
            sort(lo + n // 2, n // 2)
            merge(lo, n, 1)

    sort(0, 16)
    return pairs


SORT16 = _sort16_network()


def _top_sorted(st):
    sub = 8
    groups = [st[g * sub:(g + 1) * sub, :] for g in range(N_KEYS // sub)]
    for i, j in SORT16:
        groups[i], groups[j] = jnp.maximum(groups[i], groups[j]), jnp.minimum(groups[i], groups[j])
    floor = jnp.full_like(groups[0], -jnp.inf)
    vals = []
    for r in range(PEER_TOPK + 1):
        m = jnp.max(groups[0], axis=0, keepdims=True)
        vals.append(m)
        if r < PEER_TOPK:
            pop = groups[0] == m
            for d in range(PEER_TOPK - r):
                below = groups[d + 1] if d + 1 < len(groups) else floor
                groups[d] = jnp.where(pop, below, groups[d])
    return vals


def _count_above(vals, theta):
    def pivot(conds, lo, step):
        if not conds:
            return vals[lo + step - 1]
        return jnp.where(conds[0], pivot(conds[1:], lo + 2 * step * 2 ** (len(conds) - 1), step),
                         pivot(conds[1:], lo, step))

    conds, count = [], None
    for step in (8, 4, 2, 1):
        hit = pivot(conds, 0, step) > theta
        count = jnp.where(hit, float(step), 0.0) if count is None else jnp.where(hit, count + float(step), count)
        conds.append(hit)
    return jnp.where(vals[-1] > theta, count + 1.0, count)


def _route_kernel(x_ref, g_ref, sc_ref, sh_ref, wq_ref, keys_ref,
                  t_ref, cnt_ref, a_ref, r2_ref, b_ref, s_scr):
    x = x_ref[...]
    tf = _rms(x, g_ref[...]) * (1.0 + sc_ref[...]) + sh_ref[...]
    t_ref[...] = tf.T.astype(BF16)
    q = jnp.dot(tf.astype(BF16), wq_ref[...], preferred_element_type=F32)
    neg = jnp.float32(-jnp.inf)
    tops = []
    for hs in range(2 * PEER_HEADS):
        qh = q[:, hs * PEER_HALF:(hs + 1) * PEER_HALF].astype(BF16)
        st = lax.dot_general(keys_ref[hs], qh, NT_DIMS, preferred_element_type=F32)
        s_scr[hs] = st
        vals = _top_sorted(st)
        if hs % 2 == 1:
            r2_ref[hs // 2] = _count_above(vals[:PEER_TOPK], st).astype(BF16)
        tops.append(vals)
    v1 = [jnp.concatenate([tops[2 * h][r] for h in range(PEER_HEADS)], axis=0) for r in range(PEER_TOPK + 1)]
    v2 = [jnp.concatenate([tops[2 * h + 1][r] for h in range(PEER_HEADS)], axis=0) for r in range(PEER_TOPK + 1)]
    cell = {ab: v1[ab[0]] + v2[ab[1]] for ab in PAIR_CELLS}
    cand = list(cell.values())
    streams = [[cell[ab] for ab in s] for s in PAIR_STREAMS]
    floor = jnp.full_like(cand[0], neg)
    kth = None
    for r in range(PEER_TOPK + 1):
        m = functools.reduce(jnp.maximum, [s[0] for s in streams])
        if r == PEER_TOPK - 1:
            kth = m
        if r < PEER_TOPK:
            for s in streams:
                pop = s[0] == m
                for d in range(min(len(s), PEER_TOPK - r)):
                    s[d] = jnp.where(pop, s[d + 1] if d + 1 < len(s) else floor, s[d])
    thr = 0.5 * (kth + m)
    top = v1[0] + v2[0]
    z = functools.reduce(lambda p, c: p + jnp.where(c >= thr, jnp.exp(c - top), 0.0), cand, jnp.zeros_like(top))
    rz = 1.0 / z
    for h in range(PEER_HEADS):
        s1 = s_scr[2 * h]
        theta = thr[h:h + 1] - s1
        cnt = _count_above([vb[h:h + 1] for vb in v2[:PEER_TOPK]], theta)
        gate = jnp.exp(s1 - v1[0][h:h + 1]) * (0.5 * rz[h:h + 1])
        for c in range(ROUTE_TM // LANES):
            cnt_ref[h, c] = cnt[:, c * LANES:(c + 1) * LANES]
            a_ref[h, c] = gate[:, c * LANES:(c + 1) * LANES]
        b_ref[h] = jnp.exp(s_scr[2 * h + 1] - v2[0][h:h + 1]).astype(BF16)


def _route(x2d, g, sc, sh, wq, keys):
    rows = x2d.shape[0]
    per_b = SEQ // ROUTE_TM
    per_batch = pl.BlockSpec((None, 1, D_MODEL), lambda i: (i // per_b, 0, 0))
    tbl = pl.BlockSpec((PEER_HEADS, N_KEYS, ROUTE_TM), lambda i: (0, 0, i))
    row_tbl = jax.ShapeDtypeStruct((PEER_HEADS, rows // LANES, N_KEYS, LANES), F32)
    row_spec = pl.BlockSpec((PEER_HEADS, ROUTE_TM // LANES, N_KEYS, LANES), lambda i: (0, i, 0, 0))
    col_tbl = jax.ShapeDtypeStruct((PEER_HEADS, N_KEYS, rows), BF16)
    return pl.pallas_call(
        _route_kernel,
        grid=(rows // ROUTE_TM,),
        in_specs=[
            pl.BlockSpec((ROUTE_TM, D_MODEL), lambda i: (i, 0)),
            pl.BlockSpec((1, D_MODEL), lambda i: (0, 0)),
            per_batch, per_batch,
            pl.BlockSpec((D_MODEL, 2 * PEER_HEADS * PEER_HALF), lambda i: (0, 0)),
            pl.BlockSpec((2 * PEER_HEADS, N_KEYS, PEER_HALF), lambda i: (0, 0, 0)),
        ],
        out_specs=[pl.BlockSpec((D_MODEL, ROUTE_TM), lambda i: (0, i)), row_spec, row_spec, tbl, tbl],
        out_shape=[jax.ShapeDtypeStruct((D_MODEL, rows), BF16), row_tbl, row_tbl, col_tbl, col_tbl],
        scratch_shapes=[pltpu.VMEM((2 * PEER_HEADS, N_KEYS, ROUTE_TM), F32)],
        compiler_params=_params("arbitrary"),
        name="peer_route",
    )(x2d, g, sc, sh, wq, keys)


DENSE_TM = 512
DENSE_TE = 2048
DENSE_SUB = 512
DENSE_SHARE = 2
DENSE_JC = 32


def _dense_kernel(t_ref, cnt_ref, a_ref, r2_ref, b_ref, eu_ref, evt_ref, x_ref, g2_ref, fg_ref,
                  out_ref, acc_ref, act_ref, g_ref, rt_ref, *, final_norm):
    e = pl.program_id(1)

    @pl.when(e == 0)
    def _():
        acc_ref[...] = jnp.zeros_like(acc_ref)

    tm = t_ref.shape[1]
    zero = jnp.zeros((), BF16)

    def row_tile(ref, h, i):
        return jnp.concatenate([ref[h, c, pl.ds(i, 16, stride=0), :] for c in range(tm // LANES)],
                               axis=1).astype(BF16)

    def tiled(ref, slot, row, k, h):
        return jnp.concatenate([ref[slot, row, k, h]] * (DENSE_JC // 16), axis=0)

    rows_per_sub = DENSE_SUB // N_KEYS
    n_sub = DENSE_TE // DENSE_SUB

    def activations(sb, slot):
        act_ref[slot] = jnp.dot(eu_ref[sb * DENSE_SUB:(sb + 1) * DENSE_SUB, :], t_ref[...],
                                preferred_element_type=F32)

    def gating(sb, slot):
        for ii in range(rows_per_sub):
            i = e * (DENSE_TE // N_KEYS) + sb * rows_per_sub + ii
            for h in range(PEER_HEADS):
                rt_ref[slot, ii, 0, h] = row_tile(cnt_ref, h, i)
                rt_ref[slot, ii, 1, h] = row_tile(a_ref, h, i)
        for pair in range(0, rows_per_sub, DENSE_SHARE):
            for jc in range(N_KEYS // DENSE_JC):
                js = slice(jc * DENSE_JC, (jc + 1) * DENSE_JC)
                w = [None] * DENSE_SHARE
                for h in range(PEER_HEADS):
                    r2 = r2_ref[h, js, :]
                    bb = b_ref[h, js, :]
                    for s in range(DENSE_SHARE):
                        term = jnp.where(r2 < tiled(rt_ref, slot, pair + s, 0, h),
                                         bb * tiled(rt_ref, slot, pair + s, 1, h), zero)
                        w[s] = term if w[s] is None else w[s] + term
                for s in range(DENSE_SHARE):
                    ii = pair + s
                    rows = slice(ii * N_KEYS + jc * DENSE_JC, ii * N_KEYS + (jc + 1) * DENSE_JC)
                    g_ref[slot, rows, :] = _gelu_packed_x2(act_ref[slot, rows, :].astype(BF16)) * w[s]

    def accumulate(sb, slot):
        acc_ref[...] += jnp.dot(evt_ref[sb], g_ref[slot], preferred_element_type=F32)

    for sb in range(n_sub):
        activations(sb, sb)
    for sb in range(n_sub):
        gating(sb, sb)
        accumulate(sb, sb)

    @pl.when(e == pl.num_programs(1) - 1)
    def _():
        y = x_ref[...] + g2_ref[...] * acc_ref[...].T
        if final_norm:
            y = _rms(y, fg_ref[...])
        out_ref[...] = y


def _dense(t, cnt, a, r2, b, eu, evt, x2d, g2, fg, layer, final_norm):
    rows = x2d.shape[0]
    per_b = SEQ // DENSE_TM
    tbl = pl.BlockSpec((PEER_HEADS, N_KEYS, DENSE_TM), lambda i, e: (0, 0, i))
    row_spec = pl.BlockSpec((PEER_HEADS, DENSE_TM // LANES, N_KEYS, LANES), lambda i, e: (0, i, 0, 0))
    return pl.pallas_call(
        functools.partial(_dense_kernel, final_norm=final_norm),
        grid=(rows // DENSE_TM, N_EXPERTS // DENSE_TE),
        in_specs=[
            pl.BlockSpec((D_MODEL, DENSE_TM), lambda i, e: (0, i)),
            row_spec, row_spec, tbl, tbl,
            pl.BlockSpec((None, DENSE_TE, D_MODEL), lambda i, e: (layer, e, 0)),
            pl.BlockSpec((None, DENSE_TE // DENSE_SUB, D_MODEL, DENSE_SUB), lambda i, e: (layer, e, 0, 0)),
            pl.BlockSpec((DENSE_TM, D_MODEL), lambda i, e: (i, 0)),
            pl.BlockSpec((None, 1, D_MODEL), lambda i, e: (i // per_b, 0, 0)),
            pl.BlockSpec((1, D_MODEL), lambda i, e: (0, 0)),
        ],
        out_specs=pl.BlockSpec((DENSE_TM, D_MODEL), lambda i, e: (i, 0)),
        out_shape=jax.ShapeDtypeStruct((rows, D_MODEL), F32),
        scratch_shapes=[pltpu.VMEM((D_MODEL, DENSE_TM), F32),
                        pltpu.VMEM((DENSE_TE // DENSE_SUB, DENSE_SUB, DENSE_TM), F32),
                        pltpu.VMEM((DENSE_TE // DENSE_SUB, DENSE_SUB, DENSE_TM), BF16),
                        pltpu.VMEM((DENSE_TE // DENSE_SUB, DENSE_SUB // N_KEYS, 2, PEER_HEADS, 16, DENSE_TM), BF16)],
        compiler_params=_params("arbitrary", "arbitrary"),
        name="peer_dense",
    )(t, cnt, a, r2, b, eu, evt, x2d, g2, fg)


def _peer(x2d, g, sc, sh, g2, wq, keys, eu_all, evt_all, fg, layer, final_norm):
    wq_b = wq.astype(BF16)
    keys_b = keys.reshape(2 * PEER_HEADS, N_KEYS, PEER_HALF).astype(BF16)
    t, cnt, a, r2, b = _route(x2d, g, sc, sh, wq_b, keys_b)
    return _dense(t, cnt, a, r2, b, eu_all, evt_all, x2d, g2, fg, layer, final_norm)


def kernel(x, c, ctx, c_ctx, ada_w, ada_b, norm1_g, norm2_g, final_g, even_w_in, even_w_out, diff_lambda,
           diff_norm_g, odd_w_in, odd_b_in, sgu_norm_g, sgu_w, sgu_b, odd_w_out, peer_wq, peer_keys, peer_u, peer_v):
    cvec = jnp.concatenate([c, c_ctx[None, :], jnp.zeros((8 - BATCH - 1, D_MODEL), F32)], axis=0)
    mod = _ada(cvec, ada_w, ada_b).reshape(2, 8, 6, D_MODEL)
    row = lambda v: v.reshape(1, -1)
    per_batch = lambda l, j: mod[l, :BATCH, j].reshape(BATCH, 1, D_MODEL)
    fg = row(final_g)

    both = lambda j: jnp.stack([mod[0, :BATCH, j], jnp.broadcast_to(mod[0, BATCH, j], (BATCH, D_MODEL))],
                               axis=1).reshape(BATCH, 2, 1, D_MODEL)
    cos, sa, sb = _rope_tables()
    qkvf = _inproj(x.reshape(BATCH * SEQ, D_MODEL), ctx.reshape(BATCH * CTX_LEN, D_MODEL), row(norm1_g[0]),
                   both(1), both(0), even_w_in[0].astype(BF16), cos, sa, sb)
    qkvf = qkvf.reshape(BATCH, SEQ_ALL, EVEN_IN)
    o = _attn(qkvf, diff_lambda[0], row(diff_norm_g[0]))
    fm = _fourier(qkvf, _seq_dft())
    eu_all = peer_u.astype(BF16)
    evt_all = peer_v.astype(BF16).reshape(2, N_EXPERTS // DENSE_SUB, DENSE_SUB, D_MODEL).transpose(0, 1, 3, 2)
    x2d = x.reshape(BATCH * SEQ, D_MODEL)
    x2d = _outproj(o.reshape(BATCH * SEQ, DIFF_QK), fm.reshape(BATCH * SEQ, FOURIER_W),
                   even_w_out[0].astype(BF16), x2d, per_batch(0, 2))
    x2d = _peer(x2d, row(norm2_g[0]), per_batch(0, 4), per_batch(0, 3), per_batch(0, 5),
                peer_wq[0], peer_keys[0], eu_all, evt_all, fg, 0, False)

    bs = jnp.broadcast_to(sgu_b[0][:, :, None], (SGU_GROUPS, CHUNK, CHUNK))
    x2d = _sgu(x2d, row(norm1_g[1]), per_batch(1, 1), per_batch(1, 0), odd_w_in[0].astype(BF16),
               row(odd_b_in[0]), row(sgu_norm_g[0]), sgu_w[0].astype(BF16), bs, odd_w_out[0].astype(BF16),
               per_batch(1, 2))
    x2d = _peer(x2d, row(norm2_g[1]), per_batch(1, 4), per_batch(1, 3), per_batch(1, 5),
                peer_wq[1], peer_keys[1], eu_all, evt_all, fg, 1, True)
    return x2d.reshape(BATCH, SEQ, D_MODEL)
```

```python
import functools
import math

import numpy as np
import jax
import jax.numpy as jnp
from jax import lax
from jax.experimental import pallas as pl
from jax.experimental.pallas import tpu as pltpu

F32 = jnp.float32
BF16 = jnp.bfloat16

D_MODEL = 1024
BATCH = 4
SEQ = 4096
GRID_W = 64
CTX_LEN = 256
SEQ_ALL = CTX_LEN + SEQ
EPS = 1e-6

DIFF_HEADS = 6
DIFF_DH = 64
HEAD_W = 2 * DIFF_DH
DIFF_QK = DIFF_HEADS * HEAD_W
FOURIER_GROUPS = 4
FOURIER_GC = 64
FOURIER_W = FOURIER_GROUPS * FOURIER_GC
EVEN_IN = 3 * DIFF_QK + FOURIER_W
ROPE_BASE = 10000.0
ROPE_PAIRS = DIFF_DH // 4
LAM_INIT_L1 = 0.8 - 0.6 * math.exp(-0.3 * 0.0)
LOG2_E = math.log2(math.e)

CHUNK = 128
SGU_GROUPS = 8
SGU_W = SGU_GROUPS * CHUNK

PEER_HEADS = 8
N_KEYS = 128
N_EXPERTS = N_KEYS * N_KEYS
PEER_TOPK = 16
PEER_HALF = 128
LANES = 128

VMEM_LIMIT = 56 * 1024 * 1024

NT_DIMS = (((1,), (1,)), ((), ()))


def _rms(t, g):
    return t * lax.rsqrt(jnp.mean(t * t, axis=-1, keepdims=True) + EPS) * g


def _gelu(x):
    cdf = 0.5 * (1.0 + jnp.tanh(math.sqrt(2.0 / math.pi) * (x + 0.044715 * (x * x * x))))
    return x * cdf


def _gelu_packed_x2(x):
    c = math.sqrt(2.0 / math.pi)
    return x + x * jnp.tanh(x * (c + (c * 0.044715) * (x * x)))


def _params(*sem, flags=None):
    return pltpu.CompilerParams(dimension_semantics=sem, vmem_limit_bytes=VMEM_LIMIT, flags=flags)


ADA_TK = 256


def _ada_kernel(c_ref, w_ref, b_ref, o_ref):
    c = c_ref[...]
    s = c * jax.nn.sigmoid(c)
    part = jnp.dot(s, w_ref[...], preferred_element_type=F32)

    @pl.when(pl.program_id(1) == 0)
    def _():
        o_ref[...] = part + b_ref[...]

    @pl.when(pl.program_id(1) > 0)
    def _():
        o_ref[...] += part


def _ada(cvec, ada_w, ada_b):
    depth = ada_w.shape[0]
    return pl.pallas_call(
        _ada_kernel,
        grid=(depth, D_MODEL // ADA_TK),
        in_specs=[
            pl.BlockSpec((8, ADA_TK), lambda l, k: (0, k)),
            pl.BlockSpec((None, ADA_TK, 6 * D_MODEL), lambda l, k: (l, k, 0)),
            pl.BlockSpec((None, 1, 6 * D_MODEL), lambda l, k: (l, 0, 0)),
        ],
        out_specs=pl.BlockSpec((None, 8, 6 * D_MODEL), lambda l, k: (l, 0, 0)),
        out_shape=jax.ShapeDtypeStruct((depth, 8, 6 * D_MODEL), F32),
        compiler_params=_params("arbitrary", "arbitrary"),
        name="ada",
    )(cvec, ada_w, ada_b.reshape(depth, 1, 6 * D_MODEL))


IN_TM = 256
TILES_PER_SEQ = SEQ_ALL // IN_TM
assert CTX_LEN == IN_TM


def _inproj_kernel(x_ref, ctx_ref, g_ref, sc_ref, sh_ref, w_ref, cos_ref, sa_ref, sb_ref, o_ref):
    is_ctx = pl.program_id(0) % TILES_PER_SEQ == TILES_PER_SEQ - 1
    x = jnp.where(is_ctx, ctx_ref[...], x_ref[...])
    h = _rms(x, g_ref[...]) * (1.0 + sc_ref[...]) + sh_ref[...]
    r = jnp.dot(h.astype(BF16), w_ref[...], preferred_element_type=F32)
    cos, sa, sb = cos_ref[...], sa_ref[...], sb_ref[...]
    for cb in range(2 * DIFF_HEADS):
        t = r[:, cb * HEAD_W:(cb + 1) * HEAD_W]
        rot = t * cos + pltpu.roll(t, HEAD_W - 16, 1) * sa + pltpu.roll(t, 16, 1) * sb
        if cb < DIFF_HEADS:
            rot = rot * (DIFF_DH ** -0.5 * LOG2_E)
        o_ref[:, cb * HEAD_W:(cb + 1) * HEAD_W] = rot.astype(BF16)
    o_ref[:, 2 * DIFF_QK:] = r[:, 2 * DIFF_QK:].astype(BF16)


def _inproj(x2d, ctx2d, g, sc, sh, w, cos, sa, sb):
    rows = x2d.shape[0] + ctx2d.shape[0]
    lat_tiles = SEQ // IN_TM

    def lat_index(i):
        return (i // TILES_PER_SEQ) * lat_tiles + jnp.minimum(i % TILES_PER_SEQ, lat_tiles - 1)

    mod_spec = pl.BlockSpec((None, None, 1, D_MODEL),
                            lambda i: (i // TILES_PER_SEQ, jnp.where(i % TILES_PER_SEQ == TILES_PER_SEQ - 1, 1, 0), 0, 0))
    rope_spec = pl.BlockSpec((IN_TM, HEAD_W), lambda i: (i % TILES_PER_SEQ, 0))
    return pl.pallas_call(
        _inproj_kernel,
        grid=(rows // IN_TM,),
        in_specs=[
            pl.BlockSpec((IN_TM, D_MODEL), lambda i: (lat_index(i), 0)),
            pl.BlockSpec((CTX_LEN, D_MODEL), lambda i: (i // TILES_PER_SEQ, 0)),
            pl.BlockSpec((1, D_MODEL), lambda i: (0, 0)),
            mod_spec, mod_spec,
            pl.BlockSpec((D_MODEL, EVEN_IN), lambda i: (0, 0)),
            rope_spec, rope_spec, rope_spec,
        ],
        out_specs=pl.BlockSpec((IN_TM, EVEN_IN), lambda i: (i, 0)),
        out_shape=jax.ShapeDtypeStruct((rows, EVEN_IN), BF16),
        compiler_params=_params("arbitrary"),
        name="inproj",
    )(x2d, ctx2d, g, sc, sh, w, cos, sa, sb)


def _rope_tables():
    rows = SEQ // GRID_W
    r = jnp.repeat(jnp.arange(rows, dtype=F32), GRID_W)
    col = jnp.tile(jnp.arange(GRID_W, dtype=F32), rows)
    inv = ROPE_BASE ** (-jnp.arange(ROPE_PAIRS, dtype=F32) / ROPE_PAIRS)
    ar = r[:, None] * inv
    ac = col[:, None] * inv
    ang = jnp.concatenate([ar, ar, ac, ac] * 2, axis=-1)
    cos, sin = jnp.cos(ang), jnp.sin(ang)
    even = ((np.arange(HEAD_W) // ROPE_PAIRS) % 2 == 0)[None, :]
    sa = jnp.where(even, -sin, 0.0)
    sb = jnp.where(even, 0.0, sin)
    ident = jnp.ones((CTX_LEN, HEAD_W), F32)
    zero = jnp.zeros((CTX_LEN, HEAD_W), F32)
    return (jnp.concatenate([cos, ident], 0), jnp.concatenate([sa, zero], 0),
            jnp.concatenate([sb, zero], 0))


ATT_TQ = 2048
ATT_CHAINS = 8
ATT_KC = 256


def _attn_kernel(lam_ref, g_ref, q_ref, k_ref, v_ref, o_ref, vt_ref, st_ref, e_ref):
    @pl.when(pl.program_id(2) == 0)
    def _():
        vt_ref[:HEAD_W, :] = v_ref[...].astype(F32).T.astype(BF16)
        ones_row = lax.broadcasted_iota(jnp.int32, (16, SEQ_ALL), 0) == 0
        vt_ref[HEAD_W:, :] = jnp.where(ones_row, 1.0, 0.0).astype(BF16)

    lane = lax.broadcasted_iota(jnp.int32, (1, HEAD_W), 1)
    lp = lam_ref[...]
    lam = (jnp.exp(jnp.sum(lp[0:1] * lp[1:2], axis=-1, keepdims=True))
           - jnp.exp(jnp.sum(lp[2:3] * lp[3:4], axis=-1, keepdims=True)) + LAM_INIT_L1)
    tq = ATT_TQ // ATT_CHAINS
    n_kc = SEQ_ALL // ATT_KC

    def scores(c):
        q = q_ref[c * tq:(c + 1) * tq, :].astype(F32)
        qq = jnp.concatenate([jnp.where(lane < DIFF_DH, q, 0.0), jnp.where(lane >= DIFF_DH, q, 0.0)], axis=0)
        st_ref[c % 2] = lax.dot_general(k_ref[...], qq.astype(BF16), NT_DIMS, preferred_element_type=F32)

    def softmax(c):
        slot = c % 2
        m = functools.reduce(jnp.maximum, [jnp.max(st_ref[slot, kc * ATT_KC:(kc + 1) * ATT_KC, :], axis=0,
                                                    keepdims=True) for kc in range(n_kc)])
        for kc in range(n_kc):
            ks = slice(kc * ATT_KC, (kc + 1) * ATT_KC)
            e_ref[slot, ks, :] = jnp.exp2(st_ref[slot, ks, :] - m).astype(BF16)

    def values(c):
        ext = jnp.dot(vt_ref[...], e_ref[c % 2], preferred_element_type=F32)
        ot = ext[:HEAD_W] / ext[HEAD_W:HEAD_W + 1]
        o = (ot[:, :tq] - lam * ot[:, tq:]).T
        o_ref[c * tq:(c + 1) * tq, :] = (_rms(o, g_ref[...]) * (1.0 - LAM_INIT_L1)).astype(BF16)

    scores(0)
    for c in range(ATT_CHAINS):
        if c + 1 < ATT_CHAINS:
            scores(c + 1)
        softmax(c)
        if c > 0:
            values(c - 1)
    values(ATT_CHAINS - 1)


def _attn(qkvf, lam_p, head_g):
    nq = SEQ // ATT_TQ
    return pl.pallas_call(
        _attn_kernel,
        grid=(BATCH, DIFF_HEADS, nq),
        in_specs=[
            pl.BlockSpec((4, DIFF_DH), lambda b, h, i: (0, 0)),
            pl.BlockSpec((1, HEAD_W), lambda b, h, i: (0, 0)),
            pl.BlockSpec((None, ATT_TQ, HEAD_W), lambda b, h, i: (b, i, h)),
            pl.BlockSpec((None, SEQ_ALL, HEAD_W), lambda b, h, i: (b, 0, DIFF_HEADS + h)),
            pl.BlockSpec((None, SEQ_ALL, HEAD_W), lambda b, h, i: (b, 0, 2 * DIFF_HEADS + h)),
        ],
        out_specs=pl.BlockSpec((None, ATT_TQ, HEAD_W), lambda b, h, i: (b, i, h)),
        out_shape=jax.ShapeDtypeStruct((BATCH, SEQ, DIFF_QK), BF16),
        scratch_shapes=[pltpu.VMEM((HEAD_W + 16, SEQ_ALL), BF16),
                        pltpu.VMEM((2, SEQ_ALL, 2 * ATT_TQ // ATT_CHAINS), F32),
                        pltpu.VMEM((2, SEQ_ALL, 2 * ATT_TQ // ATT_CHAINS), BF16)],
        compiler_params=_params("arbitrary", "arbitrary", "arbitrary"),
        name="diff_attn",
    )(lam_p, head_g, qkvf, qkvf, qkvf)


def _channel_dft():
    n = np.arange(FOURIER_GC)
    ang = 2.0 * np.pi * np.outer(n, n) / FOURIER_GC
    eye = np.eye(FOURIER_GROUPS)
    scale = FOURIER_GC ** -0.5
    return np.concatenate([np.kron(eye, np.cos(ang)), np.kron(eye, np.sin(ang))], axis=1) * scale


def _seq_dft():
    k = jnp.arange(SEQ, dtype=jnp.int32)[:, None]
    n = jnp.arange(GRID_W, dtype=jnp.int32)[None, :]
    a = ((k * n) % GRID_W).astype(F32) * (2.0 * math.pi / GRID_W)
    b = ((k * n) % SEQ).astype(F32) * (2.0 * math.pi / SEQ)
    ca, sa, cb, sb = jnp.cos(a)[:, :, None], jnp.sin(a)[:, :, None], jnp.cos(b)[:, None, :], jnp.sin(b)[:, None, :]
    scale = SEQ ** -0.5
    cos = ((ca * cb - sa * sb) * scale).reshape(SEQ, SEQ)
    sin = ((sa * cb + ca * sb) * scale).reshape(SEQ, SEQ)
    return jnp.concatenate([cos, -sin], axis=1).astype(BF16)


def _fourier_chan_kernel(f_ref, bd_ref, o_ref):
    f = f_ref[:SEQ, :]
    uv = jnp.dot(f, bd_ref[...], preferred_element_type=F32)
    o_ref[:SEQ, :] = uv[:, :FOURIER_W].astype(BF16)
    o_ref[SEQ:, :] = uv[:, FOURIER_W:].astype(BF16)


def _fourier_seq_kernel(t_ref, uv_ref, o_ref):
    o_ref[...] = jnp.dot(t_ref[...], uv_ref[...], preferred_element_type=F32).astype(BF16)


FOURIER_TM = 512


def _fourier(qkvf, seq_dft):
    bd = jnp.asarray(_channel_dft(), dtype=BF16)
    uv = pl.pallas_call(
        _fourier_chan_kernel,
        grid=(BATCH,),
        in_specs=[
            pl.BlockSpec((None, SEQ_ALL, FOURIER_W), lambda b: (b, 0, 3 * DIFF_QK // FOURIER_W)),
            pl.BlockSpec((FOURIER_W, 2 * FOURIER_W), lambda b: (0, 0)),
        ],
        out_specs=pl.BlockSpec((None, 2 * SEQ, FOURIER_W), lambda b: (b, 0, 0)),
        out_shape=jax.ShapeDtypeStruct((BATCH, 2 * SEQ, FOURIER_W), BF16),
        compiler_params=_params("arbitrary"),
        name="fourier_chan",
    )(qkvf, bd)
    return pl.pallas_call(
        _fourier_seq_kernel,
        grid=(SEQ // FOURIER_TM, BATCH),
        in_specs=[
            pl.BlockSpec((FOURIER_TM, 2 * SEQ), lambda i, b: (i, 0)),
            pl.BlockSpec((None, 2 * SEQ, FOURIER_W), lambda i, b: (b, 0, 0)),
        ],
        out_specs=pl.BlockSpec((None, FOURIER_TM, FOURIER_W), lambda i, b: (b, i, 0)),
        out_shape=jax.ShapeDtypeStruct((BATCH, SEQ, FOURIER_W), BF16),
        compiler_params=_params("arbitrary", "arbitrary"),
        name="fourier_seq",
    )(seq_dft, uv)


OUT_TM = 512


def _outproj_kernel(o_ref, f_ref, w_ref, x_ref, g_ref, out_ref):
    y = jnp.dot(o_ref[...], w_ref[:DIFF_QK, :], preferred_element_type=F32)
    y = y + jnp.dot(f_ref[...], w_ref[DIFF_QK:, :], preferred_element_type=F32)
    out_ref[...] = x_ref[...] + g_ref[...] * y


def _outproj(o, fm, w, x2d, g1):
    rows = x2d.shape[0]
    per_b = SEQ // OUT_TM
    return pl.pallas_call(
        _outproj_kernel,
        grid=(rows // OUT_TM,),
        in_specs=[
            pl.BlockSpec((OUT_TM, DIFF_QK), lambda i: (i, 0)),
            pl.BlockSpec((OUT_TM, FOURIER_W), lambda i: (i, 0)),
            pl.BlockSpec((D_MODEL, D_MODEL), lambda i: (0, 0)),
            pl.BlockSpec((OUT_TM, D_MODEL), lambda i: (i, 0)),
            pl.BlockSpec((None, 1, D_MODEL), lambda i: (i // per_b, 0, 0)),
        ],
        out_specs=pl.BlockSpec((OUT_TM, D_MODEL), lambda i: (i, 0)),
        out_shape=jax.ShapeDtypeStruct((rows, D_MODEL), F32),
        compiler_params=_params("arbitrary"),
        name="outproj",
    )(o, fm, w, x2d, g1)


SGU_TM = 256


def _sgu_kernel(x_ref, g_ref, sc_ref, sh_ref, win_ref, bin_ref, ng_ref, ws_ref, bs_ref, wout_ref, g1_ref,
                out_ref, us_ref):
    x = x_ref[...]
    h = _rms(x, g_ref[...]) * (1.0 + sc_ref[...]) + sh_ref[...]
    z = _gelu(jnp.dot(h.astype(BF16), win_ref[...], preferred_element_type=F32) + bin_ref[...])
    u = z[:, :SGU_W]
    v = _rms(z[:, SGU_W:], ng_ref[...]).astype(BF16)
    for ck in range(SGU_TM // CHUNK):
        rows = slice(ck * CHUNK, (ck + 1) * CHUNK)
        for g in range(SGU_GROUPS):
            cols = slice(g * CHUNK, (g + 1) * CHUNK)
            s = jnp.dot(ws_ref[g], v[rows, cols], preferred_element_type=F32) + bs_ref[g]
            us_ref[rows, cols] = (u[rows, cols] * s).astype(BF16)
    y = jnp.dot(us_ref[...], wout_ref[...], preferred_element_type=F32)
    out_ref[...] = x + g1_ref[...] * y


def _sgu(x2d, g, sc, sh, win, b_in, ng, ws, bs, wout, g1):
    rows = x2d.shape[0]
    per_b = SEQ // SGU_TM
    vec = lambda n: pl.BlockSpec((1, n), lambda i: (0, 0))
    per_batch = pl.BlockSpec((None, 1, D_MODEL), lambda i: (i // per_b, 0, 0))
    return pl.pallas_call(
        _sgu_kernel,
        grid=(rows // SGU_TM,),
        in_specs=[
            pl.BlockSpec((SGU_TM, D_MODEL), lambda i: (i, 0)),
            vec(D_MODEL), per_batch, per_batch,
            pl.BlockSpec((D_MODEL, 2 * SGU_W), lambda i: (0, 0)),
            vec(2 * SGU_W), vec(SGU_W),
            pl.BlockSpec((SGU_GROUPS, CHUNK, CHUNK), lambda i: (0, 0, 0)),
            pl.BlockSpec((SGU_GROUPS, CHUNK, CHUNK), lambda i: (0, 0, 0)),
            pl.BlockSpec((SGU_W, D_MODEL), lambda i: (0, 0)),
            per_batch,
        ],
        out_specs=pl.BlockSpec((SGU_TM, D_MODEL), lambda i: (i, 0)),
        out_shape=jax.ShapeDtypeStruct((rows, D_MODEL), F32),
        scratch_shapes=[pltpu.VMEM((SGU_TM, SGU_W), BF16)],
        compiler_params=_params("arbitrary"),
        name="sgu",
    )(x2d, g, sc, sh, win, b_in, ng, ws, bs, wout, g1)


ROUTE_TM = 512
PAIR_CELLS = [(a, b) for a in range(PEER_TOPK + 1) for b in range(PEER_TOPK + 1)
              if (a + 1) * (b + 1) <= PEER_TOPK + 1]
PAIR_STREAMS = ([[ab for ab in PAIR_CELLS if ab[0] == a] for a in range(4)]
                + [[ab for ab in PAIR_CELLS if ab[0] >= 4 and ab[1] == b] for b in range(3)])
assert sorted(ab for s in PAIR_STREAMS for ab in s) == sorted(PAIR_CELLS)


def _sort16_network():
    pairs = []

    def merge(lo, n, r):
        step = 2 * r
        if step < n:
            merge(lo, n, step)
            merge(lo + r, n, step)
            pairs.extend((i, i + r) for i in range(lo + r, lo + n - r, step))
        else:
            pairs.append((lo, lo + r))

    def sort(lo, n):
        if n > 1:
            sort(lo, n // 2)
            sort(lo + n // 2, n // 2)
            merge(lo, n, 1)

    sort(0, 16)
    return pairs


SORT16 = _sort16_network()


def _top_sorted(st):
    sub = 8
    groups = [st[g * sub:(g + 1) * sub, :] for g in range(N_KEYS // sub)]
    for i, j in SORT16:
        groups[i], groups[j] = jnp.maximum(groups[i], groups[j]), jnp.minimum(groups[i], groups[j])
    floor = jnp.full_like(groups[0], -jnp.inf)
    vals = []
    for r in range(PEER_TOPK + 1):
        m = jnp.max(groups[0], axis=0, keepdims=True)
        vals.append(m)
        if r < PEER_TOPK:
            pop = groups[0] == m
            for d in range(PEER_TOPK - r):
                below = groups[d + 1] if d + 1 < len(groups) else floor
                groups[d] = jnp.where(pop, below, groups[d])
    return vals


def _count_above(vals, theta):
    def pivot(conds, lo, step):
        if not conds:
            return vals[lo + step - 1]
        return jnp.where(conds[0], pivot(conds[1:], lo + 2 * step * 2 ** (len(conds) - 1), step),
                         pivot(conds[1:], lo, step))

    conds, count = [], None
    for step in (8, 4, 2, 1):
        hit = pivot(conds, 0, step) > theta
        count = jnp.where(hit, float(step), 0.0) if count is None else jnp.where(hit, count + float(step), count)
        conds.append(hit)
    return jnp.where(vals[-1] > theta, count + 1.0, count)


def _route_kernel(x_ref, g_ref, sc_ref, sh_ref, wq_ref, keys_ref,
                  t_ref, cnt_ref, a_ref, r2_ref, b_ref, s_scr):
    x = x_ref[...]
    tf = _rms(x, g_ref[...]) * (1.0 + sc_ref[...]) + sh_ref[...]
    t_ref[...] = tf.T.astype(BF16)
    q = jnp.dot(tf.astype(BF16), wq_ref[...], preferred_element_type=F32)
    neg = jnp.float32(-jnp.inf)
    tops = []
    for hs in range(2 * PEER_HEADS):
        qh = q[:, hs * PEER_HALF:(hs + 1) * PEER_HALF].astype(BF16)
        st = lax.dot_general(keys_ref[hs], qh, NT_DIMS, preferred_element_type=F32)
        s_scr[hs] = st
        vals = _top_sorted(st)
        if hs % 2 == 1:
            r2_ref[hs // 2] = _count_above(vals[:PEER_TOPK], st).astype(BF16)
        tops.append(vals)
    v1 = [jnp.concatenate([tops[2 * h][r] for h in range(PEER_HEADS)], axis=0) for r in range(PEER_TOPK + 1)]
    v2 = [jnp.concatenate([tops[2 * h + 1][r] for h in range(PEER_HEADS)], axis=0) for r in range(PEER_TOPK + 1)]
    cell = {ab: v1[ab[0]] + v2[ab[1]] for ab in PAIR_CELLS}
    cand = list(cell.values())
    streams = [[cell[ab] for ab in s] for s in PAIR_STREAMS]
    floor = jnp.full_like(cand[0], neg)
    kth = None
    for r in range(PEER_TOPK + 1):
        m = functools.reduce(jnp.maximum, [s[0] for s in streams])
        if r == PEER_TOPK - 1:
            kth = m
        if r < PEER_TOPK:
            for s in streams:
                pop = s[0] == m
                for d in range(min(len(s), PEER_TOPK - r)):
                    s[d] = jnp.where(pop, s[d + 1] if d + 1 < len(s) else floor, s[d])
    thr = 0.5 * (kth + m)
    top = v1[0] + v2[0]
    z = functools.reduce(lambda p, c: p + jnp.where(c >= thr, jnp.exp(c - top), 0.0), cand, jnp.zeros_like(top))
    rz = 1.0 / z
    for h in range(PEER_HEADS):
        s1 = s_scr[2 * h]
        theta = thr[h:h + 1] - s1
        cnt = _count_above([vb[h:h + 1] for vb in v2[:PEER_TOPK]], theta)
        gate = jnp.exp(s1 - v1[0][h:h + 1]) * (0.5 * rz[h:h + 1])
        for c in range(ROUTE_TM // LANES):
            cnt_ref[h, c] = cnt[:, c * LANES:(c + 1) * LANES]
            a_ref[h, c] = gate[:, c * LANES:(c + 1) * LANES]
        b_ref[h] = jnp.exp(s_scr[2 * h + 1] - v2[0][h:h + 1]).astype(BF16)


def _route(x2d, g, sc, sh, wq, keys):
    rows = x2d.shape[0]
    per_b = SEQ // ROUTE_TM
    per_batch = pl.BlockSpec((None, 1, D_MODEL), lambda i: (i // per_b, 0, 0))
    tbl = pl.BlockSpec((PEER_HEADS, N_KEYS, ROUTE_TM), lambda i: (0, 0, i))
    row_tbl = jax.ShapeDtypeStruct((PEER_HEADS, rows // LANES, N_KEYS, LANES), F32)
    row_spec = pl.BlockSpec((PEER_HEADS, ROUTE_TM // LANES, N_KEYS, LANES), lambda i: (0, i, 0, 0))
    col_tbl = jax.ShapeDtypeStruct((PEER_HEADS, N_KEYS, rows), BF16)
    return pl.pallas_call(
        _route_kernel,
        grid=(rows // ROUTE_TM,),
        in_specs=[
            pl.BlockSpec((ROUTE_TM, D_MODEL), lambda i: (i, 0)),
            pl.BlockSpec((1, D_MODEL), lambda i: (0, 0)),
            per_batch, per_batch,
            pl.BlockSpec((D_MODEL, 2 * PEER_HEADS * PEER_HALF), lambda i: (0, 0)),
            pl.BlockSpec((2 * PEER_HEADS, N_KEYS, PEER_HALF), lambda i: (0, 0, 0)),
        ],
        out_specs=[pl.BlockSpec((D_MODEL, ROUTE_TM), lambda i: (0, i)), row_spec, row_spec, tbl, tbl],
        out_shape=[jax.ShapeDtypeStruct((D_MODEL, rows), BF16), row_tbl, row_tbl, col_tbl, col_tbl],
        scratch_shapes=[pltpu.VMEM((2 * PEER_HEADS, N_KEYS, ROUTE_TM), F32)],
        compiler_params=_params("arbitrary"),
        name="peer_route",
    )(x2d, g, sc, sh, wq, keys)


DENSE_TM = 512
DENSE_TE = 2048
DENSE_SUB = 512
DENSE_SHARE = 2
DENSE_JC = 32


def _dense_kernel(t_ref, cnt_ref, a_ref, r2_ref, b_ref, eu_ref, evt_ref, x_ref, g2_ref, fg_ref,
                  out_ref, acc_ref, act_ref, g_ref, rt_ref, *, final_norm):
    e = pl.program_id(1)

    @pl.when(e == 0)
    def _():
        acc_ref[...] = jnp.zeros_like(acc_ref)

    tm = t_ref.shape[1]
    zero = jnp.zeros((), BF16)

    def row_tile(ref, h, i):
        return jnp.concatenate([ref[h, c, pl.ds(i, 16, stride=0), :] for c in range(tm // LANES)],
                               axis=1).astype(BF16)

    def tiled(ref, slot, row, k, h):
        return jnp.concatenate([ref[slot, row, k, h]] * (DENSE_JC // 16), axis=0)

    rows_per_sub = DENSE_SUB // N_KEYS
    n_sub = DENSE_TE // DENSE_SUB

    def activations(sb, slot):
        act_ref[slot] = jnp.dot(eu_ref[sb * DENSE_SUB:(sb + 1) * DENSE_SUB, :], t_ref[...],
                                preferred_element_type=F32)

    def gating(sb, slot):
        for ii in range(rows_per_sub):
            i = e * (DENSE_TE // N_KEYS) + sb * rows_per_sub + ii
            for h in range(PEER_HEADS):
                rt_ref[slot, ii, 0, h] = row_tile(cnt_ref, h, i)
                rt_ref[slot, ii, 1, h] = row_tile(a_ref, h, i)
        for pair in range(0, rows_per_sub, DENSE_SHARE):
            for jc in range(N_KEYS // DENSE_JC):
                js = slice(jc * DENSE_JC, (jc + 1) * DENSE_JC)
                w = [None] * DENSE_SHARE
                for h in range(PEER_HEADS):
                    r2 = r2_ref[h, js, :]
                    bb = b_ref[h, js, :]
                    for s in range(DENSE_SHARE):
                        term = jnp.where(r2 < tiled(rt_ref, slot, pair + s, 0, h),
                                         bb * tiled(rt_ref, slot, pair + s, 1, h), zero)
                        w[s] = term if w[s] is None else w[s] + term
                for s in range(DENSE_SHARE):
                    ii = pair + s
                    rows = slice(ii * N_KEYS + jc * DENSE_JC, ii * N_KEYS + (jc + 1) * DENSE_JC)
                    g_ref[slot, rows, :] = _gelu_packed_x2(act_ref[slot, rows, :].astype(BF16)) * w[s]

    def accumulate(sb, slot):
        acc_ref[...] += jnp.dot(evt_ref[sb], g_ref[slot], preferred_element_type=F32)

    for sb in range(n_sub):
        activations(sb, sb)
    for sb in range(n_sub):
        gating(sb, sb)
        accumulate(sb, sb)

    @pl.when(e == pl.num_programs(1) - 1)
    def _():
        y = x_ref[...] + g2_ref[...] * acc_ref[...].T
        if final_norm:
            y = _rms(y, fg_ref[...])
        out_ref[...] = y


def _dense(t, cnt, a, r2, b, eu, evt, x2d, g2, fg, layer, final_norm):
    rows = x2d.shape[0]
    per_b = SEQ // DENSE_TM
    tbl = pl.BlockSpec((PEER_HEADS, N_KEYS, DENSE_TM), lambda i, e: (0, 0, i))
    row_spec = pl.BlockSpec((PEER_HEADS, DENSE_TM // LANES, N_KEYS, LANES), lambda i, e: (0, i, 0, 0))
    return pl.pallas_call(
        functools.partial(_dense_kernel, final_norm=final_norm),
        grid=(rows // DENSE_TM, N_EXPERTS // DENSE_TE),
        in_specs=[
            pl.BlockSpec((D_MODEL, DENSE_TM), lambda i, e: (0, i)),
            row_spec, row_spec, tbl, tbl,
            pl.BlockSpec((None, DENSE_TE, D_MODEL), lambda i, e: (layer, e, 0)),
            pl.BlockSpec((None, DENSE_TE // DENSE_SUB, D_MODEL, DENSE_SUB), lambda i, e: (layer, e, 0, 0)),
            pl.BlockSpec((DENSE_TM, D_MODEL), lambda i, e: (i, 0)),
            pl.BlockSpec((None, 1, D_MODEL), lambda i, e: (i // per_b, 0, 0)),
            pl.BlockSpec((1, D_MODEL), lambda i, e: (0, 0)),
        ],
        out_specs=pl.BlockSpec((DENSE_TM, D_MODEL), lambda i, e: (i, 0)),
        out_shape=jax.ShapeDtypeStruct((rows, D_MODEL), F32),
        scratch_shapes=[pltpu.VMEM((D_MODEL, DENSE_TM), F32),
                        pltpu.VMEM((DENSE_TE // DENSE_SUB, DENSE_SUB, DENSE_TM), F32),
                        pltpu.VMEM((DENSE_TE // DENSE_SUB, DENSE_SUB, DENSE_TM), BF16),
                        pltpu.VMEM((DENSE_TE // DENSE_SUB, DENSE_SUB // N_KEYS, 2, PEER_HEADS, 16, DENSE_TM), BF16)],
        compiler_params=_params("arbitrary", "arbitrary"),
        name="peer_dense",
    )(t, cnt, a, r2, b, eu, evt, x2d, g2, fg)


def _peer(x2d, g, sc, sh, g2, wq, keys, eu_all, evt_all, fg, layer, final_norm):
    wq_b = wq.astype(BF16)
    keys_b = keys.reshape(2 * PEER_HEADS, N_KEYS, PEER_HALF).astype(BF16)
    t, cnt, a, r2, b = _route(x2d, g, sc, sh, wq_b, keys_b)
    return _dense(t, cnt, a, r2, b, eu_all, evt_all, x2d, g2, fg, layer, final_norm)


def kernel(x, c, ctx, c_ctx, ada_w, ada_b, norm1_g, norm2_g, final_g, even_w_in, even_w_out, diff_lambda,
           diff_norm_g, odd_w_in, odd_b_in, sgu_norm_g, sgu_w, sgu_b, odd_w_out, peer_wq, peer_keys, peer_u, peer_v):
    cvec = jnp.concatenate([c, c_ctx[None, :], jnp.zeros((8 - BATCH - 1, D_MODEL), F32)], axis=0)
    mod = _ada(cvec, ada_w, ada_b).reshape(2, 8, 6, D_MODEL)
    row = lambda v: v.reshape(1, -1)
    per_batch = lambda l, j: mod[l, :BATCH, j].reshape(BATCH, 1, D_MODEL)
    fg = row(final_g)

    both = lambda j: jnp.stack([mod[0, :BATCH, j], jnp.broadcast_to(mod[0, BATCH, j], (BATCH, D_MODEL))],
                               axis=1).reshape(BATCH, 2, 1, D_MODEL)
    cos, sa, sb = _rope_tables()
    qkvf = _inproj(x.reshape(BATCH * SEQ, D_MODEL), ctx.reshape(BATCH * CTX_LEN, D_MODEL), row(norm1_g[0]),
                   both(1), both(0), even_w_in[0].astype(BF16), cos, sa, sb)
    qkvf = qkvf.reshape(BATCH, SEQ_ALL, EVEN_IN)
    o = _attn(qkvf, diff_lambda[0], row(diff_norm_g[0]))
    fm = _fourier(qkvf, _seq_dft())
    eu_all = peer_u.astype(BF16)
    evt_all = peer_v.astype(BF16).reshape(2, N_EXPERTS // DENSE_SUB, DENSE_SUB, D_MODEL).transpose(0, 1, 3, 2)
    x2d = x.reshape(BATCH * SEQ, D_MODEL)
    x2d = _outproj(o.reshape(BATCH * SEQ, DIFF_QK), fm.reshape(BATCH * SEQ, FOURIER_W),
                   even_w_out[0].astype(BF16), x2d, per_batch(0, 2))
    x2d = _peer(x2d, row(norm2_g[0]), per_batch(0, 4), per_batch(0, 3), per_batch(0, 5),
                peer_wq[0], peer_keys[0], eu_all, evt_all, fg, 0, False)

    bs = jnp.broadcast_to(sgu_b[0][:, :, None], (SGU_GROUPS, CHUNK, CHUNK))
    x2d = _sgu(x2d, row(norm1_g[1]), per_batch(1, 1), per_batch(1, 0), odd_w_in[0].astype(BF16),
               row(odd_b_in[0]), row(sgu_norm_g[0]), sgu_w[0].astype(BF16), bs, odd_w_out[0].astype(BF16),
               per_batch(1, 2))
    x2d = _peer(x2d, row(norm2_g[1]), per_batch(1, 4), per_batch(1, 3), per_batch(1, 5),
                peer_wq[1], peer_keys[1], eu_all, evt_all, fg, 1, True)
    return x2d.reshape(BATCH, SEQ, D_MODEL)
```

```python
import functools
import math

import numpy as np
import jax
import jax.numpy as jnp
from jax import lax
from jax.experimental import pallas as pl
from jax.experimental.pallas import tpu as pltpu

F32 = jnp.float32
BF16 = jnp.bfloat16

D_MODEL = 1024
BATCH = 4
SEQ = 4096
GRID_W = 64
CTX_LEN = 256
SEQ_ALL = CTX_LEN + SEQ
EPS = 1e-6

DIFF_HEADS = 6
DIFF_DH = 64
HEAD_W = 2 * DIFF_DH
DIFF_QK = DIFF_HEADS * HEAD_W
FOURIER_GROUPS = 4
FOURIER_GC = 64
FOURIER_W = FOURIER_GROUPS * FOURIER_GC
EVEN_IN = 3 * DIFF_QK + FOURIER_W
ROPE_BASE = 10000.0
ROPE_PAIRS = DIFF_DH // 4
LAM_INIT_L1 = 0.8 - 0.6 * math.exp(-0.3 * 0.0)
LOG2_E = math.log2(math.e)

CHUNK = 128
SGU_GROUPS = 8
SGU_W = SGU_GROUPS * CHUNK

PEER_HEADS = 8
N_KEYS = 128
N_EXPERTS = N_KEYS * N_KEYS
PEER_TOPK = 16
PEER_HALF = 128
LANES = 128

VMEM_LIMIT = 56 * 1024 * 1024

NT_DIMS = (((1,), (1,)), ((), ()))


def _rms(t, g):
    return t * lax.rsqrt(jnp.mean(t * t, axis=-1, keepdims=True) + EPS) * g


def _gelu(x):
    cdf = 0.5 * (1.0 + jnp.tanh(math.sqrt(2.0 / math.pi) * (x + 0.044715 * (x * x * x))))
    return x * cdf


def _gelu_packed_x2(x):
    c = math.sqrt(2.0 / math.pi)
    return x + x * jnp.tanh(x * (c + (c * 0.044715) * (x * x)))


def _params(*sem, flags=None):
    return pltpu.CompilerParams(dimension_semantics=sem, vmem_limit_bytes=VMEM_LIMIT, flags=flags)


ADA_TK = 256
ADA_SPLIT = 4


def _ada_kernel(c_ref, *refs):
    w_refs, b_ref, o_ref = refs[:ADA_SPLIT], refs[ADA_SPLIT], refs[ADA_SPLIT + 1]
    c = c_ref[...]
    s = c * jax.nn.sigmoid(c)
    rows = ADA_TK // ADA_SPLIT
    part = functools.reduce(lambda p, q: p + q, [
        jnp.dot(s[:, j * rows:(j + 1) * rows], w_refs[j][...], preferred_element_type=F32)
        for j in range(ADA_SPLIT)])

    @pl.when(pl.program_id(1) == 0)
    def _():
        o_ref[...] = part + b_ref[...]

    @pl.when(pl.program_id(1) > 0)
    def _():
        o_ref[...] += part


def _ada(cvec, ada_w, ada_b):
    depth = ada_w.shape[0]
    return pl.pallas_call(
        _ada_kernel,
        grid=(depth, D_MODEL // ADA_TK),
        in_specs=[pl.BlockSpec((8, ADA_TK), lambda l, k: (0, k))] + [
            pl.BlockSpec((None, ADA_TK // ADA_SPLIT, 6 * D_MODEL), lambda l, k, j=j: (l, k * ADA_SPLIT + j, 0))
            for j in range(ADA_SPLIT)] + [
            pl.BlockSpec((None, 1, 6 * D_MODEL), lambda l, k: (l, 0, 0)),
        ],
        out_specs=pl.BlockSpec((None, 8, 6 * D_MODEL), lambda l, k: (l, 0, 0)),
        out_shape=jax.ShapeDtypeStruct((depth, 8, 6 * D_MODEL), F32),
        compiler_params=_params("arbitrary", "arbitrary"),
        name="ada",
    )(cvec, *([ada_w] * ADA_SPLIT), ada_b.reshape(depth, 1, 6 * D_MODEL))


IN_TM = 256
TILES_PER_SEQ = SEQ_ALL // IN_TM
assert CTX_LEN == IN_TM


def _inproj_kernel(x_ref, ctx_ref, g_ref, sc_ref, sh_ref, w_ref, cos_ref, sa_ref, sb_ref, o_ref):
    is_ctx = pl.program_id(0) % TILES_PER_SEQ == TILES_PER_SEQ - 1
    x = jnp.where(is_ctx, ctx_ref[...], x_ref[...])
    h = _rms(x, g_ref[...]) * (1.0 + sc_ref[...]) + sh_ref[...]
    r = jnp.dot(h.astype(BF16), w_ref[...], preferred_element_type=F32)
    cos, sa, sb = cos_ref[...], sa_ref[...], sb_ref[...]
    for cb in range(2 * DIFF_HEADS):
        t = r[:, cb * HEAD_W:(cb + 1) * HEAD_W]
        rot = t * cos + pltpu.roll(t, HEAD_W - 16, 1) * sa + pltpu.roll(t, 16, 1) * sb
        if cb < DIFF_HEADS:
            rot = rot * (DIFF_DH ** -0.5 * LOG2_E)
        o_ref[:, cb * HEAD_W:(cb + 1) * HEAD_W] = rot.astype(BF16)
    o_ref[:, 2 * DIFF_QK:] = r[:, 2 * DIFF_QK:].astype(BF16)


def _inproj(x2d, ctx2d, g, sc, sh, w, cos, sa, sb):
    rows = x2d.shape[0] + ctx2d.shape[0]
    lat_tiles = SEQ // IN_TM

    def lat_index(i):
        return (i // TILES_PER_SEQ) * lat_tiles + jnp.minimum(i % TILES_PER_SEQ, lat_tiles - 1)

    mod_spec = pl.BlockSpec((None, None, 1, D_MODEL),
                            lambda i: (i // TILES_PER_SEQ, jnp.where(i % TILES_PER_SEQ == TILES_PER_SEQ - 1, 1, 0), 0, 0))
    rope_spec = pl.BlockSpec((IN_TM, HEAD_W), lambda i: (i % TILES_PER_SEQ, 0))
    return pl.pallas_call(
        _inproj_kernel,
        grid=(rows // IN_TM,),
        in_specs=[
            pl.BlockSpec((IN_TM, D_MODEL), lambda i: (lat_index(i), 0)),
            pl.BlockSpec((CTX_LEN, D_MODEL), lambda i: (i // TILES_PER_SEQ, 0)),
            pl.BlockSpec((1, D_MODEL), lambda i: (0, 0)),
            mod_spec, mod_spec,
            pl.BlockSpec((D_MODEL, EVEN_IN), lambda i: (0, 0)),
            rope_spec, rope_spec, rope_spec,
        ],
        out_specs=pl.BlockSpec((IN_TM, EVEN_IN), lambda i: (i, 0)),
        out_shape=jax.ShapeDtypeStruct((rows, EVEN_IN), BF16),
        compiler_params=_params("arbitrary"),
        name="inproj",
    )(x2d, ctx2d, g, sc, sh, w, cos, sa, sb)


def _rope_tables():
    rows = SEQ // GRID_W
    r = jnp.repeat(jnp.arange(rows, dtype=F32), GRID_W)
    col = jnp.tile(jnp.arange(GRID_W, dtype=F32), rows)
    inv = ROPE_BASE ** (-jnp.arange(ROPE_PAIRS, dtype=F32) / ROPE_PAIRS)
    ar = r[:, None] * inv
    ac = col[:, None] * inv
    ang = jnp.concatenate([ar, ar, ac, ac] * 2, axis=-1)
    cos, sin = jnp.cos(ang), jnp.sin(ang)
    even = ((np.arange(HEAD_W) // ROPE_PAIRS) % 2 == 0)[None, :]
    sa = jnp.where(even, -sin, 0.0)
    sb = jnp.where(even, 0.0, sin)
    ident = jnp.ones((CTX_LEN, HEAD_W), F32)
    zero = jnp.zeros((CTX_LEN, HEAD_W), F32)
    return (jnp.concatenate([cos, ident], 0), jnp.concatenate([sa, zero], 0),
            jnp.concatenate([sb, zero], 0))


ATT_TQ = 2048
ATT_CHAINS = 8
ATT_KC = 256


def _attn_kernel(lam_ref, g_ref, q_ref, k_ref, v_ref, o_ref, vt_ref, st_ref, e_ref):
    @pl.when(pl.program_id(2) == 0)
    def _():
        vt_ref[:HEAD_W, :] = v_ref[...].astype(F32).T.astype(BF16)
        ones_row = lax.broadcasted_iota(jnp.int32, (16, SEQ_ALL), 0) == 0
        vt_ref[HEAD_W:, :] = jnp.where(ones_row, 1.0, 0.0).astype(BF16)

    lane = lax.broadcasted_iota(jnp.int32, (1, HEAD_W), 1)
    lp = lam_ref[...]
    lam = (jnp.exp(jnp.sum(lp[0:1] * lp[1:2], axis=-1, keepdims=True))
           - jnp.exp(jnp.sum(lp[2:3] * lp[3:4], axis=-1, keepdims=True)) + LAM_INIT_L1)
    tq = ATT_TQ // ATT_CHAINS
    n_kc = SEQ_ALL // ATT_KC

    def scores(c):
        q = q_ref[c * tq:(c + 1) * tq, :].astype(F32)
        qq = jnp.concatenate([jnp.where(lane < DIFF_DH, q, 0.0), jnp.where(lane >= DIFF_DH, q, 0.0)], axis=0)
        st_ref[c % 2] = lax.dot_general(k_ref[...], qq.astype(BF16), NT_DIMS, preferred_element_type=F32)

    def softmax(c):
        slot = c % 2
        m = functools.reduce(jnp.maximum, [jnp.max(st_ref[slot, kc * ATT_KC:(kc + 1) * ATT_KC, :], axis=0,
                                                    keepdims=True) for kc in range(n_kc)])
        for kc in range(n_kc):
            ks = slice(kc * ATT_KC, (kc + 1) * ATT_KC)
            e_ref[slot, ks, :] = jnp.exp2(st_ref[slot, ks, :] - m).astype(BF16)

    def values(c):
        ext = jnp.dot(vt_ref[...], e_ref[c % 2], preferred_element_type=F32)
        ot = ext[:HEAD_W] / ext[HEAD_W:HEAD_W + 1]
        o = (ot[:, :tq] - lam * ot[:, tq:]).T
        o_ref[c * tq:(c + 1) * tq, :] = (_rms(o, g_ref[...]) * (1.0 - LAM_INIT_L1)).astype(BF16)

    scores(0)
    for c in range(ATT_CHAINS):
        if c + 1 < ATT_CHAINS:
            scores(c + 1)
        softmax(c)
        if c > 0:
            values(c - 1)
    values(ATT_CHAINS - 1)


def _attn(qkvf, lam_p, head_g):
    nq = SEQ // ATT_TQ
    return pl.pallas_call(
        _attn_kernel,
        grid=(BATCH, DIFF_HEADS, nq),
        in_specs=[
            pl.BlockSpec((4, DIFF_DH), lambda b, h, i: (0, 0)),
            pl.BlockSpec((1, HEAD_W), lambda b, h, i: (0, 0)),
            pl.BlockSpec((None, ATT_TQ, HEAD_W), lambda b, h, i: (b, i, h)),
            pl.BlockSpec((None, SEQ_ALL, HEAD_W), lambda b, h, i: (b, 0, DIFF_HEADS + h)),
            pl.BlockSpec((None, SEQ_ALL, HEAD_W), lambda b, h, i: (b, 0, 2 * DIFF_HEADS + h)),
        ],
        out_specs=pl.BlockSpec((None, ATT_TQ, HEAD_W), lambda b, h, i: (b, i, h)),
        out_shape=jax.ShapeDtypeStruct((BATCH, SEQ, DIFF_QK), BF16),
        scratch_shapes=[pltpu.VMEM((HEAD_W + 16, SEQ_ALL), BF16),
                        pltpu.VMEM((2, SEQ_ALL, 2 * ATT_TQ // ATT_CHAINS), F32),
                        pltpu.VMEM((2, SEQ_ALL, 2 * ATT_TQ // ATT_CHAINS), BF16)],
        compiler_params=_params("arbitrary", "arbitrary", "arbitrary"),
        name="diff_attn",
    )(lam_p, head_g, qkvf, qkvf, qkvf)


def _channel_dft():
    n = np.arange(FOURIER_GC)
    ang = 2.0 * np.pi * np.outer(n, n) / FOURIER_GC
    eye = np.eye(FOURIER_GROUPS)
    scale = FOURIER_GC ** -0.5
    return np.concatenate([np.kron(eye, np.cos(ang)), np.kron(eye, np.sin(ang))], axis=1) * scale


def _seq_dft():
    k = jnp.arange(SEQ, dtype=jnp.int32)[:, None]
    n = jnp.arange(GRID_W, dtype=jnp.int32)[None, :]
    a = ((k * n) % GRID_W).astype(F32) * (2.0 * math.pi / GRID_W)
    b = ((k * n) % SEQ).astype(F32) * (2.0 * math.pi / SEQ)
    ca, sa, cb, sb = jnp.cos(a)[:, :, None], jnp.sin(a)[:, :, None], jnp.cos(b)[:, None, :], jnp.sin(b)[:, None, :]
    scale = SEQ ** -0.5
    cos = ((ca * cb - sa * sb) * scale).reshape(SEQ, SEQ)
    sin = ((sa * cb + ca * sb) * scale).reshape(SEQ, SEQ)
    return jnp.concatenate([cos, -sin], axis=1).astype(BF16)


def _fourier_chan_kernel(f_ref, bd_ref, o_ref):
    f = f_ref[:SEQ, :]
    uv = jnp.dot(f, bd_ref[...], preferred_element_type=F32)
    o_ref[:SEQ, :] = uv[:, :FOURIER_W].astype(BF16)
    o_ref[SEQ:, :] = uv[:, FOURIER_W:].astype(BF16)


def _fourier_seq_kernel(t_ref, uv_ref, o_ref):
    o_ref[...] = jnp.dot(t_ref[...], uv_ref[...], preferred_element_type=F32).astype(BF16)


FOURIER_TM = 512


def _fourier(qkvf, seq_dft):
    bd = jnp.asarray(_channel_dft(), dtype=BF16)
    uv = pl.pallas_call(
        _fourier_chan_kernel,
        grid=(BATCH,),
        in_specs=[
            pl.BlockSpec((None, SEQ_ALL, FOURIER_W), lambda b: (b, 0, 3 * DIFF_QK // FOURIER_W)),
            pl.BlockSpec((FOURIER_W, 2 * FOURIER_W), lambda b: (0, 0)),
        ],
        out_specs=pl.BlockSpec((None, 2 * SEQ, FOURIER_W), lambda b: (b, 0, 0)),
        out_shape=jax.ShapeDtypeStruct((BATCH, 2 * SEQ, FOURIER_W), BF16),
        compiler_params=_params("arbitrary"),
        name="fourier_chan",
    )(qkvf, bd)
    return pl.pallas_call(
        _fourier_seq_kernel,
        grid=(SEQ // FOURIER_TM, BATCH),
        in_specs=[
            pl.BlockSpec((FOURIER_TM, 2 * SEQ), lambda i, b: (i, 0)),
            pl.BlockSpec((None, 2 * SEQ, FOURIER_W), lambda i, b: (b, 0, 0)),
        ],
        out_specs=pl.BlockSpec((None, FOURIER_TM, FOURIER_W), lambda i, b: (b, i, 0)),
        out_shape=jax.ShapeDtypeStruct((BATCH, SEQ, FOURIER_W), BF16),
        compiler_params=_params("arbitrary", "arbitrary"),
        name="fourier_seq",
    )(seq_dft, uv)


OUT_TM = 512


def _outproj_kernel(o_ref, f_ref, w_ref, x_ref, g_ref, out_ref):
    y = jnp.dot(o_ref[...], w_ref[:DIFF_QK, :], preferred_element_type=F32)
    y = y + jnp.dot(f_ref[...], w_ref[DIFF_QK:, :], preferred_element_type=F32)
    out_ref[...] = x_ref[...] + g_ref[...] * y


def _outproj(o, fm, w, x2d, g1):
    rows = x2d.shape[0]
    per_b = SEQ // OUT_TM
    return pl.pallas_call(
        _outproj_kernel,
        grid=(rows // OUT_TM,),
        in_specs=[
            pl.BlockSpec((OUT_TM, DIFF_QK), lambda i: (i, 0)),
            pl.BlockSpec((OUT_TM, FOURIER_W), lambda i: (i, 0)),
            pl.BlockSpec((D_MODEL, D_MODEL), lambda i: (0, 0)),
            pl.BlockSpec((OUT_TM, D_MODEL), lambda i: (i, 0)),
            pl.BlockSpec((None, 1, D_MODEL), lambda i: (i // per_b, 0, 0)),
        ],
        out_specs=pl.BlockSpec((OUT_TM, D_MODEL), lambda i: (i, 0)),
        out_shape=jax.ShapeDtypeStruct((rows, D_MODEL), F32),
        compiler_params=_params("arbitrary"),
        name="outproj",
    )(o, fm, w, x2d, g1)


SGU_TM = 256


def _sgu_kernel(x_ref, g_ref, sc_ref, sh_ref, win_ref, bin_ref, ng_ref, ws_ref, bs_ref, wout_ref, g1_ref,
                out_ref, us_ref):
    x = x_ref[...]
    h = _rms(x, g_ref[...]) * (1.0 + sc_ref[...]) + sh_ref[...]
    z = _gelu(jnp.dot(h.astype(BF16), win_ref[...], preferred_element_type=F32) + bin_ref[...])
    u = z[:, :SGU_W]
    v = _rms(z[:, SGU_W:], ng_ref[...]).astype(BF16)
    for ck in range(SGU_TM // CHUNK):
        rows = slice(ck * CHUNK, (ck + 1) * CHUNK)
        for g in range(SGU_GROUPS):
            cols = slice(g * CHUNK, (g + 1) * CHUNK)
            s = jnp.dot(ws_ref[g], v[rows, cols], preferred_element_type=F32) + bs_ref[g]
            us_ref[rows, cols] = (u[rows, cols] * s).astype(BF16)
    y = jnp.dot(us_ref[...], wout_ref[...], preferred_element_type=F32)
    out_ref[...] = x + g1_ref[...] * y


def _sgu(x2d, g, sc, sh, win, b_in, ng, ws, bs, wout, g1):
    rows = x2d.shape[0]
    per_b = SEQ // SGU_TM
    vec = lambda n: pl.BlockSpec((1, n), lambda i: (0, 0))
    per_batch = pl.BlockSpec((None, 1, D_MODEL), lambda i: (i // per_b, 0, 0))
    return pl.pallas_call(
        _sgu_kernel,
        grid=(rows // SGU_TM,),
        in_specs=[
            pl.BlockSpec((SGU_TM, D_MODEL), lambda i: (i, 0)),
            vec(D_MODEL), per_batch, per_batch,
            pl.BlockSpec((D_MODEL, 2 * SGU_W), lambda i: (0, 0)),
            vec(2 * SGU_W), vec(SGU_W),
            pl.BlockSpec((SGU_GROUPS, CHUNK, CHUNK), lambda i: (0, 0, 0)),
            pl.BlockSpec((SGU_GROUPS, CHUNK, CHUNK), lambda i: (0, 0, 0)),
            pl.BlockSpec((SGU_W, D_MODEL), lambda i: (0, 0)),
            per_batch,
        ],
        out_specs=pl.BlockSpec((SGU_TM, D_MODEL), lambda i: (i, 0)),
        out_shape=jax.ShapeDtypeStruct((rows, D_MODEL), F32),
        scratch_shapes=[pltpu.VMEM((SGU_TM, SGU_W), BF16)],
        compiler_params=_params("arbitrary"),
        name="sgu",
    )(x2d, g, sc, sh, win, b_in, ng, ws, bs, wout, g1)


ROUTE_TM = 512
PAIR_CELLS = [(a, b) for a in range(PEER_TOPK + 1) for b in range(PEER_TOPK + 1)
              if (a + 1) * (b + 1) <= PEER_TOPK + 1]
PAIR_STREAMS = ([[ab for ab in PAIR_CELLS if ab[0] == a] for a in range(4)]
                + [[ab for ab in PAIR_CELLS if ab[0] >= 4 and ab[1] == b] for b in range(3)])
assert sorted(ab for s in PAIR_STREAMS for ab in s) == sorted(PAIR_CELLS)


def _sort16_network():
    pairs = []

    def merge(lo, n, r):
        step = 2 * r
        if step < n:
            merge(lo, n, step)
            merge(lo + r, n, step)
            pairs.extend((i, i + r) for i in range(lo + r, lo + n - r, step))
        else:
            pairs.append((lo, lo + r))

    def sort(lo, n):
        if n > 1:
            sort(lo, n // 2)
            sort(lo + n // 2, n // 2)
            merge(lo, n, 1)

    sort(0, 16)
    return pairs


SORT16 = _sort16_network()


def _top_sorted(st):
    sub = 8
    groups = [st[g * sub:(g + 1) * sub, :] for g in range(N_KEYS // sub)]
    for i, j in SORT16:
        groups[i], groups[j] = jnp.maximum(groups[i], groups[j]), jnp.minimum(groups[i], groups[j])
    floor = jnp.full_like(groups[0], -jnp.inf)
    vals = []
    for r in range(PEER_TOPK + 1):
        m = jnp.max(groups[0], axis=0, keepdims=True)
        vals.append(m)
        if r < PEER_TOPK:
            pop = groups[0] == m
            for d in range(PEER_TOPK - r):
                below = groups[d + 1] if d + 1 < len(groups) else floor
                groups[d] = jnp.where(pop, below, groups[d])
    return vals


def _count_above(vals, theta):
    def pivot(conds, lo, step):
        if not conds:
            return vals[lo + step - 1]
        return jnp.where(conds[0], pivot(conds[1:], lo + 2 * step * 2 ** (len(conds) - 1), step),
                         pivot(conds[1:], lo, step))

    conds, count = [], None
    for step in (8, 4, 2, 1):
        hit = pivot(conds, 0, step) > theta
        count = jnp.where(hit, float(step), 0.0) if count is None else jnp.where(hit, count + float(step), count)
        conds.append(hit)
    return jnp.where(vals[-1] > theta, count + 1.0, count)


def _route_kernel(x_ref, g_ref, sc_ref, sh_ref, wq_ref, keys_ref,
                  t_ref, cnt_ref, a_ref, r2_ref, b_ref, s_scr):
    x = x_ref[...]
    tf = _rms(x, g_ref[...]) * (1.0 + sc_ref[...]) + sh_ref[...]
    t_ref[...] = tf.T.astype(BF16)
    q = jnp.dot(tf.astype(BF16), wq_ref[...], preferred_element_type=F32)
    neg = jnp.float32(-jnp.inf)
    tops = []
    for hs in range(2 * PEER_HEADS):
        qh = q[:, hs * PEER_HALF:(hs + 1) * PEER_HALF].astype(BF16)
        st = lax.dot_general(keys_ref[hs], qh, NT_DIMS, preferred_element_type=F32)
        s_scr[hs] = st
        vals = _top_sorted(st)
        if hs % 2 == 1:
            r2_ref[hs // 2] = _count_above(vals[:PEER_TOPK], st).astype(BF16)
        tops.append(vals)
    v1 = [jnp.concatenate([tops[2 * h][r] for h in range(PEER_HEADS)], axis=0) for r in range(PEER_TOPK + 1)]
    v2 = [jnp.concatenate([tops[2 * h + 1][r] for h in range(PEER_HEADS)], axis=0) for r in range(PEER_TOPK + 1)]
    cell = {ab: v1[ab[0]] + v2[ab[1]] for ab in PAIR_CELLS}
    cand = list(cell.values())
    streams = [[cell[ab] for ab in s] for s in PAIR_STREAMS]
    floor = jnp.full_like(cand[0], neg)
    kth = None
    for r in range(PEER_TOPK + 1):
        m = functools.reduce(jnp.maximum, [s[0] for s in streams])
        if r == PEER_TOPK - 1:
            kth = m
        if r < PEER_TOPK:
            for s in streams:
                pop = s[0] == m
                for d in range(min(len(s), PEER_TOPK - r)):
                    s[d] = jnp.where(pop, s[d + 1] if d + 1 < len(s) else floor, s[d])
    thr = 0.5 * (kth + m)
    top = v1[0] + v2[0]
    z = functools.reduce(lambda p, c: p + jnp.where(c >= thr, jnp.exp(c - top), 0.0), cand, jnp.zeros_like(top))
    rz = 1.0 / z
    for h in range(PEER_HEADS):
        s1 = s_scr[2 * h]
        theta = thr[h:h + 1] - s1
        cnt = _count_above([vb[h:h + 1] for vb in v2[:PEER_TOPK]], theta)
        gate = jnp.exp(s1 - v1[0][h:h + 1]) * (0.5 * rz[h:h + 1])
        for c in range(ROUTE_TM // LANES):
            cnt_ref[h, c] = cnt[:, c * LANES:(c + 1) * LANES]
            a_ref[h, c] = gate[:, c * LANES:(c + 1) * LANES]
        b_ref[h] = jnp.exp(s_scr[2 * h + 1] - v2[0][h:h + 1]).astype(BF16)


def _route(x2d, g, sc, sh, wq, keys):
    rows = x2d.shape[0]
    per_b = SEQ // ROUTE_TM
    per_batch = pl.BlockSpec((None, 1, D_MODEL), lambda i: (i // per_b, 0, 0))
    tbl = pl.BlockSpec((PEER_HEADS, N_KEYS, ROUTE_TM), lambda i: (0, 0, i))
    row_tbl = jax.ShapeDtypeStruct((PEER_HEADS, rows // LANES, N_KEYS, LANES), F32)
    row_spec = pl.BlockSpec((PEER_HEADS, ROUTE_TM // LANES, N_KEYS, LANES), lambda i: (0, i, 0, 0))
    col_tbl = jax.ShapeDtypeStruct((PEER_HEADS, N_KEYS, rows), BF16)
    return pl.pallas_call(
        _route_kernel,
        grid=(rows // ROUTE_TM,),
        in_specs=[
            pl.BlockSpec((ROUTE_TM, D_MODEL), lambda i: (i, 0)),
            pl.BlockSpec((1, D_MODEL), lambda i: (0, 0)),
            per_batch, per_batch,
            pl.BlockSpec((D_MODEL, 2 * PEER_HEADS * PEER_HALF), lambda i: (0, 0)),
            pl.BlockSpec((2 * PEER_HEADS, N_KEYS, PEER_HALF), lambda i: (0, 0, 0)),
        ],
        out_specs=[pl.BlockSpec((D_MODEL, ROUTE_TM), lambda i: (0, i)), row_spec, row_spec, tbl, tbl],
        out_shape=[jax.ShapeDtypeStruct((D_MODEL, rows), BF16), row_tbl, row_tbl, col_tbl, col_tbl],
        scratch_shapes=[pltpu.VMEM((2 * PEER_HEADS, N_KEYS, ROUTE_TM), F32)],
        compiler_params=_params("arbitrary"),
        name="peer_route",
    )(x2d, g, sc, sh, wq, keys)


DENSE_TM = 512
DENSE_TE = 2048
DENSE_SUB = 512
DENSE_SHARE = 2
DENSE_JC = 32


def _dense_kernel(t_ref, cnt_ref, a_ref, r2_ref, b_ref, eu_ref, evt_ref, x_ref, g2_ref, fg_ref,
                  out_ref, acc_ref, act_ref, g_ref, rt_ref, *, final_norm):
    e = pl.program_id(1)

    @pl.when(e == 0)
    def _():
        acc_ref[...] = jnp.zeros_like(acc_ref)

    tm = t_ref.shape[1]
    zero = jnp.zeros((), BF16)

    def row_tile(ref, h, i):
        return jnp.concatenate([ref[h, c, pl.ds(i, 16, stride=0), :] for c in range(tm // LANES)],
                               axis=1).astype(BF16)

    def tiled(ref, slot, row, k, h):
        return jnp.concatenate([ref[slot, row, k, h]] * (DENSE_JC // 16), axis=0)

    rows_per_sub = DENSE_SUB // N_KEYS
    n_sub = DENSE_TE // DENSE_SUB

    def activations(sb, slot):
        act_ref[slot] = jnp.dot(eu_ref[sb * DENSE_SUB:(sb + 1) * DENSE_SUB, :], t_ref[...],
                                preferred_element_type=F32)

    def gating(sb, slot):
        for ii in range(rows_per_sub):
            i = e * (DENSE_TE // N_KEYS) + sb * rows_per_sub + ii
            for h in range(PEER_HEADS):
                rt_ref[slot, ii, 0, h] = row_tile(cnt_ref, h, i)
                rt_ref[slot, ii, 1, h] = row_tile(a_ref, h, i)
        for pair in range(0, rows_per_sub, DENSE_SHARE):
            for jc in range(N_KEYS // DENSE_JC):
                js = slice(jc * DENSE_JC, (jc + 1) * DENSE_JC)
                w = [None] * DENSE_SHARE
                for h in range(PEER_HEADS):
                    r2 = r2_ref[h, js, :]
                    bb = b_ref[h, js, :]
                    for s in range(DENSE_SHARE):
                        term = jnp.where(r2 < tiled(rt_ref, slot, pair + s, 0, h),
                                         bb * tiled(rt_ref, slot, pair + s, 1, h), zero)
                        w[s] = term if w[s] is None else w[s] + term
                for s in range(DENSE_SHARE):
                    ii = pair + s
                    rows = slice(ii * N_KEYS + jc * DENSE_JC, ii * N_KEYS + (jc + 1) * DENSE_JC)
                    g_ref[slot, rows, :] = _gelu_packed_x2(act_ref[slot, rows, :].astype(BF16)) * w[s]

    def accumulate(sb, slot):
        acc_ref[...] += jnp.dot(evt_ref[sb], g_ref[slot], preferred_element_type=F32)

    for sb in range(n_sub):
        activations(sb, sb)
    for sb in range(n_sub):
        gating(sb, sb)
        accumulate(sb, sb)

    @pl.when(e == pl.num_programs(1) - 1)
    def _():
        y = x_ref[...] + g2_ref[...] * acc_ref[...].T
        if final_norm:
            y = _rms(y, fg_ref[...])
        out_ref[...] = y


def _dense(t, cnt, a, r2, b, eu, evt, x2d, g2, fg, layer, final_norm):
    rows = x2d.shape[0]
    per_b = SEQ // DENSE_TM
    tbl = pl.BlockSpec((PEER_HEADS, N_KEYS, DENSE_TM), lambda i, e: (0, 0, i))
    row_spec = pl.BlockSpec((PEER_HEADS, DENSE_TM // LANES, N_KEYS, LANES), lambda i, e: (0, i, 0, 0))
    return pl.pallas_call(
        functools.partial(_dense_kernel, final_norm=final_norm),
        grid=(rows // DENSE_TM, N_EXPERTS // DENSE_TE),
        in_specs=[
            pl.BlockSpec((D_MODEL, DENSE_TM), lambda i, e: (0, i)),
            row_spec, row_spec, tbl, tbl,
            pl.BlockSpec((None, DENSE_TE, D_MODEL), lambda i, e: (layer, e, 0)),
            pl.BlockSpec((None, DENSE_TE // DENSE_SUB, D_MODEL, DENSE_SUB), lambda i, e: (layer, e, 0, 0)),
            pl.BlockSpec((DENSE_TM, D_MODEL), lambda i, e: (i, 0)),
            pl.BlockSpec((None, 1, D_MODEL), lambda i, e: (i // per_b, 0, 0)),
            pl.BlockSpec((1, D_MODEL), lambda i, e: (0, 0)),
        ],
        out_specs=pl.BlockSpec((DENSE_TM, D_MODEL), lambda i, e: (i, 0)),
        out_shape=jax.ShapeDtypeStruct((rows, D_MODEL), F32),
        scratch_shapes=[pltpu.VMEM((D_MODEL, DENSE_TM), F32),
                        pltpu.VMEM((DENSE_TE // DENSE_SUB, DENSE_SUB, DENSE_TM), F32),
                        pltpu.VMEM((DENSE_TE // DENSE_SUB, DENSE_SUB, DENSE_TM), BF16),
                        pltpu.VMEM((DENSE_TE // DENSE_SUB, DENSE_SUB // N_KEYS, 2, PEER_HEADS, 16, DENSE_TM), BF16)],
        compiler_params=_params("arbitrary", "arbitrary"),
        name="peer_dense",
    )(t, cnt, a, r2, b, eu, evt, x2d, g2, fg)


def _peer(x2d, g, sc, sh, g2, wq, keys, eu_all, evt_all, fg, layer, final_norm):
    wq_b = wq.astype(BF16)
    keys_b = keys.reshape(2 * PEER_HEADS, N_KEYS, PEER_HALF).astype(BF16)
    t, cnt, a, r2, b = _route(x2d, g, sc, sh, wq_b, keys_b)
    return _dense(t, cnt, a, r2, b, eu_all, evt_all, x2d, g2, fg, layer, final_norm)


def kernel(x, c, ctx, c_ctx, ada_w, ada_b, norm1_g, norm2_g, final_g, even_w_in, even_w_out, diff_lambda,
           diff_norm_g, odd_w_in, odd_b_in, sgu_norm_g, sgu_w, sgu_b, odd_w_out, peer_wq, peer_keys, peer_u, peer_v):
    cvec = jnp.concatenate([c, c_ctx[None, :], jnp.zeros((8 - BATCH - 1, D_MODEL), F32)], axis=0)
    mod = _ada(cvec, ada_w, ada_b).reshape(2, 8, 6, D_MODEL)
    row = lambda v: v.reshape(1, -1)
    per_batch = lambda l, j: mod[l, :BATCH, j].reshape(BATCH, 1, D_MODEL)
    fg = row(final_g)

    both = lambda j: jnp.stack([mod[0, :BATCH, j], jnp.broadcast_to(mod[0, BATCH, j], (BATCH, D_MODEL))],
                               axis=1).reshape(BATCH, 2, 1, D_MODEL)
    cos, sa, sb = _rope_tables()
    qkvf = _inproj(x.reshape(BATCH * SEQ, D_MODEL), ctx.reshape(BATCH * CTX_LEN, D_MODEL), row(norm1_g[0]),
                   both(1), both(0), even_w_in[0].astype(BF16), cos, sa, sb)
    qkvf = qkvf.reshape(BATCH, SEQ_ALL, EVEN_IN)
    o = _attn(qkvf, diff_lambda[0], row(diff_norm_g[0]))
    fm = _fourier(qkvf, _seq_dft())
    eu_all = peer_u.astype(BF16)
    evt_all = peer_v.astype(BF16).reshape(2, N_EXPERTS // DENSE_SUB, DENSE_SUB, D_MODEL).transpose(0, 1, 3, 2)
    x2d = x.reshape(BATCH * SEQ, D_MODEL)
    x2d = _outproj(o.reshape(BATCH * SEQ, DIFF_QK), fm.reshape(BATCH * SEQ, FOURIER_W),
                   even_w_out[0].astype(BF16), x2d, per_batch(0, 2))
    x2d = _peer(x2d, row(norm2_g[0]), per_batch(0, 4), per_batch(0, 3), per_batch(0, 5),
                peer_wq[0], peer_keys[0], eu_all, evt_all, fg, 0, False)

    bs = jnp.broadcast_to(sgu_b[0][:, :, None], (SGU_GROUPS, CHUNK, CHUNK))
    x2d = _sgu(x2d, row(norm1_g[1]), per_batch(1, 1), per_batch(1, 0), odd_w_in[0].astype(BF16),
               row(odd_b_in[0]), row(sgu_norm_g[0]), sgu_w[0].astype(BF16), bs, odd_w_out[0].astype(BF16),
               per_batch(1, 2))
    x2d = _peer(x2d, row(norm2_g[1]), per_batch(1, 4), per_batch(1, 3), per_batch(1, 5),
                peer_wq[1], peer_keys[1], eu_all, evt_all, fg, 1, True)
    return x2d.reshape(BATCH, SEQ, D_MODEL)
```
